```python
import jax, jax.numpy as jnp
from jax import lax
import numpy as np

D_MODEL = 1024
BATCH = 2
SEQ = 8192
DEPTH = 2
DEC_BATCH = 32
DEC_SEQ = 64
PAST_LEN = 2048

CHUNK = 64
Q_BLOCK = 128
ROPE_THETA = 10000.0
RMS_EPS = 1e-6
NEG_INF = -1e30
N_HEADS_A = 8
HEAD_DIM_A = 64
N_KV_A = 2
N_IDX = 4
D_IDX = 64
TOPK_MAX = 256
N_HEADS_R = 8
HEAD_DIM_R = 64
D_DECAY_LORA = 64
D_A_LORA = 64
GN_EPS = 64e-5
N_HEADS_M = 4
HEAD_DIM_M = 128
N_MEM = 256
N_BRANCH = 3
D_A = N_HEADS_A * HEAD_DIM_A
D_KV_A = N_KV_A * HEAD_DIM_A
D_R = N_HEADS_R * HEAD_DIM_R
D_M = N_HEADS_M * HEAD_DIM_M
D_SHIFT = 3 * D_R + D_DECAY_LORA + D_A_LORA
IN_SIZES = (D_A, D_KV_A, D_KV_A, N_IDX * D_IDX, D_IDX, N_IDX, D_A, D_SHIFT, D_R, D_M, D_M, N_BRANCH * D_MODEL)
D_IN = 2 * D_A + 2 * D_KV_A + N_IDX * D_IDX + D_IDX + N_IDX + D_SHIFT + D_R + 2 * D_M + N_BRANCH * D_MODEL

kernel_name = "dsa_rwkv7_memory_hybrid_stream_step"


def _rmsnorm(x, g):
    xf = x.astype(jnp.float32)
    y = xf * lax.rsqrt(jnp.mean(xf * xf, axis=-1, keepdims=True) + RMS_EPS)
    return (y * g.astype(jnp.float32)).astype(x.dtype)


def _rotary(x, pos):
    d = x.shape[-1]
    inv = ROPE_THETA ** (-jnp.arange(0, d, 2, dtype=jnp.float32) / d)
    ang = pos.astype(jnp.float32)[:, None] * inv[None, :]
    cos = jnp.cos(ang)[None, :, None, :]
    sin = jnp.sin(ang)[None, :, None, :]
    xf = x.astype(jnp.float32)
    x1, x2 = xf[..., : d // 2], xf[..., d // 2:]
    return jnp.concatenate([x1 * cos - x2 * sin, x1 * sin + x2 * cos], axis=-1).astype(x.dtype)


def _split_cols(p):
    offs = np.cumsum(np.array(IN_SIZES))[:-1].tolist()
    return jnp.split(p, offs, axis=-1)


def _dsa_block(q, qi, wi, qpos, k_all, v_all, ki_all, topk):
    B, Tb, H, dh = q.shape
    kpos = jnp.arange(k_all.shape[1])
    s = jnp.einsum('bthd,bsd->bths', qi, ki_all, preferred_element_type=jnp.float32) * (D_IDX ** -0.5)
    score = jnp.einsum('bth,bths->bts', wi.astype(jnp.float32), jax.nn.relu(s))
    admissible = (kpos[None, :] // CHUNK) <= (qpos[:, None] // CHUNK)
    score = jnp.where(admissible[None], score, NEG_INF)
    _, idx = lax.top_k(score, topk)
    valid = (idx // CHUNK) <= (qpos[None, :, None] // CHUNK)
    take = jax.vmap(lambda a, i: a[i])
    k_sel = take(k_all, idx)
    v_sel = take(v_all, idx)
    qg = q.reshape(B, Tb, N_KV_A, H // N_KV_A, dh)
    logits = jnp.einsum('btjgd,btnjd->btjgn', qg, k_sel, preferred_element_type=jnp.float32) * (dh ** -0.5)
    logits = jnp.where(valid[:, :, None, None, :], logits, NEG_INF)
    p = jax.nn.softmax(logits, axis=-1).astype(v_sel.dtype)
    o = jnp.einsum('btjgn,btnjd->btjgd', p, v_sel)
    return o.reshape(B, Tb, H, dh)


def _dsa(q, qi, wi, qpos, k_all, v_all, ki_all, topk):
    B, T = q.shape[:2]
    qb = Q_BLOCK if T % Q_BLOCK == 0 else T
    nb = T // qb

    def blk(a):
        return jnp.moveaxis(a.reshape((B, nb, qb) + a.shape[2:]), 1, 0)

    o = lax.map(lambda args: _dsa_block(*args, k_all, v_all, ki_all, topk),
                (blk(q), blk(qi), blk(wi), qpos.reshape(nb, qb)))
    return jnp.moveaxis(o, 0, 1).reshape(q.shape)


def _rwkv(shifted, state0, w0, w2, a0, a2, k_k, k_a, r_k, ln_w, ln_b):
    f32 = jnp.float32
    B, T, _ = shifted.shape
    r, k, v, wl, al = jnp.split(shifted, [D_R, 2 * D_R, 3 * D_R, 3 * D_R + D_DECAY_LORA], axis=-1)
    w = -jax.nn.softplus(-(w0 + jnp.tanh(wl) @ w2).astype(f32)) - 0.5
    decay = jnp.exp(-jnp.exp(w))
    a = jax.nn.sigmoid((a0 + al @ a2).astype(f32))
    heads = lambda t: t.astype(f32).reshape(B, T, N_HEADS_R, HEAD_DIM_R)
    kk = heads(k * k_k)
    kk = kk / jnp.maximum(jnp.sqrt(jnp.sum(kk * kk, axis=-1, keepdims=True)), 1e-12)
    k = k.astype(f32) * (1.0 + (a - 1.0) * k_a.astype(f32))
    r, k, v, a, decay = heads(r), heads(k), heads(v), heads(a), heads(decay)

    def step(S, inp):
        rt, wt, kt, vt, at, bt = inp
        sa = jnp.einsum('bhij,bhj->bhi', S, at)
        S = S * wt[:, :, None, :] + sa[..., None] * bt[:, :, None, :] + vt[..., None] * kt[:, :, None, :]
        return S, jnp.einsum('bhij,bhj->bhi', S, rt)

    tm = lambda t: jnp.moveaxis(t, 1, 0)
    S_T, y = lax.scan(step, state0.astype(f32), (tm(r), tm(decay), tm(k), tm(v), tm(-kk), tm(kk * a)))
    y = jnp.moveaxis(y, 0, 1)
    mu = jnp.mean(y, axis=-1, keepdims=True)
    var = jnp.mean(jnp.square(y - mu), axis=-1, keepdims=True)
    y = ((y - mu) * lax.rsqrt(var + GN_EPS)).reshape(B, T, D_R) * ln_w.astype(f32) + ln_b.astype(f32)
    bonus = jnp.sum(r * k * r_k.astype(f32), axis=-1, keepdims=True) * v
    y = y + bonus.reshape(B, T, D_R)
    return y.astype(shifted.dtype), S_T


def _mem_kv(mem, g, w_kv):
    B = mem.shape[0]
    mk, mv = jnp.split(_rmsnorm(mem, g) @ w_kv, 2, axis=-1)
    return (mk.reshape(B, N_MEM, N_HEADS_M, HEAD_DIM_M), mv.reshape(B, N_MEM, N_HEADS_M, HEAD_DIM_M))


def _mem_attn(q, mem_k, mem_v):
    logits = jnp.einsum('bthd,bmhd->bthm', q, mem_k, preferred_element_type=jnp.float32) * (HEAD_DIM_M ** -0.5)
    p = jax.nn.softmax(logits, axis=-1).astype(mem_v.dtype)
    return jnp.einsum('bthm,bmhd->bthd', p, mem_v)


def _layer(x, past_k, past_v, past_ki, wkv0, shift0, mem_k, mem_v,
           norm_g, w_in, mu_shift, w0, w2, a0, a2, k_k, k_a, r_k, ln_w, ln_b,
           w_br_a, w_br_r, w_br_m, w_out):
    B, T, _ = x.shape
    pos0 = past_k.shape[1]
    h = _rmsnorm(x, norm_g)
    q, k, v, qi, ki, wi, za, sh, zb, qm, zm, gates = _split_cols(h @ w_in)
    pos = pos0 + jnp.arange(T)
    q = _rotary(q.reshape(B, T, N_HEADS_A, HEAD_DIM_A), pos)
    k = _rotary(k.reshape(B, T, N_KV_A, HEAD_DIM_A), pos)
    v = v.reshape(B, T, N_KV_A, HEAD_DIM_A)
    qi = _rotary(qi.reshape(B, T, N_IDX, D_IDX), pos)
    ki = _rotary(ki.reshape(B, T, 1, D_IDX), pos)[:, :, 0]
    wi = wi * (N_IDX ** -0.5)
    k_all = jnp.concatenate([past_k.astype(k.dtype), k], axis=1)
    v_all = jnp.concatenate([past_v.astype(v.dtype), v], axis=1)
    ki_all = jnp.concatenate([past_ki.astype(ki.dtype), ki], axis=1)
    topk = min(TOPK_MAX, (pos0 + T) // 4)
    o_a = _dsa(q, qi, wi, pos, k_all, v_all, ki_all, topk).reshape(B, T, D_A) * jax.nn.silu(za)
    prev = jnp.concatenate([shift0.astype(sh.dtype), sh[:, :-1]], axis=1)
    shifted = sh + (prev - sh) * mu_shift
    y_r, wkv_T = _rwkv(shifted, wkv0, w0, w2, a0, a2, k_k, k_a, r_k, ln_w, ln_b)
    o_r = y_r * jax.nn.silu(zb)
    o_m = _mem_attn(qm.reshape(B, T, N_HEADS_M, HEAD_DIM_M), mem_k, mem_v).reshape(B, T, D_M) * jax.nn.silu(zm)
    g_a, g_r, g_m = jnp.split(jax.nn.sigmoid(gates), N_BRANCH, axis=-1)
    merged = g_a * (o_a @ w_br_a) + g_r * (o_r @ w_br_r) + g_m * (o_m @ w_br_m)
    x = x + merged @ w_out
    return x, k, v, ki, wkv_T, sh[:, -1:]


def setup_inputs(seed: int = 0) -> dict:
    key = jax.random.key(seed)
    ks = iter(jax.random.split(key, 48))
    f32 = jnp.float32
    nrm = lambda shape, scale=1.0: scale * jax.random.normal(next(ks), shape, f32)
    unif = lambda shape, lo, hi: jax.random.uniform(next(ks), shape, f32, lo, hi)
    return {
        'x_prompt': nrm((BATCH, SEQ, D_MODEL)),
        'x_sample': nrm((DEC_BATCH, DEC_SEQ, D_MODEL)),
        'mem_prompt': nrm((BATCH, N_MEM, D_MODEL)),
        'cache_k': nrm((DEPTH, DEC_BATCH, PAST_LEN, N_KV_A, HEAD_DIM_A)),
        'cache_v': nrm((DEPTH, DEC_BATCH, PAST_LEN, N_KV_A, HEAD_DIM_A)),
        'cache_kidx': nrm((DEPTH, DEC_BATCH, PAST_LEN, D_IDX)),
        'state_wkv': nrm((DEPTH, DEC_BATCH, N_HEADS_R, HEAD_DIM_R, HEAD_DIM_R), 0.5),
        'state_shift': nrm((DEPTH, DEC_BATCH, 1, D_SHIFT)),
        'cache_mem_k': nrm((DEPTH, DEC_BATCH, N_MEM, N_HEADS_M, HEAD_DIM_M)),
        'cache_mem_v': nrm((DEPTH, DEC_BATCH, N_MEM, N_HEADS_M, HEAD_DIM_M)),
        'norm_g': 1.0 + nrm((DEPTH, D_MODEL), 0.02),
        'w_in': nrm((DEPTH, D_MODEL, D_IN), D_MODEL ** -0.5),
        'mu_shift': unif((DEPTH, D_SHIFT), 0.0, 1.0),
        'w0': unif((DEPTH, D_R), -6.0, 1.0),
        'w2': nrm((DEPTH, D_DECAY_LORA, D_R), 0.5 * D_DECAY_LORA ** -0.5),
        'a0': nrm((DEPTH, D_R), 0.1),
        'a2': nrm((DEPTH, D_A_LORA, D_R), 0.5 * D_A_LORA ** -0.5),
        'k_k': 0.85 + nrm((DEPTH, D_R), 0.02),
        'k_a': 1.0 + nrm((DEPTH, D_R), 0.02),
        'r_k': nrm((DEPTH, N_HEADS_R, HEAD_DIM_R), 0.1),
        'ln_x_w': 1.0 + nrm((DEPTH, D_R), 0.02),
        'ln_x_b': nrm((DEPTH, D_R), 0.02),
        'mem_norm_g': 1.0 + nrm((DEPTH, D_MODEL), 0.02),
        'w_mem_kv': nrm((DEPTH, D_MODEL, 2 * D_M), D_MODEL ** -0.5),
        'w_br_a': nrm((DEPTH, D_A, D_MODEL), D_A ** -0.5),
        'w_br_r': nrm((DEPTH, D_R, D_MODEL), D_R ** -0.5),
        'w_br_m': nrm((DEPTH, D_M, D_MODEL), D_M ** -0.5),
        'w_out': nrm((DEPTH, D_MODEL, D_MODEL), D_MODEL ** -0.5),
        'final_norm_g': 1.0 + nrm((D_MODEL,), 0.02),
    }


def reference(x_prompt, x_sample, mem_prompt, cache_k, cache_v, cache_kidx, state_wkv, state_shift,
              cache_mem_k, cache_mem_v, norm_g, w_in, mu_shift, w0, w2, a0, a2, k_k, k_a, r_k,
              ln_x_w, ln_x_b, mem_norm_g, w_mem_kv, w_br_a, w_br_r, w_br_m, w_out, final_norm_g):
    B = x_prompt.shape[0]
    dt = x_prompt.dtype
    xp, xs = x_prompt, x_sample
    pk, pv, pki, pwkv, psh, pmk, pmv = [], [], [], [], [], [], []
    sk, sv, ski, swkv, ssh = [], [], [], [], []
    for l in range(DEPTH):
        lw = (norm_g[l], w_in[l], mu_shift[l], w0[l], w2[l], a0[l], a2[l], k_k[l], k_a[l], r_k[l],
              ln_x_w[l], ln_x_b[l], w_br_a[l], w_br_r[l], w_br_m[l], w_out[l])
        mk, mv = _mem_kv(mem_prompt, mem_norm_g[l], w_mem_kv[l])
        empty_kv = jnp.zeros((B, 0, N_KV_A, HEAD_DIM_A), dt)
        xp, k_n, v_n, ki_n, wkv_n, sh_n = _layer(
            xp, empty_kv, empty_kv, jnp.zeros((B, 0, D_IDX), dt),
            jnp.zeros((B, N_HEADS_R, HEAD_DIM_R, HEAD_DIM_R), jnp.float32),
            jnp.zeros((B, 1, D_SHIFT), dt), mk, mv, *lw)
        pk.append(k_n); pv.append(v_n); pki.append(ki_n); pwkv.append(wkv_n); psh.append(sh_n)
        pmk.append(mk); pmv.append(mv)
        xs, k_n, v_n, ki_n, wkv_n, sh_n = _layer(
            xs, cache_k[l], cache_v[l], cache_kidx[l], state_wkv[l], state_shift[l],
            cache_mem_k[l], cache_mem_v[l], *lw)
        sk.append(k_n); sv.append(v_n); ski.append(ki_n); swkv.append(wkv_n); ssh.append(sh_n)
    y_prompt = _rmsnorm(xp, final_norm_g)
    y_sample = _rmsnorm(xs, final_norm_g)
    return (y_prompt, y_sample,
            jnp.stack(pk), jnp.stack(pv), jnp.stack(pki), jnp.stack(pwkv), jnp.stack(psh),
            jnp.stack(pmk), jnp.stack(pmv),
            jnp.stack(sk), jnp.stack(sv), jnp.stack(ski), jnp.stack(swkv), jnp.stack(ssh))
```

```python
import functools

import jax
import jax.numpy as jnp
import numpy as np
from jax import lax
from jax.experimental import pallas as pl
from jax.experimental.pallas import tpu as pltpu

F32 = jnp.float32
BF16 = jnp.bfloat16
HIGHEST = lax.Precision.HIGHEST

CHUNK = 64
ROPE_THETA = 10000.0
RMS_EPS = 1e-6
NEG_INF = -1e30
N_HEADS_A = 8
HEAD_DIM_A = 64
N_KV_A = 2
N_IDX = 4
D_IDX = 64
TOPK_MAX = 256
N_HEADS_R = 8
HEAD_DIM_R = 64
D_LORA = 64
GN_EPS = 64e-5
N_HEADS_M = 4
HEAD_DIM_M = 128
N_BRANCH = 3

LANES = 128
VMEM_LIMIT = 48 * 1024 * 1024

_NEG_BITS = int(np.array(NEG_INF, np.float32).view(np.int32))
NEG_KEY = _NEG_BITS ^ 0x7FFFFFFF
SIGN_FLIP = -(2 ** 31)


def _cparams(*sem):
    return pltpu.CompilerParams(dimension_semantics=sem, vmem_limit_bytes=VMEM_LIMIT)


def _rmsnorm_kernel(x_ref, g_ref, o_ref):
    x = x_ref[...]
    y = x * lax.rsqrt(jnp.mean(x * x, axis=-1, keepdims=True) + RMS_EPS)
    o_ref[...] = (y * g_ref[...]).astype(o_ref.dtype)


def _rmsnorm(x2d, g, out_dtype, tm):
    m, d = x2d.shape
    tm = min(tm, m)
    return pl.pallas_call(
        _rmsnorm_kernel,
        grid=(m // tm,),
        in_specs=[pl.BlockSpec((tm, d), lambda i: (i, 0)), pl.BlockSpec((1, d), lambda i: (0, 0))],
        out_specs=pl.BlockSpec((tm, d), lambda i: (i, 0)),
        out_shape=jax.ShapeDtypeStruct((m, d), out_dtype),
        compiler_params=_cparams("parallel"),
        name="rmsnorm",
    )(x2d, g.reshape(1, d))


def _matmul_kernel(x_ref, w_ref, o_ref):
    o_ref[...] = jnp.dot(x_ref[...].astype(BF16), w_ref[...], preferred_element_type=F32)


def _matmul(x2d, w, tm, tn):
    m, k = x2d.shape
    n = w.shape[1]
    tm = min(tm, m)
    return pl.pallas_call(
        _matmul_kernel,
        grid=(n // tn, m // tm),
        in_specs=[pl.BlockSpec((tm, k), lambda j, i: (i, 0)), pl.BlockSpec((k, tn), lambda j, i: (0, j))],
        out_specs=pl.BlockSpec((tm, tn), lambda j, i: (i, j)),
        out_shape=jax.ShapeDtypeStruct((m, n), F32),
        compiler_params=_cparams("parallel", "parallel"),
        name="matmul",
    )(x2d, w)


ATT_COLS = N_HEADS_A * HEAD_DIM_A + 2 * N_KV_A * HEAD_DIM_A + N_IDX * D_IDX + LANES
_ROPE_FULL = (0, 1, 2, 3, 4, 6, 7)
_ROPE_NONE = (5,)
_ROPE_HALF = (8,)


def _attn_proj_kernel(h_ref, w_ref, cos_ref, sa_ref, sb_ref, o_ref):
    p = jnp.dot(h_ref[...].astype(BF16), w_ref[...], preferred_element_type=F32)
    cos, sa, sb = cos_ref[...], sa_ref[...], sb_ref[...]
    lane = lax.broadcasted_iota(jnp.int32, cos.shape, 1)
    for g in range(ATT_COLS // LANES):
        x = p[:, g * LANES:(g + 1) * LANES]
        if g in _ROPE_NONE:
            o_ref[:, g * LANES:(g + 1) * LANES] = x
            continue
        y = x * cos + pltpu.roll(x, LANES - 32, 1) * sa + pltpu.roll(x, 32, 1) * sb
        if g in _ROPE_HALF:
            y = jnp.where(lane < D_IDX, y, x)
        o_ref[:, g * LANES:(g + 1) * LANES] = y


def _rope_tables(pos0, t):
    d = HEAD_DIM_A
    inv = ROPE_THETA ** (-jnp.arange(0, d, 2, dtype=F32) / d)
    ang = (pos0 + jnp.arange(t)).astype(F32)[:, None] * inv[None, :]
    cos, sin = jnp.cos(ang), jnp.sin(ang)
    zero = jnp.zeros_like(sin)
    cos_f = jnp.concatenate([cos] * 4, axis=-1)
    sin_a = jnp.concatenate([-sin, zero] * 2, axis=-1)
    sin_b = jnp.concatenate([zero, sin] * 2, axis=-1)
    return cos_f, sin_a, sin_b


def _attn_proj(h2d, w_att, tables, t, tm):
    m, k = h2d.shape
    nt = t // tm
    tab_spec = pl.BlockSpec((tm, LANES), lambda i: (i % nt, 0))
    return pl.pallas_call(
        _attn_proj_kernel,
        grid=(m // tm,),
        in_specs=[pl.BlockSpec((tm, k), lambda i: (i, 0)), pl.BlockSpec((k, ATT_COLS), lambda i: (0, 0)),
                  tab_spec, tab_spec, tab_spec],
        out_specs=pl.BlockSpec((tm, ATT_COLS), lambda i: (i, 0)),
        out_shape=jax.ShapeDtypeStruct((m, ATT_COLS), F32),
        compiler_params=_cparams("parallel"),
        name="attn_proj",
    )(h2d, w_att, *tables)


def _dsa_kernel(qi_ref, w_ref, q_ref, ki_ref, k_ref, v_ref, o_ref, key_sc, *, tq, tk, pos0, s_real, topk):
    qb = pl.program_id(1)
    q0 = pos0 + qb * tq
    n_adm = jnp.minimum(q0 + tq, s_real)
    n_kt = (n_adm + tk - 1) // tk
    reps = tk // LANES

    row_pos = q0 + lax.broadcasted_iota(jnp.int32, (tq, tk), 0)
    row_chunk = row_pos // CHUNK
    col_iota = lax.broadcasted_iota(jnp.int32, (tq, tk), 1)

    qi = qi_ref[0].reshape(N_IDX * tq, D_IDX)

    def score_body(kt, carry):
        start = pl.multiple_of(kt * tk, tk)
        kit = ki_ref[0, pl.ds(start, tk), :]
        s = lax.dot_general(qi, kit, (((1,), (1,)), ((), ())), preferred_element_type=F32)
        score = jnp.zeros((tq, tk), F32)
        for h in range(N_IDX):
            wh = jnp.concatenate([w_ref[0, h]] * reps, axis=1)
            score = score + jnp.maximum(s[h * tq:(h + 1) * tq], 0.0) * wh
        score = score + 0.0
        kpos = start + col_iota
        adm = (kpos // CHUNK <= row_chunk) & (kpos < s_real)
        score = jnp.where(adm, score, NEG_INF)
        bits = pltpu.bitcast(score, jnp.int32)
        key_sc[kt] = jnp.where(bits < 0, bits ^ 0x7FFFFFFF, bits)
        return carry

    lax.fori_loop(0, n_kt, score_body, 0)

    def count(pred_fn):
        def body(kt, acc):
            hit = jnp.where(pred_fn(key_sc[kt], kt), 1, 0)
            for g in range(reps):
                acc = acc + hit[:, g * LANES:(g + 1) * LANES]
            return acc

        acc = lax.fori_loop(0, n_kt, body, jnp.zeros((tq, LANES), jnp.int32))
        return jnp.sum(acc, axis=1, keepdims=True)

    def bit_body(i, prefix):
        cand = prefix | lax.shift_left(jnp.int32(1), 31 - i)
        c = cand ^ SIGN_FLIP
        cnt = count(lambda key, kt: key >= c)
        return jnp.where(cnt >= topk, cand, prefix)

    prefix = lax.fori_loop(0, 32, bit_body, jnp.zeros((tq, 1), jnp.int32))
    thr = jnp.maximum(prefix ^ SIGN_FLIP, NEG_KEY)
    cnt_gt = count(lambda key, kt: key > thr)
    cnt_ge = count(lambda key, kt: key >= thr)
    need = topk - cnt_gt
    real_thr = thr != NEG_KEY
    has_excess = jnp.max(jnp.where(real_thr & (cnt_ge > topk), 1, 0)) > 0

    n_idx_bits = max(1, int(np.ceil(np.log2(key_sc.shape[0] * tk))))

    def tie_search():
        def jbody(i, jq):
            cand = jq | lax.shift_left(jnp.int32(1), n_idx_bits - 1 - i)
            cnt = count(lambda key, kt: (key == thr) & (kt * tk + col_iota < cand))
            return jnp.where(cnt < need, cand, jq)

        return lax.fori_loop(0, n_idx_bits, jbody, jnp.zeros((tq, 1), jnp.int32))

    j_cut = lax.cond(has_excess, tie_search, lambda: jnp.full((tq, 1), 2 ** 30, jnp.int32))
    j_cut = jnp.where(real_thr, j_cut, -1)

    group = N_HEADS_A // N_KV_A
    rows = group * tq
    qs = [q_ref[0, j * group:(j + 1) * group].reshape(rows, HEAD_DIM_A) for j in range(N_KV_A)]

    def attn_body(kt, carry):
        start = pl.multiple_of(kt * tk, tk)
        key = key_sc[kt]
        sel = (key > thr) | ((key == thr) & (start + col_iota <= j_cut))
        bias = jnp.where(sel, 0.0, NEG_INF).astype(F32)
        bias = jnp.concatenate([bias] * group, axis=0)
        out = []
        for j in range(N_KV_A):
            m_i, l_i, acc = carry[j]
            kj = k_ref[0, j, pl.ds(start, tk), :]
            vj = v_ref[0, j, pl.ds(start, tk), :]
            logits = lax.dot_general(qs[j], kj, (((1,), (1,)), ((), ())), preferred_element_type=F32)
            logits = logits + bias
            m_new = jnp.maximum(m_i, jnp.max(logits, axis=1, keepdims=True))
            alpha = jnp.exp(m_i - m_new)
            p = jnp.exp(logits - m_new)
            l_new = l_i * alpha + jnp.sum(p, axis=1, keepdims=True)
            acc = acc * alpha + jnp.dot(p.astype(BF16), vj, preferred_element_type=F32)
            out.append((m_new, l_new, acc))
        return tuple(out)

    init = tuple((jnp.full((rows, 1), 0.1 * NEG_INF, F32), jnp.zeros((rows, 1), F32),
                  jnp.zeros((rows, HEAD_DIM_A), F32)) for _ in range(N_KV_A))
    res = lax.fori_loop(0, n_kt, attn_body, init)
    for j in range(N_KV_A):
        _, l_i, acc = res[j]
        o = acc / l_i
        for g in range(group):
            o_ref[0, j * group + g] = o[g * tq:(g + 1) * tq]


def _dsa(qi_hm, wrep, q_hm, ki_all, k_hm, v_hm, *, pos0, s_real, tq, tk):
    b, _, t, _ = q_hm.shape
    s_pad = ki_all.shape[1]
    topk = min(TOPK_MAX, s_real // 4)
    kern = functools.partial(_dsa_kernel, tq=tq, tk=tk, pos0=pos0, s_real=s_real, topk=topk)
    return pl.pallas_call(
        kern,
        grid=(b, t // tq),
        in_specs=[
            pl.BlockSpec((1, N_IDX, tq, D_IDX), lambda bi, qi: (bi, 0, qi, 0)),
            pl.BlockSpec((1, N_IDX, tq, LANES), lambda bi, qi: (bi, 0, qi, 0)),
            pl.BlockSpec((1, N_HEADS_A, tq, HEAD_DIM_A), lambda bi, qi: (bi, 0, qi, 0)),
            pl.BlockSpec((1, s_pad, D_IDX), lambda bi, qi: (bi, 0, 0)),
            pl.BlockSpec((1, N_KV_A, s_pad, HEAD_DIM_A), lambda bi, qi: (bi, 0, 0, 0)),
            pl.BlockSpec((1, N_KV_A, s_pad, HEAD_DIM_A), lambda bi, qi: (bi, 0, 0, 0)),
        ],
        out_specs=pl.BlockSpec((1, N_HEADS_A, tq, HEAD_DIM_A), lambda bi, qi: (bi, 0, qi, 0)),
        out_shape=jax.ShapeDtypeStruct((b, N_HEADS_A, t, HEAD_DIM_A), F32),
        scratch_shapes=[pltpu.VMEM((s_pad // tk, tq, tk), jnp.int32)],
        compiler_params=_cparams("parallel", "arbitrary"),
        name="dsa",
    )(qi_hm, wrep, q_hm, ki_all, k_hm, v_hm)


RW_SLOTS = 3 * N_HEADS_R + 2


def _hdot(a, b):
    return jnp.dot(a, b, preferred_element_type=F32, precision=HIGHEST)


def _hdot_nt(a, b):
    return lax.dot_general(a, b, (((1,), (1,)), ((), ())), preferred_element_type=F32, precision=HIGHEST)


def _hdot_tn(a, b):
    return lax.dot_general(a, b, (((0,), (0,)), ((), ())), preferred_element_type=F32, precision=HIGHEST)


def _rwkv_kernel(sh_ref, pv_ref, mu_ref, w0_ref, w2_ref, a0_ref, a2_ref, kk_ref, ka_ref, rk_ref,
                 lnw_ref, lnb_ref, s0_ref, y_ref, sout_ref, st_sc, *, bb, c):
    ci = pl.program_id(1)
    nh = N_HEADS_R

    @pl.when(ci == 0)
    def _():
        st_sc[...] = s0_ref[...]

    ri = lax.broadcasted_iota(jnp.int32, (c, c), 0)
    cj = lax.broadcasted_iota(jnp.int32, (c, c), 1)
    tri_incl = jnp.where(ri >= cj, 1.0, 0.0).astype(F32)
    strict = ri > cj
    incl = ri >= cj
    eye = jnp.where(ri == cj, 1.0, 0.0).astype(F32)

    def mixed(bi, slot):
        s = sh_ref[bi, slot]
        return s + (pv_ref[bi, slot] - s) * mu_ref[slot]

    for bi in range(bb):
        tw = jnp.tanh(mixed(bi, 3 * nh))
        al = mixed(bi, 3 * nh + 1)
        for h in range(nh):
            r = mixed(bi, h)
            k = mixed(bi, nh + h)
            v = mixed(bi, 2 * nh + h)
            w_raw = w0_ref[h] + jnp.dot(tw.astype(BF16), w2_ref[h].astype(BF16), preferred_element_type=F32)
            lw = -jnp.exp(-jax.nn.softplus(-w_raw) - 0.5)
            a = jax.nn.sigmoid(
                a0_ref[h] + jnp.dot(al.astype(BF16), a2_ref[h].astype(BF16), preferred_element_type=F32))
            kkv = k * kk_ref[h]
            kk = kkv / jnp.maximum(jnp.sqrt(jnp.sum(kkv * kkv, axis=-1, keepdims=True)), 1e-12)
            k2 = k * (1.0 + (a - 1.0) * ka_ref[h])
            alpha = -kk
            beta = kk * a

            cum = _hdot(tri_incl, lw)
            p_fwd = jnp.exp(cum)
            p_inv = jnp.exp(-cum)
            at = alpha * jnp.exp(cum - lw)
            bt = beta * p_inv
            kt = k2 * p_inv
            rt = r * p_fwd
            ab = _hdot_nt(at, bt)
            ak = _hdot_nt(at, kt)
            rb = _hdot_nt(rt, bt)
            rk = _hdot_nt(rt, kt)
            l_ab = jnp.where(strict, ab, 0.0)
            l_ak = jnp.where(strict, ak, 0.0)
            m_rb = jnp.where(incl, rb, 0.0)
            m_rk = jnp.where(incl, rk, 0.0)
            x = eye + l_ab
            lp = l_ab
            n = 1
            while 2 * n < c:
                lp = _hdot(lp, lp)
                x = x + _hdot(x, lp)
                n *= 2
            g = bi * nh + h
            s_mat = st_sc[g]
            u = _hdot(x, _hdot_nt(at, s_mat) + _hdot(l_ak, v))
            y = _hdot_nt(rt, s_mat) + _hdot(m_rb, u) + _hdot(m_rk, v)
            p_last = p_fwd[c - 1:c, :]
            st_sc[g] = s_mat * p_last + _hdot_tn(u, bt * p_last) + _hdot_tn(v, kt * p_last)

            mean = jnp.mean(y, axis=-1, keepdims=True)
            var = jnp.mean(jnp.square(y - mean), axis=-1, keepdims=True)
            yn = (y - mean) * lax.rsqrt(var + GN_EPS) * lnw_ref[h] + lnb_ref[h]
            bonus = jnp.sum(r * k2 * rk_ref[h], axis=-1, keepdims=True) * v
            y_ref[bi, h] = yn + bonus

    @pl.when(ci == pl.num_programs(1) - 1)
    def _():
        sout_ref[...] = st_sc[...].reshape(sout_ref.shape)


def _rwkv(sh_hm, prev_hm, mu_hm, w0, w2, a0, a2, k_k, k_a, r_k, ln_w, ln_b, state0, *, bb, c):
    b, _, t, n = sh_hm.shape
    nh = N_HEADS_R
    g = bb * nh
    vec = lambda a: a.reshape(nh, 1, n)
    lora = lambda a: a.reshape(D_LORA, nh, n).transpose(1, 0, 2)
    full = lambda shape: pl.BlockSpec(shape, lambda bi, ci: (0,) * len(shape))
    seq_spec = pl.BlockSpec((bb, RW_SLOTS, c, n), lambda bi, ci: (bi, 0, ci, 0))
    kern = functools.partial(_rwkv_kernel, bb=bb, c=c)
    s0 =state0.reshape(b * nh, n, n)
    y, s_out = pl.pallas_call(
        kern,
        grid=(b // bb, t // c),
        in_specs=[seq_spec, seq_spec, full((RW_SLOTS, 1, n)), full((nh, 1, n)), full((nh, D_LORA, n)),
                  full((nh, 1, n)), full((nh, D_LORA, n)), full((nh, 1, n)), full((nh, 1, n)), full((nh, 1, n)),
                  full((nh, 1, n)), full((nh, 1, n)),
                  pl.BlockSpec((g, n, n), lambda bi, ci: (bi, 0, 0))],
        out_specs=[pl.BlockSpec((bb, nh, c, n), lambda bi, ci: (bi, 0, ci, 0)),
                   pl.BlockSpec((g, n, n), lambda bi, ci: (bi, 0, 0))],
        out_shape=[jax.ShapeDtypeStruct((b, nh, t, n), F32), jax.ShapeDtypeStruct((b * nh, n, n), F32)],
        scratch_shapes=[pltpu.VMEM((g, n, n), F32)],
        compiler_params=_cparams("parallel", "arbitrary"),
        name="rwkv",
    )(sh_hm, prev_hm, mu_hm, vec(w0), lora(w2), vec(a0), lora(a2), vec(k_k), vec(k_a), vec(r_k),
      vec(ln_w), vec(ln_b), s0)
    return y, s_out.reshape(b, nh, n, n)


def _mem_attn_kernel(q_ref, mk_ref, mv_ref, o_ref):
    scale = HEAD_DIM_M ** -0.5
    for h in range(N_HEADS_M):
        sl = slice(h * HEAD_DIM_M, (h + 1) * HEAD_DIM_M)
        q = q_ref[0, :, sl].astype(BF16)
        logits = lax.dot_general(q, mk_ref[0, :, sl], (((1,), (1,)), ((), ())), preferred_element_type=F32) * scale
        e = jnp.exp(logits - jnp.max(logits, axis=1, keepdims=True))
        p = e / jnp.sum(e, axis=1, keepdims=True)
        o_ref[0, :, sl] = jnp.dot(p.astype(BF16), mv_ref[0, :, sl], preferred_element_type=F32)


def _mem_attn(qm, mk, mv, tm):
    b, t, d = qm.shape
    nm = mk.shape[1]
    return pl.pallas_call(
        _mem_attn_kernel,
        grid=(b, t // tm),
        in_specs=[pl.BlockSpec((1, tm, d), lambda bi, i: (bi, i, 0)),
                  pl.BlockSpec((1, nm, d), lambda bi, i: (bi, 0, 0)),
                  pl.BlockSpec((1, nm, d), lambda bi, i: (bi, 0, 0))],
        out_specs=pl.BlockSpec((1, tm, d), lambda bi, i: (bi, i, 0)),
        out_shape=jax.ShapeDtypeStruct((b, t, d), F32),
        compiler_params=_cparams("parallel", "parallel"),
        name="mem_attn",
    )(qm, mk, mv)


def _merge_kernel(oa_ref, yr_ref, om_ref, zg_ref, x_ref, wa_ref, wr_ref, wm_ref, wo_ref, g_ref, xo_ref, ho_ref):
    d_br = oa_ref.shape[1]
    d = x_ref.shape[1]

    def branch(o, z, w_ref):
        return jnp.dot((o * jax.nn.silu(z)).astype(BF16), w_ref[...], preferred_element_type=F32)

    pa = branch(oa_ref[...], zg_ref[:, 0:d_br], wa_ref)
    pr = branch(yr_ref[...], zg_ref[:, d_br:2 * d_br], wr_ref)
    pm = branch(om_ref[...], zg_ref[:, 2 * d_br:3 * d_br], wm_ref)
    g0 = 3 * d_br
    merged = (jax.nn.sigmoid(zg_ref[:, g0:g0 + d]) * pa
              + jax.nn.sigmoid(zg_ref[:, g0 + d:g0 + 2 * d]) * pr
              + jax.nn.sigmoid(zg_ref[:, g0 + 2 * d:g0 + 3 * d]) * pm)
    xn = x_ref[...] + jnp.dot(merged.astype(BF16), wo_ref[...], preferred_element_type=F32)
    xo_ref[...] = xn
    y = xn * lax.rsqrt(jnp.mean(xn * xn, axis=-1, keepdims=True) + RMS_EPS)
    ho_ref[...] = y * g_ref[...]


def _merge(oa, yr, om, zg, x2d, wa, wr, wm, wo, g_next, tm):
    m, d = x2d.shape
    d_br = oa.shape[1]
    tm = min(tm, m)
    row = lambda w: pl.BlockSpec((tm, w), lambda i: (i, 0))
    full = lambda a: pl.BlockSpec(a.shape, lambda i: (0, 0))
    g2 = g_next.reshape(1, d)
    return pl.pallas_call(
        _merge_kernel,
        grid=(m // tm,),
        in_specs=[row(d_br), row(d_br), row(d_br), row(zg.shape[1]), row(d),
                  full(wa), full(wr), full(wm), full(wo), full(g2)],
        out_specs=[row(d), row(d)],
        out_shape=[jax.ShapeDtypeStruct((m, d), F32), jax.ShapeDtypeStruct((m, d), F32)],
        compiler_params=_cparams("parallel"),
        name="merge",
    )(oa, yr, om, zg, x2d, wa, wr, wm, wo, g2)


def _round_up(x, m):
    return (x + m - 1) // m * m


def _split_w_in(w_in):
    d_a = N_HEADS_A * HEAD_DIM_A
    d_kv = N_KV_A * HEAD_DIM_A
    d_r = N_HEADS_R * HEAD_DIM_R
    d_m = N_HEADS_M * HEAD_DIM_M
    d_shift = 3 * d_r + 2 * D_LORA
    d_model = w_in.shape[0]
    sizes = (d_a, d_kv, d_kv, N_IDX * D_IDX, D_IDX, N_IDX, d_a, d_shift, d_r, d_m, d_m, N_BRANCH * d_model)
    offs = np.cumsum((0,) + sizes)
    col = lambda i: w_in[:, offs[i]:offs[i + 1]]
    pad = jnp.zeros((d_model, LANES - D_IDX - N_IDX), w_in.dtype)
    w_att = jnp.concatenate([col(0), col(1), col(2), col(3), col(4), col(5), pad], axis=1).astype(BF16)
    w_sh = col(7).astype(BF16)
    w_qm = col(9).astype(BF16)
    w_zg = jnp.concatenate([col(6), col(8), col(10), col(11)], axis=1).astype(BF16)
    return w_att, w_sh, w_qm, w_zg


def _layer(x2d, h2d, b, t, pos0, past_k, past_v, past_ki, wkv0, shift0, mem_k, mem_v, lw, g_next, cfg):
    (w_att, w_sh, w_qm, w_zg, mu_shift, w0, w2, a0, a2, k_k, k_a, r_k, ln_w, ln_b, w_br_a, w_br_r, w_br_m,
     w_out) = lw
    tm = cfg["tm"]
    d_a = N_HEADS_A * HEAD_DIM_A
    d_kv = N_KV_A * HEAD_DIM_A
    d_r = N_HEADS_R * HEAD_DIM_R
    d_m = N_HEADS_M * HEAD_DIM_M

    tables = _rope_tables(pos0, t)
    pa = _attn_proj(h2d, w_att, tables, t, min(tm, t)).reshape(b, t, ATT_COLS)
    sh = _matmul(h2d, w_sh, tm, w_sh.shape[1]).reshape(b, t, -1)
    qm = _matmul(h2d, w_qm, tm, w_qm.shape[1]).reshape(b, t, d_m)
    zg = _matmul(h2d, w_zg, tm, cfg["tn_zg"])

    q = pa[..., :d_a]
    k_new = pa[..., d_a:d_a + d_kv].reshape(b, t, N_KV_A, HEAD_DIM_A)
    v_new = pa[..., d_a + d_kv:d_a + 2 * d_kv].reshape(b, t, N_KV_A, HEAD_DIM_A)
    o_qi = d_a + 2 * d_kv
    qi = pa[..., o_qi:o_qi + N_IDX * D_IDX]
    ki_new = pa[..., o_qi + N_IDX * D_IDX:o_qi + N_IDX * D_IDX + D_IDX]
    wi = pa[..., o_qi + N_IDX * D_IDX + D_IDX:o_qi + N_IDX * D_IDX + D_IDX + N_IDX] * (N_IDX ** -0.5)

    s_real = pos0 + t
    tq, tk = cfg["tq"], cfg["tk"]
    s_pad = _round_up(s_real, tk)
    pad_s = lambda a: jnp.pad(a, ((0, 0), (0, s_pad - s_real)) + ((0, 0),) * (a.ndim - 2))
    k_all = pad_s(jnp.concatenate([past_k, k_new], axis=1)).astype(BF16).transpose(0, 2, 1, 3)
    v_all = pad_s(jnp.concatenate([past_v, v_new], axis=1)).astype(BF16).transpose(0, 2, 1, 3)
    ki_all = pad_s(jnp.concatenate([past_ki, ki_new], axis=1)).astype(BF16)
    q_hm = (q * (HEAD_DIM_A ** -0.5)).astype(BF16).reshape(b, t, N_HEADS_A, HEAD_DIM_A).transpose(0, 2, 1, 3)
    qi_hm = (qi * (D_IDX ** -0.5)).astype(BF16).reshape(b, t, N_IDX, D_IDX).transpose(0, 2, 1, 3)
    wrep = jnp.broadcast_to(wi.transpose(0, 2, 1)[..., None], (b, N_IDX, t, LANES))
    o_a = _dsa(qi_hm, wrep, q_hm, ki_all, k_all, v_all, pos0=pos0, s_real=s_real, tq=tq, tk=tk)
    o_a = o_a.transpose(0, 2, 1, 3).reshape(b * t, d_a)

    prev = jnp.concatenate([shift0, sh[:, :-1]], axis=1)
    to_hm = lambda a: a.reshape(b, t, RW_SLOTS, HEAD_DIM_R).transpose(0, 2, 1, 3)
    y_r, wkv_t = _rwkv(to_hm(sh), to_hm(prev), mu_shift.reshape(RW_SLOTS, 1, HEAD_DIM_R), w0, w2, a0, a2,
                       k_k, k_a, r_k, ln_w, ln_b, wkv0, bb=cfg["bb"], c=CHUNK)
    y_r = y_r.transpose(0, 2, 1, 3).reshape(b * t, d_r)

    o_m = _mem_attn(qm, mem_k, mem_v, min(tm, t)).reshape(b * t, d_m)

    x_new, h_next = _merge(o_a, y_r, o_m, zg, x2d, w_br_a, w_br_r, w_br_m, w_out, g_next, cfg["tm_merge"])
    return x_new, h_next, k_new, v_new, ki_new, wkv_t, sh[:, -1:]


PROMPT_CFG = dict(tm=512, tn_zg=2304, tq=128, tk=512, bb=2, tm_merge=256)
SAMPLE_CFG = dict(tm=512, tn_zg=2304, tq=64, tk=768, bb=2, tm_merge=256)


def kernel(x_prompt, x_sample, mem_prompt, cache_k, cache_v, cache_kidx, state_wkv, state_shift, cache_mem_k,
           cache_mem_v, norm_g, w_in, mu_shift, w0, w2, a0, a2, k_k, k_a, r_k, ln_x_w, ln_x_b, mem_norm_g,
           w_mem_kv, w_br_a, w_br_r, w_br_m, w_out, final_norm_g):
    depth = w_in.shape[0]
    bp, tp, d = x_prompt.shape
    bs, ts, _ = x_sample.shape
    n_mem = mem_prompt.shape[1]
    d_m = N_HEADS_M * HEAD_DIM_M
    past = cache_k.shape[2]

    xp = x_prompt.reshape(bp * tp, d)
    xs = x_sample.reshape(bs * ts, d)
    hp = _rmsnorm(xp, norm_g[0], F32, 512)
    hs = _rmsnorm(xs, norm_g[0], F32, 512)
    mem2d = mem_prompt.reshape(bp * n_mem, d)

    empty_kv = jnp.zeros((bp, 0, N_KV_A, HEAD_DIM_A), F32)
    empty_ki = jnp.zeros((bp, 0, D_IDX), F32)
    zero_state = jnp.zeros((bp, N_HEADS_R, HEAD_DIM_R, HEAD_DIM_R), F32)
    zero_shift = jnp.zeros((bp, 1, mu_shift.shape[1]), F32)

    outs_p = [[] for _ in range(7)]
    outs_s = [[] for _ in range(5)]
    for l in range(depth):
        g_next = norm_g[l + 1] if l + 1 < depth else final_norm_g
        lw = _split_w_in(w_in[l]) + (mu_shift[l], w0[l], w2[l], a0[l], a2[l], k_k[l], k_a[l], r_k[l], ln_x_w[l],
                                     ln_x_b[l], w_br_a[l].astype(BF16), w_br_r[l].astype(BF16),
                                     w_br_m[l].astype(BF16), w_out[l].astype(BF16))
        mem_h = _rmsnorm(mem2d, mem_norm_g[l], F32, 512)
        mkv = _matmul(mem_h, w_mem_kv[l].astype(BF16), 512, 2 * d_m).reshape(bp, n_mem, 2 * d_m)
        mk, mv = mkv[..., :d_m], mkv[..., d_m:]
        xp, hp, k_n, v_n, ki_n, wkv_n, sh_n = _layer(
            xp, hp, bp, tp, 0, empty_kv, empty_kv, empty_ki, zero_state, zero_shift,
            mk.astype(BF16), mv.astype(BF16), lw, g_next, PROMPT_CFG)
        for lst, val in zip(outs_p, (k_n, v_n, ki_n, wkv_n, sh_n,
                                     mk.reshape(bp, n_mem, N_HEADS_M, HEAD_DIM_M),
                                     mv.reshape(bp, n_mem, N_HEADS_M, HEAD_DIM_M))):
            lst.append(val)
        xs, hs, k_n, v_n, ki_n, wkv_n, sh_n = _layer(
            xs, hs, bs, ts, past, cache_k[l], cache_v[l], cache_kidx[l], state_wkv[l], state_shift[l],
            cache_mem_k[l].reshape(bs, n_mem, d_m).astype(BF16), cache_mem_v[l].reshape(bs, n_mem, d_m).astype(BF16),
            lw, g_next, SAMPLE_CFG)
        for lst, val in zip(outs_s, (k_n, v_n, ki_n, wkv_n, sh_n)):
            lst.append(val)

    y_prompt = hp.reshape(bp, tp, d)
    y_sample = hs.reshape(bs, ts, d)
    return (y_prompt, y_sample) + tuple(jnp.stack(o) for o in outs_p) + tuple(jnp.stack(o) for o in outs_s)
```

```python
import functools

import jax
import jax.numpy as jnp
import numpy as np
from jax import lax
from jax.experimental import pallas as pl
from jax.experimental.pallas import tpu as pltpu

F32 = jnp.float32
BF16 = jnp.bfloat16

CHUNK = 64
ROPE_THETA = 10000.0
RMS_EPS = 1e-6
NEG_INF = -1e30
N_HEADS_A = 8
HEAD_DIM_A = 64
N_KV_A = 2
N_IDX = 4
D_IDX = 64
TOPK_MAX = 256
N_HEADS_R = 8
HEAD_DIM_R = 64
D_LORA = 64
GN_EPS = 64e-5
N_HEADS_M = 4
HEAD_DIM_M = 128
N_BRANCH = 3

LANES = 128
VMEM_LIMIT = 48 * 1024 * 1024

_NEG_BITS = int(np.array(NEG_INF, np.float32).view(np.int32))
NEG_KEY = _NEG_BITS ^ 0x7FFFFFFF
SIGN_FLIP = -(2 ** 31)


def _cparams(*sem):
    return pltpu.CompilerParams(dimension_semantics=sem, vmem_limit_bytes=VMEM_LIMIT)


def _rmsnorm_kernel(x_ref, g_ref, o_ref):
    x = x_ref[...]
    y = x * lax.rsqrt(jnp.mean(x * x, axis=-1, keepdims=True) + RMS_EPS)
    o_ref[...] = (y * g_ref[...]).astype(o_ref.dtype)


def _rmsnorm(x2d, g, out_dtype, tm):
    m, d = x2d.shape
    tm = min(tm, m)
    return pl.pallas_call(
        _rmsnorm_kernel,
        grid=(m // tm,),
        in_specs=[pl.BlockSpec((tm, d), lambda i: (i, 0)), pl.BlockSpec((1, d), lambda i: (0, 0))],
        out_specs=pl.BlockSpec((tm, d), lambda i: (i, 0)),
        out_shape=jax.ShapeDtypeStruct((m, d), out_dtype),
        compiler_params=_cparams("parallel"),
        name="rmsnorm",
    )(x2d, g.reshape(1, d))


def _matmul_kernel(x_ref, w_ref, o_ref):
    o_ref[...] = jnp.dot(x_ref[...].astype(BF16), w_ref[...], preferred_element_type=F32)


def _matmul(x2d, w, tm, tn):
    m, k = x2d.shape
    n = w.shape[1]
    tm = min(tm, m)
    return pl.pallas_call(
        _matmul_kernel,
        grid=(n // tn, m // tm),
        in_specs=[pl.BlockSpec((tm, k), lambda j, i: (i, 0)), pl.BlockSpec((k, tn), lambda j, i: (0, j))],
        out_specs=pl.BlockSpec((tm, tn), lambda j, i: (i, j)),
        out_shape=jax.ShapeDtypeStruct((m, n), F32),
        compiler_params=_cparams("parallel", "parallel"),
        name="matmul",
    )(x2d, w)


ATT_COLS = N_HEADS_A * HEAD_DIM_A + 2 * N_KV_A * HEAD_DIM_A + N_IDX * D_IDX + LANES
_ROPE_FULL = (0, 1, 2, 3, 4, 6, 7)
_ROPE_NONE = (5,)
_ROPE_HALF = (8,)


def _attn_proj_kernel(h_ref, w_ref, cos_ref, sa_ref, sb_ref, o_ref):
    p = jnp.dot(h_ref[...].astype(BF16), w_ref[...], preferred_element_type=F32)
    cos, sa, sb = cos_ref[...], sa_ref[...], sb_ref[...]
    lane = lax.broadcasted_iota(jnp.int32, cos.shape, 1)
    for g in range(ATT_COLS // LANES):
        x = p[:, g * LANES:(g + 1) * LANES]
        if g in _ROPE_NONE:
            o_ref[:, g * LANES:(g + 1) * LANES] = x
            continue
        y = x * cos + pltpu.roll(x, LANES - 32, 1) * sa + pltpu.roll(x, 32, 1) * sb
        if g in _ROPE_HALF:
            y = jnp.where(lane < D_IDX, y, x)
        o_ref[:, g * LANES:(g + 1) * LANES] = y


def _rope_tables(pos0, t):
    d = HEAD_DIM_A
    inv = ROPE_THETA ** (-jnp.arange(0, d, 2, dtype=F32) / d)
    ang = (pos0 + jnp.arange(t)).astype(F32)[:, None] * inv[None, :]
    cos, sin = jnp.cos(ang), jnp.sin(ang)
    zero = jnp.zeros_like(sin)
    cos_f = jnp.concatenate([cos] * 4, axis=-1)
    sin_a = jnp.concatenate([-sin, zero] * 2, axis=-1)
    sin_b = jnp.concatenate([zero, sin] * 2, axis=-1)
    return cos_f, sin_a, sin_b


def _attn_proj(h2d, w_att, tables, t, tm):
    m, k = h2d.shape
    nt = t // tm
    tab_spec = pl.BlockSpec((tm, LANES), lambda i: (i % nt, 0))
    return pl.pallas_call(
        _attn_proj_kernel,
        grid=(m // tm,),
        in_specs=[pl.BlockSpec((tm, k), lambda i: (i, 0)), pl.BlockSpec((k, ATT_COLS), lambda i: (0, 0)),
                  tab_spec, tab_spec, tab_spec],
        out_specs=pl.BlockSpec((tm, ATT_COLS), lambda i: (i, 0)),
        out_shape=jax.ShapeDtypeStruct((m, ATT_COLS), F32),
        compiler_params=_cparams("parallel"),
        name="attn_proj",
    )(h2d, w_att, *tables)


def _dsa_kernel(qi_ref, w_ref, q_ref, ki_ref, k_ref, v_ref, o_ref, key_sc, *, tq, tk, pos0, s_real, topk):
    qb = pl.program_id(1)
    q0 = pos0 + qb * tq
    n_adm = jnp.minimum(q0 + tq, s_real)
    n_kt = (n_adm + tk - 1) // tk
    reps = tk // LANES

    row_pos = q0 + lax.broadcasted_iota(jnp.int32, (tq, tk), 0)
    row_chunk = row_pos // CHUNK
    col_iota = lax.broadcasted_iota(jnp.int32, (tq, tk), 1)

    qi = qi_ref[0].reshape(N_IDX * tq, D_IDX)

    def score_body(kt, carry):
        start = pl.multiple_of(kt * tk, tk)
        kit = ki_ref[0, pl.ds(start, tk), :]
        s = lax.dot_general(qi, kit, (((1,), (1,)), ((), ())), preferred_element_type=F32)
        score = jnp.zeros((tq, tk), F32)
        for h in range(N_IDX):
            wh = jnp.concatenate([w_ref[0, h]] * reps, axis=1)
            score = score + jnp.maximum(s[h * tq:(h + 1) * tq], 0.0) * wh
        score = score + 0.0
        kpos = start + col_iota
        adm = (kpos // CHUNK <= row_chunk) & (kpos < s_real)
        score = jnp.where(adm, score, NEG_INF)
        bits = pltpu.bitcast(score, jnp.int32)
        key_sc[kt] = jnp.where(bits < 0, bits ^ 0x7FFFFFFF, bits)
        return carry

    lax.fori_loop(0, n_kt, score_body, 0)

    def count(pred_fn):
        def body(kt, acc):
            hit = jnp.where(pred_fn(key_sc[kt], kt), 1, 0)
            for g in range(reps):
                acc = acc + hit[:, g * LANES:(g + 1) * LANES]
            return acc

        acc = lax.fori_loop(0, n_kt, body, jnp.zeros((tq, LANES), jnp.int32))
        return jnp.sum(acc, axis=1, keepdims=True)

    def bit_body(i, prefix):
        cand = prefix | lax.shift_left(jnp.int32(1), 31 - i)
        c = cand ^ SIGN_FLIP
        cnt = count(lambda key, kt: key >= c)
        return jnp.where(cnt >= topk, cand, prefix)

    prefix = lax.fori_loop(0, 32, bit_body, jnp.zeros((tq, 1), jnp.int32))
    thr = jnp.maximum(prefix ^ SIGN_FLIP, NEG_KEY)
    cnt_gt = count(lambda key, kt: key > thr)
    cnt_ge = count(lambda key, kt: key >= thr)
    need = topk - cnt_gt
    real_thr = thr != NEG_KEY
    has_excess = jnp.max(jnp.where(real_thr & (cnt_ge > topk), 1, 0)) > 0

    n_idx_bits = max(1, int(np.ceil(np.log2(key_sc.shape[0] * tk))))

    def tie_search():
        def jbody(i, jq):
            cand = jq | lax.shift_left(jnp.int32(1), n_idx_bits - 1 - i)
            cnt = count(lambda key, kt: (key == thr) & (kt * tk + col_iota < cand))
            return jnp.where(cnt < need, cand, jq)

        return lax.fori_loop(0, n_idx_bits, jbody, jnp.zeros((tq, 1), jnp.int32))

    j_cut = lax.cond(has_excess, tie_search, lambda: jnp.full((tq, 1), 2 ** 30, jnp.int32))
    j_cut = jnp.where(real_thr, j_cut, -1)

    group = N_HEADS_A // N_KV_A
    rows = group * tq
    qs = [q_ref[0, j * group:(j + 1) * group].reshape(rows, HEAD_DIM_A) for j in range(N_KV_A)]

    def attn_body(kt, carry):
        start = pl.multiple_of(kt * tk, tk)
        key = key_sc[kt]
        sel = (key > thr) | ((key == thr) & (start + col_iota <= j_cut))
        bias = jnp.where(sel, 0.0, NEG_INF).astype(F32)
        bias = jnp.concatenate([bias] * group, axis=0)
        out = []
        for j in range(N_KV_A):
            m_i, l_i, acc = carry[j]
            kj = k_ref[0, j, pl.ds(start, tk), :]
            vj = v_ref[0, j, pl.ds(start, tk), :]
            logits = lax.dot_general(qs[j], kj, (((1,), (1,)), ((), ())), preferred_element_type=F32)
            logits = logits + bias
            m_new = jnp.maximum(m_i, jnp.max(logits, axis=1, keepdims=True))
            alpha = jnp.exp(m_i - m_new)
            p = jnp.exp(logits - m_new)
            l_new = l_i * alpha + jnp.sum(p, axis=1, keepdims=True)
            acc = acc * alpha + jnp.dot(p.astype(BF16), vj, preferred_element_type=F32)
            out.append((m_new, l_new, acc))
        return tuple(out)

    init = tuple((jnp.full((rows, 1), 0.1 * NEG_INF, F32), jnp.zeros((rows, 1), F32),
                  jnp.zeros((rows, HEAD_DIM_A), F32)) for _ in range(N_KV_A))
    res = lax.fori_loop(0, n_kt, attn_body, init)
    for j in range(N_KV_A):
        _, l_i, acc = res[j]
        o = acc / l_i
        for g in range(group):
            o_ref[0, j * group + g] = o[g * tq:(g + 1) * tq]


def _dsa(qi_hm, wrep, q_hm, ki_all, k_hm, v_hm, *, pos0, s_real, tq, tk):
    b, _, t, _ = q_hm.shape
    s_pad = ki_all.shape[1]
    topk = min(TOPK_MAX, s_real // 4)
    kern = functools.partial(_dsa_kernel, tq=tq, tk=tk, pos0=pos0, s_real=s_real, topk=topk)
    return pl.pallas_call(
        kern,
        grid=(b, t // tq),
        in_specs=[
            pl.BlockSpec((1, N_IDX, tq, D_IDX), lambda bi, qi: (bi, 0, qi, 0)),
            pl.BlockSpec((1, N_IDX, tq, LANES), lambda bi, qi: (bi, 0, qi, 0)),
            pl.BlockSpec((1, N_HEADS_A, tq, HEAD_DIM_A), lambda bi, qi: (bi, 0, qi, 0)),
            pl.BlockSpec((1, s_pad, D_IDX), lambda bi, qi: (bi, 0, 0)),
            pl.BlockSpec((1, N_KV_A, s_pad, HEAD_DIM_A), lambda bi, qi: (bi, 0, 0, 0)),
            pl.BlockSpec((1, N_KV_A, s_pad, HEAD_DIM_A), lambda bi, qi: (bi, 0, 0, 0)),
        ],
        out_specs=pl.BlockSpec((1, N_HEADS_A, tq, HEAD_DIM_A), lambda bi, qi: (bi, 0, qi, 0)),
        out_shape=jax.ShapeDtypeStruct((b, N_HEADS_A, t, HEAD_DIM_A), F32),
        scratch_shapes=[pltpu.VMEM((s_pad // tk, tq, tk), jnp.int32)],
        compiler_params=_cparams("parallel", "arbitrary"),
        name="dsa",
    )(qi_hm, wrep, q_hm, ki_all, k_hm, v_hm)


D_R = N_HEADS_R * HEAD_DIM_R
D_SHIFT = 3 * D_R + 2 * D_LORA
N_PAIR = D_R // LANES
N_VEC = 8


def _split2(x):
    hi = x.astype(BF16)
    lo = (x - hi.astype(F32)).astype(BF16)
    return hi, lo


def _mm(a, b, ca, cb):
    return lax.dot_general(a, b, (((ca,), (cb,)), ((), ())), preferred_element_type=F32)


def _rwkv_kernel(sh_ref, sh0_ref, mu_ref, vec_ref, wl_ref, s0_ref, y_ref, sout_ref, st_sc, row_sc, *, bb, c):
    ci = pl.program_id(1)
    n = HEAD_DIM_R
    c2 = 2 * c

    @pl.when(ci == 0)
    def _():
        st_sc[...] = s0_ref[...]
        row_sc[...] = sh0_ref[...]

    head0 = lax.broadcasted_iota(jnp.int32, (c, LANES), 1) < n
    r2 = lax.broadcasted_iota(jnp.int32, (c2, c2), 0)
    q2 = lax.broadcasted_iota(jnp.int32, (c2, c2), 1)
    same_head = (r2 // c) == (q2 // c)
    strict = same_head & (r2 > q2)
    incl = same_head & (r2 >= q2)
    eye2 = jnp.where(r2 == q2, 1.0, 0.0).astype(F32)
    rc = lax.broadcasted_iota(jnp.int32, (c, c), 0)
    cc = lax.broadcasted_iota(jnp.int32, (c, c), 1)
    tri = jnp.where(rc >= cc, 1.0, 0.0).astype(BF16)
    tri3 = jnp.concatenate([tri, tri, tri], axis=1)
    la = lax.broadcasted_iota(jnp.int32, (LANES, LANES), 0)
    lb = lax.broadcasted_iota(jnp.int32, (LANES, LANES), 1)
    seg = jnp.where((la // n) == (lb // n), 1.0, 0.0).astype(BF16)
    seg2 = jnp.concatenate([seg, seg], axis=0)
    first_row = lax.broadcasted_iota(jnp.int32, (c, D_SHIFT), 0) == 0

    def seg_sum(x):
        xh, xl = _split2(x)
        return jnp.dot(jnp.concatenate([xh, xl], axis=1), seg2, preferred_element_type=F32)

    def vec(i, sl=slice(None)):
        return vec_ref[i:i + 1, sl]

    n_double = 0
    while (2 << n_double) < c:
        n_double += 1

    def stack(zz):
        return jnp.concatenate([jnp.where(head0, zz, 0.0), jnp.where(head0, 0.0, zz)], axis=0)

    chains = [(bi, p) for bi in range(bb) for p in range(N_PAIR)]
    states = [st_sc[bi, p] for bi, p in chains]

    prep = []
    for bi in range(bb):
        sh = sh_ref[bi]
        prev = jnp.where(first_row, row_sc[bi], pltpu.roll(sh, 1, 0))
        x = sh + (prev - sh) * mu_ref[...]
        r = x[:, :D_R]
        k = x[:, D_R:2 * D_R]
        v = x[:, 2 * D_R:3 * D_R]
        xl = x[:, 3 * D_R:]
        lora_in = jnp.where(head0, jnp.tanh(xl), xl)
        wa = jnp.dot(lora_in.astype(BF16), wl_ref[...], preferred_element_type=F32)
        w_raw = vec(0) + wa[:, :D_R]
        a = jax.nn.sigmoid(vec(1) + wa[:, D_R:])
        lw = -jnp.exp(-jax.nn.softplus(-w_raw) - 0.5)
        kkv = k * vec(2)
        kk2 = kkv * kkv
        ss = jnp.concatenate([seg_sum(kk2[:, p * LANES:(p + 1) * LANES]) for p in range(N_PAIR)], axis=1)
        kk = kkv / jnp.maximum(jnp.sqrt(ss), 1e-12)
        k2 = k * (1.0 + (a - 1.0) * vec(3))
        l_hi = lw.astype(BF16)
        l_r1 = lw - l_hi.astype(F32)
        l_mid = l_r1.astype(BF16)
        l_lo = (l_r1 - l_mid.astype(F32)).astype(BF16)
        cum = jnp.dot(tri3, jnp.concatenate([l_hi, l_mid, l_lo], axis=0), preferred_element_type=F32)
        p_fwd = jnp.exp(cum)
        p_inv = jnp.exp(-cum)
        prep.append(dict(v=v, p_fwd=p_fwd, at=-kk * jnp.exp(cum - lw), bt=kk * a * p_inv, kt=k2 * p_inv,
                         rt=r * p_fwd, rk_prod=r * k2 * vec(4)))
    for bi in range(bb):
        row_sc[bi] = sh_ref[bi, c - 1:c, :]

    ops = []
    for bi, p in chains:
        sl = slice(p * LANES, (p + 1) * LANES)
        d = prep[bi]
        b_st, k_st, v_st = stack(d["bt"][:, sl]), stack(d["kt"][:, sl]), stack(d["v"][:, sl])
        ar = jnp.concatenate([stack(d["at"][:, sl]), stack(d["rt"][:, sl])], axis=0).astype(BF16)
        bk = jnp.concatenate([b_st, k_st], axis=0).astype(BF16)
        p_last = d["p_fwd"][c - 1:c, sl]
        ops.append(dict(ar=ar, bk=bk, v_b=v_st.astype(BF16), p_last=p_last,
                        b_dec=(b_st * p_last).astype(BF16), k_dec=(k_st * p_last).astype(BF16)))
    grams = [_mm(o["ar"], o["bk"], 1, 1) for o in ops]
    l_ab = [jnp.where(strict, g[:c2, :c2], 0.0) for g in grams]
    l_ak = [jnp.where(strict, g[:c2, c2:], 0.0).astype(BF16) for g in grams]
    m_rb = [jnp.where(incl, g[c2:, :c2], 0.0).astype(BF16) for g in grams]
    m_rk = [jnp.where(incl, g[c2:, c2:], 0.0).astype(BF16) for g in grams]
    inv = [eye2 + l for l in l_ab]
    lp = l_ab
    for _ in range(n_double):
        lpb = [l.astype(BF16) for l in lp]
        lp = [_mm(l, l, 1, 0) for l in lpb]
        inv = [x + _mm(x.astype(BF16), l.astype(BF16), 1, 0) for x, l in zip(inv, lp)]

    s_b = [s.astype(BF16) for s in states]
    z = [_mm(o["ar"][:c2], sb, 1, 1) + _mm(la, o["v_b"], 1, 0) for o, sb, la in zip(ops, s_b, l_ak)]
    u_b = [_mm(x.astype(BF16), zz.astype(BF16), 1, 0).astype(BF16) for x, zz in zip(inv, z)]
    y_st = [_mm(o["ar"][c2:], sb, 1, 1) + _mm(mb, u, 1, 0) + _mm(mk, o["v_b"], 1, 0)
            for o, sb, mb, mk, u in zip(ops, s_b, m_rb, m_rk, u_b)]
    new_states = [s * o["p_last"] + _mm(u, o["b_dec"], 0, 0) + _mm(o["v_b"], o["k_dec"], 0, 0)
                  for s, o, u in zip(states, ops, u_b)]
    for (bi, p), s_new, ys in zip(chains, new_states, y_st):
        sl = slice(p * LANES, (p + 1) * LANES)
        st_sc[bi, p] = s_new
        y = ys[:c] + ys[c:]
        dev = y - seg_sum(y) * (1.0 / n)
        var = seg_sum(dev * dev) * (1.0 / n)
        yn = dev * lax.rsqrt(var + GN_EPS) * vec(5, sl) + vec(6, sl)
        y_ref[bi, :, sl] = yn + seg_sum(prep[bi]["rk_prod"][:, sl]) * prep[bi]["v"][:, sl]

    @pl.when(ci == pl.num_programs(1) - 1)
    def _():
        sout_ref[...] = st_sc[...]


def _rwkv(sh, shift0, mu, w0, w2, a0, a2, k_k, k_a, r_k, ln_w, ln_b, state0, *, bb, c):
    b, t, _ = sh.shape
    n = HEAD_DIM_R
    eye = jnp.eye(2, dtype=F32)
    s0 = jnp.einsum('bphij,hg->bphigj', state0.reshape(b, N_PAIR, 2, n, n), eye).reshape(b, N_PAIR, LANES, LANES)
    vecs = jnp.stack([w0, a0, k_k, k_a, r_k.reshape(D_R), ln_w, ln_b, jnp.zeros_like(w0)])
    zero = jnp.zeros((D_LORA, D_R), F32)
    w_lora = jnp.concatenate([jnp.concatenate([w2, zero], axis=1), jnp.concatenate([zero, a2], axis=1)],
                             axis=0).astype(BF16)
    full = lambda shape: pl.BlockSpec(shape, lambda bi, ci: (0,) * len(shape))
    st_spec = pl.BlockSpec((bb, N_PAIR, LANES, LANES), lambda bi, ci: (bi, 0, 0, 0))
    y, s_out = pl.pallas_call(
        functools.partial(_rwkv_kernel, bb=bb, c=c),
        grid=(b // bb, t // c),
        in_specs=[pl.BlockSpec((bb, c, D_SHIFT), lambda bi, ci: (bi, ci, 0)),
                  pl.BlockSpec((bb, 1, D_SHIFT), lambda bi, ci: (bi, 0, 0)),
                  full((1, D_SHIFT)), full((N_VEC, D_R)), full((2 * D_LORA, 2 * D_R)), st_spec],
        out_specs=[pl.BlockSpec((bb, c, D_R), lambda bi, ci: (bi, ci, 0)), st_spec],
        out_shape=[jax.ShapeDtypeStruct((b, t, D_R), F32),
                   jax.ShapeDtypeStruct((b, N_PAIR, LANES, LANES), F32)],
        scratch_shapes=[pltpu.VMEM((bb, N_PAIR, LANES, LANES), F32), pltpu.VMEM((bb, 1, D_SHIFT), F32)],
        compiler_params=_cparams("parallel", "arbitrary"),
        name="rwkv",
    )(sh, shift0, mu.reshape(1, D_SHIFT), vecs, w_lora, s0)
    s_t = jnp.einsum('bphigj,hg->bphij', s_out.reshape(b, N_PAIR, 2, n, 2, n), eye)
    return y, s_t.reshape(b, N_HEADS_R, n, n)


def _mem_attn_kernel(q_ref, mk_ref, mv_ref, o_ref):
    scale = HEAD_DIM_M ** -0.5
    for h in range(N_HEADS_M):
        sl = slice(h * HEAD_DIM_M, (h + 1) * HEAD_DIM_M)
        q = q_ref[0, :, sl].astype(BF16)
        logits = lax.dot_general(q, mk_ref[0, :, sl], (((1,), (1,)), ((), ())), preferred_element_type=F32) * scale
        e = jnp.exp(logits - jnp.max(logits, axis=1, keepdims=True))
        p = e / jnp.sum(e, axis=1, keepdims=True)
        o_ref[0, :, sl] = jnp.dot(p.astype(BF16), mv_ref[0, :, sl], preferred_element_type=F32)


def _mem_attn(qm, mk, mv, tm):
    b, t, d = qm.shape
    nm = mk.shape[1]
    return pl.pallas_call(
        _mem_attn_kernel,
        grid=(b, t // tm),
        in_specs=[pl.BlockSpec((1, tm, d), lambda bi, i: (bi, i, 0)),
                  pl.BlockSpec((1, nm, d), lambda bi, i: (bi, 0, 0)),
                  pl.BlockSpec((1, nm, d), lambda bi, i: (bi, 0, 0))],
        out_specs=pl.BlockSpec((1, tm, d), lambda bi, i: (bi, i, 0)),
        out_shape=jax.ShapeDtypeStruct((b, t, d), F32),
        compiler_params=_cparams("parallel", "parallel"),
        name="mem_attn",
    )(qm, mk, mv)


def _merge_kernel(oa_ref, yr_ref, om_ref, zg_ref, x_ref, wa_ref, wr_ref, wm_ref, wo_ref, g_ref, xo_ref, ho_ref):
    d_br = oa_ref.shape[1]
    d = x_ref.shape[1]

    def branch(o, z, w_ref):
        return jnp.dot((o * jax.nn.silu(z)).astype(BF16), w_ref[...], preferred_element_type=F32)

    pa = branch(oa_ref[...], zg_ref[:, 0:d_br], wa_ref)
    pr = branch(yr_ref[...], zg_ref[:, d_br:2 * d_br], wr_ref)
    pm = branch(om_ref[...], zg_ref[:, 2 * d_br:3 * d_br], wm_ref)
    g0 = 3 * d_br
    merged = (jax.nn.sigmoid(zg_ref[:, g0:g0 + d]) * pa
              + jax.nn.sigmoid(zg_ref[:, g0 + d:g0 + 2 * d]) * pr
              + jax.nn.sigmoid(zg_ref[:, g0 + 2 * d:g0 + 3 * d]) * pm)
    xn = x_ref[...] + jnp.dot(merged.astype(BF16), wo_ref[...], preferred_element_type=F32)
    xo_ref[...] = xn
    y = xn * lax.rsqrt(jnp.mean(xn * xn, axis=-1, keepdims=True) + RMS_EPS)
    ho_ref[...] = y * g_ref[...]


def _merge(oa, yr, om, zg, x2d, wa, wr, wm, wo, g_next, tm):
    m, d = x2d.shape
    d_br = oa.shape[1]
    tm = min(tm, m)
    row = lambda w: pl.BlockSpec((tm, w), lambda i: (i, 0))
    full = lambda a: pl.BlockSpec(a.shape, lambda i: (0, 0))
    g2 = g_next.reshape(1, d)
    return pl.pallas_call(
        _merge_kernel,
        grid=(m // tm,),
        in_specs=[row(d_br), row(d_br), row(d_br), row(zg.shape[1]), row(d),
                  full(wa), full(wr), full(wm), full(wo), full(g2)],
        out_specs=[row(d), row(d)],
        out_shape=[jax.ShapeDtypeStruct((m, d), F32), jax.ShapeDtypeStruct((m, d), F32)],
        compiler_params=_cparams("parallel"),
        name="merge",
    )(oa, yr, om, zg, x2d, wa, wr, wm, wo, g2)


def _round_up(x, m):
    return (x + m - 1) // m * m


def _split_w_in(w_in):
    d_a = N_HEADS_A * HEAD_DIM_A
    d_kv = N_KV_A * HEAD_DIM_A
    d_r = N_HEADS_R * HEAD_DIM_R
    d_m = N_HEADS_M * HEAD_DIM_M
    d_shift = 3 * d_r + 2 * D_LORA
    d_model = w_in.shape[0]
    sizes = (d_a, d_kv, d_kv, N_IDX * D_IDX, D_IDX, N_IDX, d_a, d_shift, d_r, d_m, d_m, N_BRANCH * d_model)
    offs = np.cumsum((0,) + sizes)
    col = lambda i: w_in[:, offs[i]:offs[i + 1]]
    pad = jnp.zeros((d_model, LANES - D_IDX - N_IDX), w_in.dtype)
    w_att = jnp.concatenate([col(0), col(1), col(2), col(3), col(4), col(5), pad], axis=1).astype(BF16)
    w_sh = col(7).astype(BF16)
    w_qm = col(9).astype(BF16)
    w_zg = jnp.concatenate([col(6), col(8), col(10), col(11)], axis=1).astype(BF16)
    return w_att, w_sh, w_qm, w_zg


def _layer(x2d, h2d, b, t, pos0, past_k, past_v, past_ki, wkv0, shift0, mem_k, mem_v, lw, g_next, cfg):
    (w_att, w_sh, w_qm, w_zg, mu_shift, w0, w2, a0, a2, k_k, k_a, r_k, ln_w, ln_b, w_br_a, w_br_r, w_br_m,
     w_out) = lw
    tm = cfg["tm"]
    d_a = N_HEADS_A * HEAD_DIM_A
    d_kv = N_KV_A * HEAD_DIM_A
    d_r = N_HEADS_R * HEAD_DIM_R
    d_m = N_HEADS_M * HEAD_DIM_M

    tables = _rope_tables(pos0, t)
    pa = _attn_proj(h2d, w_att, tables, t, min(tm, t)).reshape(b, t, ATT_COLS)
    sh = _matmul(h2d, w_sh, tm, w_sh.shape[1]).reshape(b, t, -1)
    qm = _matmul(h2d, w_qm, tm, w_qm.shape[1]).reshape(b, t, d_m)
    zg = _matmul(h2d, w_zg, tm, cfg["tn_zg"])

    q = pa[..., :d_a]
    k_new = pa[..., d_a:d_a + d_kv].reshape(b, t, N_KV_A, HEAD_DIM_A)
    v_new = pa[..., d_a + d_kv:d_a + 2 * d_kv].reshape(b, t, N_KV_A, HEAD_DIM_A)
    o_qi = d_a + 2 * d_kv
    qi = pa[..., o_qi:o_qi + N_IDX * D_IDX]
    ki_new = pa[..., o_qi + N_IDX * D_IDX:o_qi + N_IDX * D_IDX + D_IDX]
    wi = pa[..., o_qi + N_IDX * D_IDX + D_IDX:o_qi + N_IDX * D_IDX + D_IDX + N_IDX] * (N_IDX ** -0.5)

    s_real = pos0 + t
    tq, tk = cfg["tq"], cfg["tk"]
    s_pad = _round_up(s_real, tk)
    pad_s = lambda a: jnp.pad(a, ((0, 0), (0, s_pad - s_real)) + ((0, 0),) * (a.ndim - 2))
    k_all = pad_s(jnp.concatenate([past_k, k_new], axis=1)).astype(BF16).transpose(0, 2, 1, 3)
    v_all = pad_s(jnp.concatenate([past_v, v_new], axis=1)).astype(BF16).transpose(0, 2, 1, 3)
    ki_all = pad_s(jnp.concatenate([past_ki, ki_new], axis=1)).astype(BF16)
    q_hm = (q * (HEAD_DIM_A ** -0.5)).astype(BF16).reshape(b, t, N_HEADS_A, HEAD_DIM_A).transpose(0, 2, 1, 3)
    qi_hm = (qi * (D_IDX ** -0.5)).astype(BF16).reshape(b, t, N_IDX, D_IDX).transpose(0, 2, 1, 3)
    wrep = jnp.broadcast_to(wi.transpose(0, 2, 1)[..., None], (b, N_IDX, t, LANES))
    o_a = _dsa(qi_hm, wrep, q_hm, ki_all, k_all, v_all, pos0=pos0, s_real=s_real, tq=tq, tk=tk)
    o_a = o_a.transpose(0, 2, 1, 3).reshape(b * t, d_a)

    y_r, wkv_t = _rwkv(sh, shift0, mu_shift, w0, w2, a0, a2, k_k, k_a, r_k, ln_w, ln_b, wkv0,
                       bb=cfg["bb"], c=CHUNK)
    y_r = y_r.reshape(b * t, d_r)

    o_m = _mem_attn(qm, mem_k, mem_v, min(tm, t)).reshape(b * t, d_m)

    x_new, h_next = _merge(o_a, y_r, o_m, zg, x2d, w_br_a, w_br_r, w_br_m, w_out, g_next, cfg["tm_merge"])
    return x_new, h_next, k_new, v_new, ki_new, wkv_t, sh[:, -1:]


PROMPT_CFG = dict(tm=512, tn_zg=2304, tq=128, tk=512, bb=2, tm_merge=256)
SAMPLE_CFG = dict(tm=512, tn_zg=2304, tq=64, tk=768, bb=2, tm_merge=256)


def kernel(x_prompt, x_sample, mem_prompt, cache_k, cache_v, cache_kidx, state_wkv, state_shift, cache_mem_k,
           cache_mem_v, norm_g, w_in, mu_shift, w0, w2, a0, a2, k_k, k_a, r_k, ln_x_w, ln_x_b, mem_norm_g,
           w_mem_kv, w_br_a, w_br_r, w_br_m, w_out, final_norm_g):
    depth = w_in.shape[0]
    bp, tp, d = x_prompt.shape
    bs, ts, _ = x_sample.shape
    n_mem = mem_prompt.shape[1]
    d_m = N_HEADS_M * HEAD_DIM_M
    past = cache_k.shape[2]

    xp = x_prompt.reshape(bp * tp, d)
    xs = x_sample.reshape(bs * ts, d)
    hp = _rmsnorm(xp, norm_g[0], F32, 512)
    hs = _rmsnorm(xs, norm_g[0], F32, 512)
    mem2d = mem_prompt.reshape(bp * n_mem, d)

    empty_kv = jnp.zeros((bp, 0, N_KV_A, HEAD_DIM_A), F32)
    empty_ki = jnp.zeros((bp, 0, D_IDX), F32)
    zero_state = jnp.zeros((bp, N_HEADS_R, HEAD_DIM_R, HEAD_DIM_R), F32)
    zero_shift = jnp.zeros((bp, 1, mu_shift.shape[1]), F32)

    outs_p = [[] for _ in range(7)]
    outs_s = [[] for _ in range(5)]
    for l in range(depth):
        g_next = norm_g[l + 1] if l + 1 < depth else final_norm_g
        lw = _split_w_in(w_in[l]) + (mu_shift[l], w0[l], w2[l], a0[l], a2[l], k_k[l], k_a[l], r_k[l], ln_x_w[l],
                                     ln_x_b[l], w_br_a[l].astype(BF16), w_br_r[l].astype(BF16),
                                     w_br_m[l].astype(BF16), w_out[l].astype(BF16))
        mem_h = _rmsnorm(mem2d, mem_norm_g[l], F32, 512)
        mkv = _matmul(mem_h, w_mem_kv[l].astype(BF16), 512, 2 * d_m).reshape(bp, n_mem, 2 * d_m)
        mk, mv = mkv[..., :d_m], mkv[..., d_m:]
        xp, hp, k_n, v_n, ki_n, wkv_n, sh_n = _layer(
            xp, hp, bp, tp, 0, empty_kv, empty_kv, empty_ki, zero_state, zero_shift,
            mk.astype(BF16), mv.astype(BF16), lw, g_next, PROMPT_CFG)
        for lst, val in zip(outs_p, (k_n, v_n, ki_n, wkv_n, sh_n,
                                     mk.reshape(bp, n_mem, N_HEADS_M, HEAD_DIM_M),
                                     mv.reshape(bp, n_mem, N_HEADS_M, HEAD_DIM_M))):
            lst.append(val)
        xs, hs, k_n, v_n, ki_n, wkv_n, sh_n = _layer(
            xs, hs, bs, ts, past, cache_k[l], cache_v[l], cache_kidx[l], state_wkv[l], state_shift[l],
            cache_mem_k[l].reshape(bs, n_mem, d_m).astype(BF16), cache_mem_v[l].reshape(bs, n_mem, d_m).astype(BF16),
            lw, g_next, SAMPLE_CFG)
        for lst, val in zip(outs_s, (k_n, v_n, ki_n, wkv_n, sh_n)):
            lst.append(val)

    y_prompt = hp.reshape(bp, tp, d)
    y_sample = hs.reshape(bs, ts, d)
    return (y_prompt, y_sample) + tuple(jnp.stack(o) for o in outs_p) + tuple(jnp.stack(o) for o in outs_s)
```

```python
import functools

import jax
import jax.numpy as jnp
import numpy as np
from jax import lax
from jax.experimental import pallas as pl
from jax.experimental.pallas import tpu as pltpu

F32 = jnp.float32
BF16 = jnp.bfloat16

CHUNK = 64
ROPE_THETA = 10000.0
RMS_EPS = 1e-6
NEG_INF = -1e30
N_HEADS_A = 8
HEAD_DIM_A = 64
N_KV_A = 2
N_IDX = 4
D_IDX = 64
TOPK_MAX = 256
N_HEADS_R = 8
HEAD_DIM_R = 64
D_LORA = 64
GN_EPS = 64e-5
N_HEADS_M = 4
HEAD_DIM_M = 128
N_BRANCH = 3

LANES = 128
VMEM_LIMIT = 48 * 1024 * 1024

_NEG_BITS = int(np.array(NEG_INF, np.float32).view(np.int32))
NEG_KEY = _NEG_BITS ^ 0x7FFFFFFF
SIGN_FLIP = -(2 ** 31)


def _cparams(*sem):
    return pltpu.CompilerParams(dimension_semantics=sem, vmem_limit_bytes=VMEM_LIMIT)


def _rmsnorm_kernel(x_ref, g_ref, o_ref):
    x = x_ref[...]
    y = x * lax.rsqrt(jnp.mean(x * x, axis=-1, keepdims=True) + RMS_EPS)
    o_ref[...] = (y * g_ref[...]).astype(o_ref.dtype)


def _rmsnorm(x2d, g, out_dtype, tm):
    m, d = x2d.shape
    tm = min(tm, m)
    return pl.pallas_call(
        _rmsnorm_kernel,
        grid=(m // tm,),
        in_specs=[pl.BlockSpec((tm, d), lambda i: (i, 0)), pl.BlockSpec((1, d), lambda i: (0, 0))],
        out_specs=pl.BlockSpec((tm, d), lambda i: (i, 0)),
        out_shape=jax.ShapeDtypeStruct((m, d), out_dtype),
        compiler_params=_cparams("parallel"),
        name="rmsnorm",
    )(x2d, g.reshape(1, d))


def _matmul_kernel(x_ref, w_ref, o_ref):
    o_ref[...] = jnp.dot(x_ref[...].astype(BF16), w_ref[...], preferred_element_type=F32)


def _matmul(x2d, w, tm, tn):
    m, k = x2d.shape
    n = w.shape[1]
    tm = min(tm, m)
    return pl.pallas_call(
        _matmul_kernel,
        grid=(n // tn, m // tm),
        in_specs=[pl.BlockSpec((tm, k), lambda j, i: (i, 0)), pl.BlockSpec((k, tn), lambda j, i: (0, j))],
        out_specs=pl.BlockSpec((tm, tn), lambda j, i: (i, j)),
        out_shape=jax.ShapeDtypeStruct((m, n), F32),
        compiler_params=_cparams("parallel", "parallel"),
        name="matmul",
    )(x2d, w)


D_A = N_HEADS_A * HEAD_DIM_A
D_KV = N_KV_A * HEAD_DIM_A
K_OFF = D_A
V_OFF = K_OFF + D_KV
QI_OFF = V_OFF + D_KV
KI_OFF = QI_OFF + N_IDX * D_IDX
W_OFF = KI_OFF + D_IDX
ATT_COLS = KI_OFF + LANES
_G_K = K_OFF // LANES
_G_V = V_OFF // LANES
_G_KI = KI_OFF // LANES


def _attn_proj_kernel(h_ref, w_ref, cos_ref, sa_ref, sb_ref, o_ref, kb_ref, vb_ref, ki2_ref):
    p = jnp.dot(h_ref[...].astype(BF16), w_ref[...], preferred_element_type=F32)
    cos, sa, sb = cos_ref[...], sa_ref[...], sb_ref[...]
    lane = lax.broadcasted_iota(jnp.int32, cos.shape, 1)
    for g in range(ATT_COLS // LANES):
        x = p[:, g * LANES:(g + 1) * LANES]
        if g == _G_V:
            o_ref[:, g * LANES:(g + 1) * LANES] = x
            vb_ref[...] = x.astype(BF16)
            continue
        y = x * cos + pltpu.roll(x, LANES - 32, 1) * sa + pltpu.roll(x, 32, 1) * sb
        if g == _G_KI:
            y = jnp.where(lane < D_IDX, y, x)
            ki2_ref[...] = jnp.where(lane < D_IDX, y, pltpu.roll(y, D_IDX, 1)).astype(BF16)
        if g == _G_K:
            kb_ref[...] = y.astype(BF16)
        o_ref[:, g * LANES:(g + 1) * LANES] = y


def _rope_tables(pos0, t):
    d = HEAD_DIM_A
    inv = ROPE_THETA ** (-jnp.arange(0, d, 2, dtype=F32) / d)
    ang = (pos0 + jnp.arange(t)).astype(F32)[:, None] * inv[None, :]
    cos, sin = jnp.cos(ang), jnp.sin(ang)
    zero = jnp.zeros_like(sin)
    cos_f = jnp.concatenate([cos] * 4, axis=-1)
    sin_a = jnp.concatenate([-sin, zero] * 2, axis=-1)
    sin_b = jnp.concatenate([zero, sin] * 2, axis=-1)
    return cos_f, sin_a, sin_b


def _attn_proj(h2d, w_att, tables, t, tm):
    m, k = h2d.shape
    nt = t // tm
    tab_spec = pl.BlockSpec((tm, LANES), lambda i: (i % nt, 0))
    lane_spec = pl.BlockSpec((tm, LANES), lambda i: (i, 0))
    lane_shape = jax.ShapeDtypeStruct((m, LANES), BF16)
    return pl.pallas_call(
        _attn_proj_kernel,
        grid=(m // tm,),
        in_specs=[pl.BlockSpec((tm, k), lambda i: (i, 0)), pl.BlockSpec((k, ATT_COLS), lambda i: (0, 0)),
                  tab_spec, tab_spec, tab_spec],
        out_specs=[pl.BlockSpec((tm, ATT_COLS), lambda i: (i, 0)), lane_spec, lane_spec, lane_spec],
        out_shape=[jax.ShapeDtypeStruct((m, ATT_COLS), F32), lane_shape, lane_shape, lane_shape],
        compiler_params=_cparams("parallel"),
        name="attn_proj",
    )(h2d, w_att, *tables)


I16 = jnp.int16
I16_MIN = -(2 ** 15)
I16_ROWS = 16
I32_ROWS = 8
LOG2E = 1.4426950408889634


def _tree(parts, op):
    while len(parts) > 1:
        parts = [op(parts[i], parts[i + 1]) if i + 1 < len(parts) else parts[i] for i in range(0, len(parts), 2)]
    return parts[0]


def _fold_rows(x, rows, op):
    return _tree([x[g * rows:(g + 1) * rows] for g in range(x.shape[0] // rows)], op)


def _dsa_kernel(pa_ref, ki_ref, k_ref, vt_ref, o_ref, key_sc, hi_sc, lo_sc, *, tq, tk, pos0, t_real, s_real, topk):
    qb = pl.program_id(1)
    q0 = pos0 + qb * tq
    n_adm = jnp.minimum(q0 + tq, s_real)
    n_kt = (n_adm + tk - 1) // tk
    hd = HEAD_DIM_A
    group = N_HEADS_A // N_KV_A
    cols = group * tq

    q_pos = q0 + lax.broadcasted_iota(jnp.int32, (1, tq), 1)
    key_limit = jnp.minimum((lax.shift_right_arithmetic(q_pos, CHUNK.bit_length() - 1) + 1) * CHUNK, s_real)
    key_iota = lax.broadcasted_iota(jnp.int32, (tk, tq), 0)
    low_half = lax.broadcasted_iota(jnp.int32, (tq, LANES), 1) < hd
    halves = (low_half, jnp.logical_not(low_half))
    real_q = qb * tq + lax.broadcasted_iota(jnp.int32, (1, tq), 1) < t_real

    pa = pa_ref[0]
    qi = jnp.concatenate(
        [jnp.where(halves[h % 2], pa[:, QI_OFF + (h // 2) * LANES:QI_OFF + (h // 2 + 1) * LANES] * (D_IDX ** -0.5), 0.0)
         for h in range(N_IDX)], axis=0).astype(BF16)
    w_t = pa[:, KI_OFF:KI_OFF + LANES].T * (N_IDX ** -0.5)
    w_rows = [w_t[D_IDX + h:D_IDX + h + 1, :] for h in range(N_IDX)]

    def score_body(kt, carry):
        start = pl.multiple_of(kt * tk, tk)
        s = lax.dot_general(ki_ref[0, pl.ds(start, tk), :], qi, (((1,), (1,)), ((), ())),
                            preferred_element_type=F32)
        score = jnp.zeros((tk, tq), F32)
        for h in range(N_IDX):
            score = score + jnp.maximum(s[:, h * tq:(h + 1) * tq], 0.0) * w_rows[h]
        score = score + 0.0
        score = jnp.where(start + key_iota < key_limit, score, NEG_INF)
        bits = pltpu.bitcast(score, jnp.int32)
        key = jnp.where(bits < 0, bits ^ 0x7FFFFFFF, bits)
        key_sc[kt] = key
        hi_sc[kt] = lax.shift_right_arithmetic(key, 16).astype(I16)
        return carry

    lax.fori_loop(0, n_kt, score_body, 0)

    def count(pred_fn):
        def body(kt, acc):
            hit = jnp.where(pred_fn(key_sc[kt], kt), 1, 0)
            return acc + _fold_rows(hit, I32_ROWS, jnp.add)

        acc = lax.fori_loop(0, n_kt, body, jnp.zeros((I32_ROWS, tq), jnp.int32))
        return jnp.sum(acc, axis=0, keepdims=True)

    def count16(ref, c):
        c16 = c.astype(I16)

        def body(kt, acc):
            hit = jnp.where(ref[kt] >= c16, jnp.ones((), I16), jnp.zeros((), I16))
            return acc + _fold_rows(hit, I16_ROWS, jnp.add)

        acc = lax.fori_loop(0, n_kt, body, jnp.zeros((I16_ROWS, tq), I16))
        return jnp.sum(acc.astype(jnp.int32), axis=0, keepdims=True)

    def kth_digit(ref, kk):
        def body(i, prefix):
            cand = prefix | lax.shift_left(jnp.int32(1), 15 - i)
            cnt = count16(ref, cand + I16_MIN)
            return jnp.where(cnt >= kk, cand, prefix)

        return lax.fori_loop(0, 16, body, jnp.zeros((1, tq), jnp.int32))

    hi_thr = kth_digit(hi_sc, topk) + I16_MIN
    above = jnp.where(hi_thr < 2 ** 15 - 1, count16(hi_sc, jnp.minimum(hi_thr + 1, 2 ** 15 - 1)), 0)

    def lo_body(kt, carry):
        key = key_sc[kt]
        in_bucket = lax.shift_right_arithmetic(key, 16) == hi_thr
        lo_sc[kt] = jnp.where(in_bucket, (key & 0xFFFF) + I16_MIN, I16_MIN).astype(I16)
        return carry

    lax.fori_loop(0, n_kt, lo_body, 0)
    lo_thr = kth_digit(lo_sc, topk - above)
    thr = jnp.maximum(lax.shift_left(hi_thr, 16) | lo_thr, NEG_KEY)
    cnt_gt = count(lambda key, kt: key > thr)
    cnt_ge = count(lambda key, kt: key >= thr)
    need = topk - cnt_gt
    real_thr = thr != NEG_KEY
    has_excess = jnp.max(jnp.where(real_thr & real_q & (cnt_ge > topk), 1, 0)) > 0

    n_idx_bits = max(1, int(np.ceil(np.log2(key_sc.shape[0] * tk))))

    def tie_search():
        def jbody(i, jq):
            cand = jq | lax.shift_left(jnp.int32(1), n_idx_bits - 1 - i)
            cnt = count(lambda key, kt: (key == thr) & (kt * tk + key_iota < cand))
            return jnp.where(cnt < need, cand, jq)

        return lax.fori_loop(0, n_idx_bits, jbody, jnp.zeros((1, tq), jnp.int32))

    j_cut = lax.cond(has_excess, tie_search, lambda: jnp.full((1, tq), 2 ** 30, jnp.int32))
    j_cut = jnp.where(real_thr, j_cut, -1)

    qs = []
    for j in range(N_KV_A):
        parts = []
        for g in range(group):
            h = j * group + g
            grp = pa[:, (h // 2) * LANES:(h // 2 + 1) * LANES] * (hd ** -0.5)
            if h % 2 != j:
                grp = pltpu.roll(grp, hd, 1)
            parts.append(jnp.where(halves[j], grp, 0.0))
        qs.append(jnp.concatenate(parts, axis=0).astype(BF16))

    def attn_body(kt, carry):
        start = pl.multiple_of(kt * tk, tk)
        key = key_sc[kt]
        sel = (key > thr) | ((key == thr) & (start + key_iota <= j_cut))
        bias = jnp.where(sel, 0.0, NEG_INF).astype(F32)
        bias = jnp.concatenate([bias] * group, axis=1)
        k_tile = k_ref[0, pl.ds(start, tk), :]
        heads = range(N_KV_A)
        logits = [lax.dot_general(k_tile, qs[j], (((1,), (1,)), ((), ())), preferred_element_type=F32) * LOG2E
                  + bias for j in heads]
        m_new = [jnp.maximum(carry[j][0], jnp.max(_fold_rows(logits[j], I32_ROWS, jnp.maximum), axis=0,
                                                   keepdims=True)) for j in heads]
        alpha = [jnp.exp2(carry[j][0] - m_new[j]) for j in heads]
        p = [jnp.exp2(logits[j] - m_new[j]) for j in heads]
        l_new = [carry[j][1] * alpha[j] + jnp.sum(_fold_rows(p[j], I32_ROWS, jnp.add), axis=0, keepdims=True)
                 for j in heads]
        acc = [carry[j][2] * alpha[j]
               + jnp.dot(vt_ref[0, j * hd:(j + 1) * hd, pl.ds(start, tk)], p[j].astype(BF16),
                         preferred_element_type=F32) for j in heads]
        return tuple((m_new[j], l_new[j], acc[j]) for j in heads)

    init = tuple((jnp.full((1, cols), 0.1 * NEG_INF, F32), jnp.zeros((1, cols), F32),
                  jnp.zeros((hd, cols), F32)) for _ in range(N_KV_A))
    res = lax.fori_loop(0, n_kt, attn_body, init)
    outs = [acc / l_i for _, l_i, acc in res]
    for grp_i in range(N_HEADS_A // 2):
        pair = []
        for h in (2 * grp_i, 2 * grp_i + 1):
            j, g = h // group, h % group
            pair.append(outs[j][:, g * tq:(g + 1) * tq])
        o_ref[0, :, grp_i * LANES:(grp_i + 1) * LANES] = jnp.concatenate(pair, axis=0).T


def _dsa(pa, ki_all, k_all, vt_all, *, pos0, s_real, tq, tk):
    b, t, _ = pa.shape
    t_pad = _round_up(t, tq)
    pa = jnp.pad(pa, ((0, 0), (0, t_pad - t), (0, 0)))
    s_pad = ki_all.shape[1]
    topk = min(TOPK_MAX, s_real // 4)
    kern = functools.partial(_dsa_kernel, tq=tq, tk=tk, pos0=pos0, t_real=t, s_real=s_real, topk=topk)
    kv_spec = pl.BlockSpec((1, s_pad, LANES), lambda bi, qi: (bi, 0, 0))
    out = pl.pallas_call(
        kern,
        grid=(b, t_pad // tq),
        in_specs=[pl.BlockSpec((1, tq, ATT_COLS), lambda bi, qi: (bi, qi, 0)), kv_spec, kv_spec,
                  pl.BlockSpec((1, LANES, s_pad), lambda bi, qi: (bi, 0, 0))],
        out_specs=pl.BlockSpec((1, tq, D_A), lambda bi, qi: (bi, qi, 0)),
        out_shape=jax.ShapeDtypeStruct((b, t_pad, D_A), F32),
        scratch_shapes=[pltpu.VMEM((s_pad // tk, tk, tq), jnp.int32), pltpu.VMEM((s_pad // tk, tk, tq), I16),
                        pltpu.VMEM((s_pad // tk, tk, tq), I16)],
        compiler_params=_cparams("parallel", "arbitrary"),
        name="dsa",
    )(pa, ki_all, k_all, vt_all)
    return out[:, :t]


D_R = N_HEADS_R * HEAD_DIM_R
D_SHIFT = 3 * D_R + 2 * D_LORA
N_PAIR = D_R // LANES
N_VEC = 8


def _split2(x):
    hi = x.astype(BF16)
    lo = (x - hi.astype(F32)).astype(BF16)
    return hi, lo


def _mm(a, b, ca, cb):
    return lax.dot_general(a, b, (((ca,), (cb,)), ((), ())), preferred_element_type=F32)


def _rwkv_kernel(sh_ref, sh0_ref, mu_ref, vec_ref, wl_ref, s0_ref, y_ref, sout_ref, st_sc, row_sc, *, bb, c):
    ci = pl.program_id(1)
    n = HEAD_DIM_R
    c2 = 2 * c

    @pl.when(ci == 0)
    def _():
        st_sc[...] = s0_ref[...]
        row_sc[...] = sh0_ref[...]

    head0 = lax.broadcasted_iota(jnp.int32, (c, LANES), 1) < n
    r2 = lax.broadcasted_iota(jnp.int32, (c2, c2), 0)
    q2 = lax.broadcasted_iota(jnp.int32, (c2, c2), 1)
    same_head = (r2 // c) == (q2 // c)
    strict = same_head & (r2 > q2)
    incl = same_head & (r2 >= q2)
    eye2 = jnp.where(r2 == q2, 1.0, 0.0).astype(F32)
    rc = lax.broadcasted_iota(jnp.int32, (c, c), 0)
    cc = lax.broadcasted_iota(jnp.int32, (c, c), 1)
    tri = jnp.where(rc >= cc, 1.0, 0.0).astype(BF16)
    tri3 = jnp.concatenate([tri, tri, tri], axis=1)
    la = lax.broadcasted_iota(jnp.int32, (LANES, LANES), 0)
    lb = lax.broadcasted_iota(jnp.int32, (LANES, LANES), 1)
    seg = jnp.where((la // n) == (lb // n), 1.0, 0.0).astype(BF16)
    seg2 = jnp.concatenate([seg, seg], axis=0)
    first_row = lax.broadcasted_iota(jnp.int32, (c, D_SHIFT), 0) == 0

    def seg_sum(x):
        xh, xl = _split2(x)
        return jnp.dot(jnp.concatenate([xh, xl], axis=1), seg2, preferred_element_type=F32)

    def vec(i, sl=slice(None)):
        return vec_ref[i:i + 1, sl]

    n_double = 0
    while (2 << n_double) < c:
        n_double += 1

    def stack(zz):
        return jnp.concatenate([jnp.where(head0, zz, 0.0), jnp.where(head0, 0.0, zz)], axis=0)

    chains = [(bi, p) for bi in range(bb) for p in range(N_PAIR)]
    states = [st_sc[bi, p] for bi, p in chains]

    prep = []
    for bi in range(bb):
        sh = sh_ref[bi]
        prev = jnp.where(first_row, row_sc[bi], pltpu.roll(sh, 1, 0))
        x = sh + (prev - sh) * mu_ref[...]
        r = x[:, :D_R]
        k = x[:, D_R:2 * D_R]
        v = x[:, 2 * D_R:3 * D_R]
        xl = x[:, 3 * D_R:]
        lora_in = jnp.where(head0, jnp.tanh(xl), xl)
        wa = jnp.dot(lora_in.astype(BF16), wl_ref[...], preferred_element_type=F32)
        w_raw = vec(0) + wa[:, :D_R]
        a = jax.nn.sigmoid(vec(1) + wa[:, D_R:])
        lw = -jnp.exp(-jax.nn.softplus(-w_raw) - 0.5)
        kkv = k * vec(2)
        kk2 = kkv * kkv
        ss = jnp.concatenate([seg_sum(kk2[:, p * LANES:(p + 1) * LANES]) for p in range(N_PAIR)], axis=1)
        kk = kkv / jnp.maximum(jnp.sqrt(ss), 1e-12)
        k2 = k * (1.0 + (a - 1.0) * vec(3))
        l_hi = lw.astype(BF16)
        l_r1 = lw - l_hi.astype(F32)
        l_mid = l_r1.astype(BF16)
        l_lo = (l_r1 - l_mid.astype(F32)).astype(BF16)
        cum = jnp.dot(tri3, jnp.concatenate([l_hi, l_mid, l_lo], axis=0), preferred_element_type=F32)
        p_fwd = jnp.exp(cum)
        p_inv = jnp.exp(-cum)
        prep.append(dict(v=v, p_fwd=p_fwd, at=-kk * jnp.exp(cum - lw), bt=kk * a * p_inv, kt=k2 * p_inv,
                         rt=r * p_fwd, rk_prod=r * k2 * vec(4)))
    for bi in range(bb):
        row_sc[bi] = sh_ref[bi, c - 1:c, :]

    ops = []
    for bi, p in chains:
        sl = slice(p * LANES, (p + 1) * LANES)
        d = prep[bi]
        b_st, k_st, v_st = stack(d["bt"][:, sl]), stack(d["kt"][:, sl]), stack(d["v"][:, sl])
        ar = jnp.concatenate([stack(d["at"][:, sl]), stack(d["rt"][:, sl])], axis=0).astype(BF16)
        bk = jnp.concatenate([b_st, k_st], axis=0).astype(BF16)
        p_last = d["p_fwd"][c - 1:c, sl]
        ops.append(dict(ar=ar, bk=bk, v_b=v_st.astype(BF16), p_last=p_last,
                        b_dec=(b_st * p_last).astype(BF16), k_dec=(k_st * p_last).astype(BF16)))
    grams = [_mm(o["ar"], o["bk"], 1, 1) for o in ops]
    l_ab = [jnp.where(strict, g[:c2, :c2], 0.0) for g in grams]
    l_ak = [jnp.where(strict, g[:c2, c2:], 0.0).astype(BF16) for g in grams]
    m_rb = [jnp.where(incl, g[c2:, :c2], 0.0).astype(BF16) for g in grams]
    m_rk = [jnp.where(incl, g[c2:, c2:], 0.0).astype(BF16) for g in grams]
    inv = [eye2 + l for l in l_ab]
    lp = l_ab
    for _ in range(n_double):
        lpb = [l.astype(BF16) for l in lp]
        lp = [_mm(l, l, 1, 0) for l in lpb]
        inv = [x + _mm(x.astype(BF16), l.astype(BF16), 1, 0) for x, l in zip(inv, lp)]

    s_b = [s.astype(BF16) for s in states]
    z = [_mm(o["ar"][:c2], sb, 1, 1) + _mm(la, o["v_b"], 1, 0) for o, sb, la in zip(ops, s_b, l_ak)]
    u_b = [_mm(x.astype(BF16), zz.astype(BF16), 1, 0).astype(BF16) for x, zz in zip(inv, z)]
    y_st = [_mm(o["ar"][c2:], sb, 1, 1) + _mm(mb, u, 1, 0) + _mm(mk, o["v_b"], 1, 0)
            for o, sb, mb, mk, u in zip(ops, s_b, m_rb, m_rk, u_b)]
    new_states = [s * o["p_last"] + _mm(u, o["b_dec"], 0, 0) + _mm(o["v_b"], o["k_dec"], 0, 0)
                  for s, o, u in zip(states, ops, u_b)]
    for (bi, p), s_new, ys in zip(chains, new_states, y_st):
        sl = slice(p * LANES, (p + 1) * LANES)
        st_sc[bi, p] = s_new
        y = ys[:c] + ys[c:]
        dev = y - seg_sum(y) * (1.0 / n)
        var = seg_sum(dev * dev) * (1.0 / n)
        yn = dev * lax.rsqrt(var + GN_EPS) * vec(5, sl) + vec(6, sl)
        y_ref[bi, :, sl] = yn + seg_sum(prep[bi]["rk_prod"][:, sl]) * prep[bi]["v"][:, sl]

    @pl.when(ci == pl.num_programs(1) - 1)
    def _():
        sout_ref[...] = st_sc[...]


def _rwkv(sh, shift0, mu, w0, w2, a0, a2, k_k, k_a, r_k, ln_w, ln_b, state0, *, bb, c):
    b, t, _ = sh.shape
    n = HEAD_DIM_R
    eye = jnp.eye(2, dtype=F32)
    s0 = jnp.einsum('bphij,hg->bphigj', state0.reshape(b, N_PAIR, 2, n, n), eye).reshape(b, N_PAIR, LANES, LANES)
    vecs = jnp.stack([w0, a0, k_k, k_a, r_k.reshape(D_R), ln_w, ln_b, jnp.zeros_like(w0)])
    zero = jnp.zeros((D_LORA, D_R), F32)
    w_lora = jnp.concatenate([jnp.concatenate([w2, zero], axis=1), jnp.concatenate([zero, a2], axis=1)],
                             axis=0).astype(BF16)
    full = lambda shape: pl.BlockSpec(shape, lambda bi, ci: (0,) * len(shape))
    st_spec = pl.BlockSpec((bb, N_PAIR, LANES, LANES), lambda bi, ci: (bi, 0, 0, 0))
    y, s_out = pl.pallas_call(
        functools.partial(_rwkv_kernel, bb=bb, c=c),
        grid=(b // bb, t // c),
        in_specs=[pl.BlockSpec((bb, c, D_SHIFT), lambda bi, ci: (bi, ci, 0)),
                  pl.BlockSpec((bb, 1, D_SHIFT), lambda bi, ci: (bi, 0, 0)),
                  full((1, D_SHIFT)), full((N_VEC, D_R)), full((2 * D_LORA, 2 * D_R)), st_spec],
        out_specs=[pl.BlockSpec((bb, c, D_R), lambda bi, ci: (bi, ci, 0)), st_spec],
        out_shape=[jax.ShapeDtypeStruct((b, t, D_R), F32),
                   jax.ShapeDtypeStruct((b, N_PAIR, LANES, LANES), F32)],
        scratch_shapes=[pltpu.VMEM((bb, N_PAIR, LANES, LANES), F32), pltpu.VMEM((bb, 1, D_SHIFT), F32)],
        compiler_params=_cparams("parallel", "arbitrary"),
        name="rwkv",
    )(sh, shift0, mu.reshape(1, D_SHIFT), vecs, w_lora, s0)
    s_t = jnp.einsum('bphigj,hg->bphij', s_out.reshape(b, N_PAIR, 2, n, 2, n), eye)
    return y, s_t.reshape(b, N_HEADS_R, n, n)


def _mem_attn_kernel(q_ref, mk_ref, mv_ref, o_ref):
    scale = HEAD_DIM_M ** -0.5
    for h in range(N_HEADS_M):
        sl = slice(h * HEAD_DIM_M, (h + 1) * HEAD_DIM_M)
        q = q_ref[0, :, sl].astype(BF16)
        logits = lax.dot_general(q, mk_ref[0, :, sl], (((1,), (1,)), ((), ())), preferred_element_type=F32) * scale
        e = jnp.exp(logits - jnp.max(logits, axis=1, keepdims=True))
        p = e / jnp.sum(e, axis=1, keepdims=True)
        o_ref[0, :, sl] = jnp.dot(p.astype(BF16), mv_ref[0, :, sl], preferred_element_type=F32)


def _mem_attn(qm, mk, mv, tm):
    b, t, d = qm.shape
    nm = mk.shape[1]
    return pl.pallas_call(
        _mem_attn_kernel,
        grid=(b, t // tm),
        in_specs=[pl.BlockSpec((1, tm, d), lambda bi, i: (bi, i, 0)),
                  pl.BlockSpec((1, nm, d), lambda bi, i: (bi, 0, 0)),
                  pl.BlockSpec((1, nm, d), lambda bi, i: (bi, 0, 0))],
        out_specs=pl.BlockSpec((1, tm, d), lambda bi, i: (bi, i, 0)),
        out_shape=jax.ShapeDtypeStruct((b, t, d), F32),
        compiler_params=_cparams("parallel", "parallel"),
        name="mem_attn",
    )(qm, mk, mv)


def _merge_kernel(oa_ref, yr_ref, om_ref, zg_ref, x_ref, wa_ref, wr_ref, wm_ref, wo_ref, g_ref, xo_ref, ho_ref):
    d_br = oa_ref.shape[1]
    d = x_ref.shape[1]

    def branch(o, z, w_ref):
        return jnp.dot((o * jax.nn.silu(z)).astype(BF16), w_ref[...], preferred_element_type=F32)

    pa = branch(oa_ref[...], zg_ref[:, 0:d_br], wa_ref)
    pr = branch(yr_ref[...], zg_ref[:, d_br:2 * d_br], wr_ref)
    pm = branch(om_ref[...], zg_ref[:, 2 * d_br:3 * d_br], wm_ref)
    g0 = 3 * d_br
    merged = (jax.nn.sigmoid(zg_ref[:, g0:g0 + d]) * pa
              + jax.nn.sigmoid(zg_ref[:, g0 + d:g0 + 2 * d]) * pr
              + jax.nn.sigmoid(zg_ref[:, g0 + 2 * d:g0 + 3 * d]) * pm)
    xn = x_ref[...] + jnp.dot(merged.astype(BF16), wo_ref[...], preferred_element_type=F32)
    xo_ref[...] = xn
    y = xn * lax.rsqrt(jnp.mean(xn * xn, axis=-1, keepdims=True) + RMS_EPS)
    ho_ref[...] = y * g_ref[...]


def _merge(oa, yr, om, zg, x2d, wa, wr, wm, wo, g_next, tm):
    m, d = x2d.shape
    d_br = oa.shape[1]
    tm = min(tm, m)
    row = lambda w: pl.BlockSpec((tm, w), lambda i: (i, 0))
    full = lambda a: pl.BlockSpec(a.shape, lambda i: (0, 0))
    g2 = g_next.reshape(1, d)
    return pl.pallas_call(
        _merge_kernel,
        grid=(m // tm,),
        in_specs=[row(d_br), row(d_br), row(d_br), row(zg.shape[1]), row(d),
                  full(wa), full(wr), full(wm), full(wo), full(g2)],
        out_specs=[row(d), row(d)],
        out_shape=[jax.ShapeDtypeStruct((m, d), F32), jax.ShapeDtypeStruct((m, d), F32)],
        compiler_params=_cparams("parallel"),
        name="merge",
    )(oa, yr, om, zg, x2d, wa, wr, wm, wo, g2)


def _round_up(x, m):
    return (x + m - 1) // m * m


def _split_w_in(w_in):
    d_a = N_HEADS_A * HEAD_DIM_A
    d_kv = N_KV_A * HEAD_DIM_A
    d_r = N_HEADS_R * HEAD_DIM_R
    d_m = N_HEADS_M * HEAD_DIM_M
    d_shift = 3 * d_r + 2 * D_LORA
    d_model = w_in.shape[0]
    sizes = (d_a, d_kv, d_kv, N_IDX * D_IDX, D_IDX, N_IDX, d_a, d_shift, d_r, d_m, d_m, N_BRANCH * d_model)
    offs = np.cumsum((0,) + sizes)
    col = lambda i: w_in[:, offs[i]:offs[i + 1]]
    pad = jnp.zeros((d_model, LANES - D_IDX - N_IDX), w_in.dtype)
    w_att = jnp.concatenate([col(0), col(1), col(2), col(3), col(4), col(5), pad], axis=1).astype(BF16)
    w_sh = col(7).astype(BF16)
    w_qm = col(9).astype(BF16)
    w_zg = jnp.concatenate([col(6), col(8), col(10), col(11)], axis=1).astype(BF16)
    return w_att, w_sh, w_qm, w_zg


def _layer(x2d, h2d, b, t, pos0, past_k, past_v, past_ki, wkv0, shift0, mem_k, mem_v, lw, g_next, cfg):
    (w_att, w_sh, w_qm, w_zg, mu_shift, w0, w2, a0, a2, k_k, k_a, r_k, ln_w, ln_b, w_br_a, w_br_r, w_br_m,
     w_out) = lw
    tm = cfg["tm"]
    d_a = N_HEADS_A * HEAD_DIM_A
    d_kv = N_KV_A * HEAD_DIM_A
    d_r = N_HEADS_R * HEAD_DIM_R
    d_m = N_HEADS_M * HEAD_DIM_M

    tables = _rope_tables(pos0, t)
    pa, kb, vb, ki2 = _attn_proj(h2d, w_att, tables, t, min(tm, t))
    pa = pa.reshape(b, t, ATT_COLS)
    sh = _matmul(h2d, w_sh, tm, w_sh.shape[1]).reshape(b, t, -1)
    qm = _matmul(h2d, w_qm, tm, w_qm.shape[1]).reshape(b, t, d_m)
    zg = _matmul(h2d, w_zg, tm, cfg["tn_zg"])

    k_new = pa[..., K_OFF:K_OFF + D_KV].reshape(b, t, N_KV_A, HEAD_DIM_A)
    v_new = pa[..., V_OFF:V_OFF + D_KV].reshape(b, t, N_KV_A, HEAD_DIM_A)
    ki_new = pa[..., KI_OFF:KI_OFF + D_IDX]
    s_real = pos0 + t
    tq, tk = cfg["tq"], cfg["tk"]
    s_pad = _round_up(s_real, tk)

    def with_past(past, new):
        full = jnp.concatenate([past.reshape(b, pos0, LANES).astype(BF16), new.reshape(b, t, LANES)], axis=1)
        return jnp.pad(full, ((0, 0), (0, s_pad - s_real), (0, 0)))

    vt_all = with_past(past_v, vb).transpose(0, 2, 1)
    o_a = _dsa(pa, with_past(jnp.concatenate([past_ki, past_ki], axis=-1), ki2), with_past(past_k, kb),
               vt_all, pos0=pos0, s_real=s_real, tq=tq, tk=tk).reshape(b * t, d_a)

    y_r, wkv_t = _rwkv(sh, shift0, mu_shift, w0, w2, a0, a2, k_k, k_a, r_k, ln_w, ln_b, wkv0,
                       bb=cfg["bb"], c=CHUNK)
    y_r = y_r.reshape(b * t, d_r)

    o_m = _mem_attn(qm, mem_k, mem_v, min(tm, t)).reshape(b * t, d_m)

    x_new, h_next = _merge(o_a, y_r, o_m, zg, x2d, w_br_a, w_br_r, w_br_m, w_out, g_next, cfg["tm_merge"])
    return x_new, h_next, k_new, v_new, ki_new, wkv_t, sh[:, -1:]


PROMPT_CFG = dict(tm=512, tn_zg=2304, tq=128, tk=1024, bb=2, tm_merge=256)
SAMPLE_CFG = dict(tm=512, tn_zg=2304, tq=128, tk=1152, bb=2, tm_merge=256)


def kernel(x_prompt, x_sample, mem_prompt, cache_k, cache_v, cache_kidx, state_wkv, state_shift, cache_mem_k,
           cache_mem_v, norm_g, w_in, mu_shift, w0, w2, a0, a2, k_k, k_a, r_k, ln_x_w, ln_x_b, mem_norm_g,
           w_mem_kv, w_br_a, w_br_r, w_br_m, w_out, final_norm_g):
    depth = w_in.shape[0]
    bp, tp, d = x_prompt.shape
    bs, ts, _ = x_sample.shape
    n_mem = mem_prompt.shape[1]
    d_m = N_HEADS_M * HEAD_DIM_M
    past = cache_k.shape[2]

    xp = x_prompt.reshape(bp * tp, d)
    xs = x_sample.reshape(bs * ts, d)
    hp = _rmsnorm(xp, norm_g[0], F32, 512)
    hs = _rmsnorm(xs, norm_g[0], F32, 512)
    mem2d = mem_prompt.reshape(bp * n_mem, d)

    empty_kv = jnp.zeros((bp, 0, N_KV_A, HEAD_DIM_A), F32)
    empty_ki = jnp.zeros((bp, 0, D_IDX), F32)
    zero_state = jnp.zeros((bp, N_HEADS_R, HEAD_DIM_R, HEAD_DIM_R), F32)
    zero_shift = jnp.zeros((bp, 1, mu_shift.shape[1]), F32)

    outs_p = [[] for _ in range(7)]
    outs_s = [[] for _ in range(5)]
    for l in range(depth):
        g_next = norm_g[l + 1] if l + 1 < depth else final_norm_g
        lw = _split_w_in(w_in[l]) + (mu_shift[l], w0[l], w2[l], a0[l], a2[l], k_k[l], k_a[l], r_k[l], ln_x_w[l],
                                     ln_x_b[l], w_br_a[l].astype(BF16), w_br_r[l].astype(BF16),
                                     w_br_m[l].astype(BF16), w_out[l].astype(BF16))
        mem_h = _rmsnorm(mem2d, mem_norm_g[l], F32, 512)
        mkv = _matmul(mem_h, w_mem_kv[l].astype(BF16), 512, 2 * d_m).reshape(bp, n_mem, 2 * d_m)
        mk, mv = mkv[..., :d_m], mkv[..., d_m:]
        xp, hp, k_n, v_n, ki_n, wkv_n, sh_n = _layer(
            xp, hp, bp, tp, 0, empty_kv, empty_kv, empty_ki, zero_state, zero_shift,
            mk.astype(BF16), mv.astype(BF16), lw, g_next, PROMPT_CFG)
        for lst, val in zip(outs_p, (k_n, v_n, ki_n, wkv_n, sh_n,
                                     mk.reshape(bp, n_mem, N_HEADS_M, HEAD_DIM_M),
                                     mv.reshape(bp, n_mem, N_HEADS_M, HEAD_DIM_M))):
            lst.append(val)
        xs, hs, k_n, v_n, ki_n, wkv_n, sh_n = _layer(
            xs, hs, bs, ts, past, cache_k[l], cache_v[l], cache_kidx[l], state_wkv[l], state_shift[l],
            cache_mem_k[l].reshape(bs, n_mem, d_m).astype(BF16), cache_mem_v[l].reshape(bs, n_mem, d_m).astype(BF16),
            lw, g_next, SAMPLE_CFG)
        for lst, val in zip(outs_s, (k_n, v_n, ki_n, wkv_n, sh_n)):
            lst.append(val)

    y_prompt = hp.reshape(bp, tp, d)
    y_sample = hs.reshape(bs, ts, d)
    return (y_prompt, y_sample) + tuple(jnp.stack(o) for o in outs_p) + tuple(jnp.stack(o) for o in outs_s)
```

```python
import functools

import jax
import jax.numpy as jnp
import numpy as np
from jax import lax
from jax.experimental import pallas as pl
from jax.experimental.pallas import tpu as pltpu

F32 = jnp.float32
BF16 = jnp.bfloat16

CHUNK = 64
ROPE_THETA = 10000.0
RMS_EPS = 1e-6
NEG_INF = -1e30
N_HEADS_A = 8
HEAD_DIM_A = 64
N_KV_A = 2
N_IDX = 4
D_IDX = 64
TOPK_MAX = 256
N_HEADS_R = 8
HEAD_DIM_R = 64
D_LORA = 64
GN_EPS = 64e-5
N_HEADS_M = 4
HEAD_DIM_M = 128
N_BRANCH = 3

LANES = 128
VMEM_LIMIT = 48 * 1024 * 1024

_NEG_BITS = int(np.array(NEG_INF, np.float32).view(np.int32))
NEG_KEY = _NEG_BITS ^ 0x7FFFFFFF
SIGN_FLIP = -(2 ** 31)


def _cparams(*sem):
    return pltpu.CompilerParams(dimension_semantics=sem, vmem_limit_bytes=VMEM_LIMIT)


def _rmsnorm_kernel(x_ref, g_ref, o_ref):
    x = x_ref[...]
    y = x * lax.rsqrt(jnp.mean(x * x, axis=-1, keepdims=True) + RMS_EPS)
    o_ref[...] = (y * g_ref[...]).astype(o_ref.dtype)


def _rmsnorm(x2d, g, out_dtype, tm):
    m, d = x2d.shape
    tm = min(tm, m)
    return pl.pallas_call(
        _rmsnorm_kernel,
        grid=(m // tm,),
        in_specs=[pl.BlockSpec((tm, d), lambda i: (i, 0)), pl.BlockSpec((1, d), lambda i: (0, 0))],
        out_specs=pl.BlockSpec((tm, d), lambda i: (i, 0)),
        out_shape=jax.ShapeDtypeStruct((m, d), out_dtype),
        compiler_params=_cparams("parallel"),
        name="rmsnorm",
    )(x2d, g.reshape(1, d))


def _matmul_kernel(x_ref, w_ref, o_ref):
    o_ref[...] = jnp.dot(x_ref[...].astype(BF16), w_ref[...], preferred_element_type=F32)


def _matmul(x2d, w, tm, tn):
    m, k = x2d.shape
    n = w.shape[1]
    tm = min(tm, m)
    return pl.pallas_call(
        _matmul_kernel,
        grid=(n // tn, m // tm),
        in_specs=[pl.BlockSpec((tm, k), lambda j, i: (i, 0)), pl.BlockSpec((k, tn), lambda j, i: (0, j))],
        out_specs=pl.BlockSpec((tm, tn), lambda j, i: (i, j)),
        out_shape=jax.ShapeDtypeStruct((m, n), F32),
        compiler_params=_cparams("parallel", "parallel"),
        name="matmul",
    )(x2d, w)


D_A = N_HEADS_A * HEAD_DIM_A
D_KV = N_KV_A * HEAD_DIM_A
K_OFF = D_A
V_OFF = K_OFF + D_KV
QI_OFF = V_OFF + D_KV
KI_OFF = QI_OFF + N_IDX * D_IDX
W_OFF = KI_OFF + D_IDX
ATT_COLS = KI_OFF + LANES
_G_K = K_OFF // LANES
_G_V = V_OFF // LANES
_G_KI = KI_OFF // LANES


def _attn_proj_kernel(h_ref, w_ref, cos_ref, sa_ref, sb_ref, o_ref, kb_ref, vb_ref, ki2_ref):
    p = jnp.dot(h_ref[...].astype(BF16), w_ref[...], preferred_element_type=F32)
    cos, sa, sb = cos_ref[...], sa_ref[...], sb_ref[...]
    lane = lax.broadcasted_iota(jnp.int32, cos.shape, 1)
    for g in range(ATT_COLS // LANES):
        x = p[:, g * LANES:(g + 1) * LANES]
        if g == _G_V:
            o_ref[:, g * LANES:(g + 1) * LANES] = x
            vb_ref[...] = x.astype(BF16)
            continue
        y = x * cos + pltpu.roll(x, LANES - 32, 1) * sa + pltpu.roll(x, 32, 1) * sb
        if g == _G_KI:
            y = jnp.where(lane < D_IDX, y, x)
            ki2_ref[...] = jnp.where(lane < D_IDX, y, pltpu.roll(y, D_IDX, 1)).astype(BF16)
        if g == _G_K:
            kb_ref[...] = y.astype(BF16)
        o_ref[:, g * LANES:(g + 1) * LANES] = y


def _rope_tables(pos0, t):
    d = HEAD_DIM_A
    inv = ROPE_THETA ** (-jnp.arange(0, d, 2, dtype=F32) / d)
    ang = (pos0 + jnp.arange(t)).astype(F32)[:, None] * inv[None, :]
    cos, sin = jnp.cos(ang), jnp.sin(ang)
    zero = jnp.zeros_like(sin)
    cos_f = jnp.concatenate([cos] * 4, axis=-1)
    sin_a = jnp.concatenate([-sin, zero] * 2, axis=-1)
    sin_b = jnp.concatenate([zero, sin] * 2, axis=-1)
    return cos_f, sin_a, sin_b


def _attn_proj(h2d, w_att, tables, t, tm):
    m, k = h2d.shape
    nt = t // tm
    tab_spec = pl.BlockSpec((tm, LANES), lambda i: (i % nt, 0))
    lane_spec = pl.BlockSpec((tm, LANES), lambda i: (i, 0))
    lane_shape = jax.ShapeDtypeStruct((m, LANES), BF16)
    return pl.pallas_call(
        _attn_proj_kernel,
        grid=(m // tm,),
        in_specs=[pl.BlockSpec((tm, k), lambda i: (i, 0)), pl.BlockSpec((k, ATT_COLS), lambda i: (0, 0)),
                  tab_spec, tab_spec, tab_spec],
        out_specs=[pl.BlockSpec((tm, ATT_COLS), lambda i: (i, 0)), lane_spec, lane_spec, lane_spec],
        out_shape=[jax.ShapeDtypeStruct((m, ATT_COLS), F32), lane_shape, lane_shape, lane_shape],
        compiler_params=_cparams("parallel"),
        name="attn_proj",
    )(h2d, w_att, *tables)


I16 = jnp.int16
I16_MIN = -(2 ** 15)
I16_ROWS = 16
I32_ROWS = 8
LOG2E = 1.4426950408889634


def _tree(parts, op):
    while len(parts) > 1:
        parts = [op(parts[i], parts[i + 1]) if i + 1 < len(parts) else parts[i] for i in range(0, len(parts), 2)]
    return parts[0]


def _fold_rows(x, rows, op):
    return _tree([x[g * rows:(g + 1) * rows] for g in range(x.shape[0] // rows)], op)


def _dsa_kernel(pa_ref, ki_ref, k_ref, v_ref, o_ref, key_sc, hi_sc, lo_sc, *, tq, tk, pos0, t_real, s_real, topk):
    qb = pl.program_id(1)
    q0 = pos0 + qb * tq
    n_adm = jnp.minimum(q0 + tq, s_real)
    n_kt = (n_adm + tk - 1) // tk
    hd = HEAD_DIM_A
    group = N_HEADS_A // N_KV_A
    cols = group * tq

    q_pos = q0 + lax.broadcasted_iota(jnp.int32, (1, tq), 1)
    key_limit = jnp.minimum((lax.shift_right_arithmetic(q_pos, CHUNK.bit_length() - 1) + 1) * CHUNK, s_real)
    key_iota = lax.broadcasted_iota(jnp.int32, (tk, tq), 0)
    low_half = lax.broadcasted_iota(jnp.int32, (tq, LANES), 1) < hd
    halves = (low_half, jnp.logical_not(low_half))
    real_q = qb * tq + lax.broadcasted_iota(jnp.int32, (1, tq), 1) < t_real

    pa = pa_ref[0]
    qi = jnp.concatenate(
        [jnp.where(halves[h % 2], pa[:, QI_OFF + (h // 2) * LANES:QI_OFF + (h // 2 + 1) * LANES] * (D_IDX ** -0.5), 0.0)
         for h in range(N_IDX)], axis=0).astype(BF16)
    w_t = pa[:, KI_OFF:KI_OFF + LANES].T * (N_IDX ** -0.5)
    w_rows = [w_t[D_IDX + h:D_IDX + h + 1, :] for h in range(N_IDX)]

    def score_body(kt, carry):
        start = pl.multiple_of(kt * tk, tk)
        s = lax.dot_general(ki_ref[0, pl.ds(start, tk), :], qi, (((1,), (1,)), ((), ())),
                            preferred_element_type=F32)
        score = jnp.zeros((tk, tq), F32)
        for h in range(N_IDX):
            score = score + jnp.maximum(s[:, h * tq:(h + 1) * tq], 0.0) * w_rows[h]
        score = score + 0.0
        score = jnp.where(start + key_iota < key_limit, score, NEG_INF)
        bits = pltpu.bitcast(score, jnp.int32)
        key = jnp.where(bits < 0, bits ^ 0x7FFFFFFF, bits)
        key_sc[kt] = key
        hi_sc[kt] = lax.shift_right_arithmetic(key, 16).astype(I16)
        return carry

    lax.fori_loop(0, n_kt, score_body, 0)

    def count(pred_fn):
        def body(kt, acc):
            hit = jnp.where(pred_fn(key_sc[kt], kt), 1, 0)
            return acc + _fold_rows(hit, I32_ROWS, jnp.add)

        acc = lax.fori_loop(0, n_kt, body, jnp.zeros((I32_ROWS, tq), jnp.int32))
        return jnp.sum(acc, axis=0, keepdims=True)

    def count16(ref, c):
        c16 = c.astype(I16)

        def body(kt, acc):
            hit = jnp.where(ref[kt] >= c16, jnp.ones((), I16), jnp.zeros((), I16))
            return acc + _fold_rows(hit, I16_ROWS, jnp.add)

        acc = lax.fori_loop(0, n_kt, body, jnp.zeros((I16_ROWS, tq), I16))
        return jnp.sum(acc.astype(jnp.int32), axis=0, keepdims=True)

    def kth_digit(ref, kk):
        def body(i, prefix):
            cand = prefix | lax.shift_left(jnp.int32(1), 15 - i)
            cnt = count16(ref, cand + I16_MIN)
            return jnp.where(cnt >= kk, cand, prefix)

        return lax.fori_loop(0, 16, body, jnp.zeros((1, tq), jnp.int32))

    hi_thr = kth_digit(hi_sc, topk) + I16_MIN
    above = jnp.where(hi_thr < 2 ** 15 - 1, count16(hi_sc, jnp.minimum(hi_thr + 1, 2 ** 15 - 1)), 0)

    def lo_body(kt, carry):
        key = key_sc[kt]
        in_bucket = lax.shift_right_arithmetic(key, 16) == hi_thr
        lo_sc[kt] = jnp.where(in_bucket, (key & 0xFFFF) + I16_MIN, I16_MIN).astype(I16)
        return carry

    lax.fori_loop(0, n_kt, lo_body, 0)
    lo_thr = kth_digit(lo_sc, topk - above)
    thr = jnp.maximum(lax.shift_left(hi_thr, 16) | lo_thr, NEG_KEY)
    cnt_gt = count(lambda key, kt: key > thr)
    cnt_ge = count(lambda key, kt: key >= thr)
    need = topk - cnt_gt
    real_thr = thr != NEG_KEY
    has_excess = jnp.max(jnp.where(real_thr & real_q & (cnt_ge > topk), 1, 0)) > 0

    n_idx_bits = max(1, int(np.ceil(np.log2(key_sc.shape[0] * tk))))

    def tie_search():
        def jbody(i, jq):
            cand = jq | lax.shift_left(jnp.int32(1), n_idx_bits - 1 - i)
            cnt = count(lambda key, kt: (key == thr) & (kt * tk + key_iota < cand))
            return jnp.where(cnt < need, cand, jq)

        return lax.fori_loop(0, n_idx_bits, jbody, jnp.zeros((1, tq), jnp.int32))

    j_cut = lax.cond(has_excess, tie_search, lambda: jnp.full((1, tq), 2 ** 30, jnp.int32))
    j_cut = jnp.where(real_thr, j_cut, -1)

    qs = []
    for j in range(N_KV_A):
        parts = []
        for g in range(group):
            h = j * group + g
            grp = pa[:, (h // 2) * LANES:(h // 2 + 1) * LANES] * (hd ** -0.5)
            if h % 2 != j:
                grp = pltpu.roll(grp, hd, 1)
            parts.append(jnp.where(halves[j], grp, 0.0))
        qs.append(jnp.concatenate(parts, axis=0).astype(BF16))

    def attn_body(kt, carry):
        start = pl.multiple_of(kt * tk, tk)
        key = key_sc[kt]
        sel = (key > thr) | ((key == thr) & (start + key_iota <= j_cut))
        bias = jnp.where(sel, 0.0, NEG_INF).astype(F32)
        bias = jnp.concatenate([bias] * group, axis=1)
        k_tile = k_ref[0, pl.ds(start, tk), :]
        heads = range(N_KV_A)
        logits = [lax.dot_general(k_tile, qs[j], (((1,), (1,)), ((), ())), preferred_element_type=F32) * LOG2E
                  + bias for j in heads]
        m_new = [jnp.maximum(carry[j][0], jnp.max(_fold_rows(logits[j], I32_ROWS, jnp.maximum), axis=0,
                                                   keepdims=True)) for j in heads]
        alpha = [jnp.exp2(carry[j][0] - m_new[j]) for j in heads]
        p = [jnp.exp2(logits[j] - m_new[j]) for j in heads]
        l_new = [carry[j][1] * alpha[j] + jnp.sum(_fold_rows(p[j], I32_ROWS, jnp.add), axis=0, keepdims=True)
                 for j in heads]
        v_tile = v_ref[0, pl.ds(start, tk), :]
        pv = [lax.dot_general(p[j].astype(BF16), v_tile, (((0,), (0,)), ((), ())), preferred_element_type=F32)
              for j in heads]
        acc = [carry[j][2] * alpha[j] + pv[j].T[j * hd:(j + 1) * hd] for j in heads]
        return tuple((m_new[j], l_new[j], acc[j]) for j in heads)

    init = tuple((jnp.full((1, cols), 0.1 * NEG_INF, F32), jnp.zeros((1, cols), F32),
                  jnp.zeros((hd, cols), F32)) for _ in range(N_KV_A))
    res = lax.fori_loop(0, n_kt, attn_body, init)
    outs = [acc / l_i for _, l_i, acc in res]
    for grp_i in range(N_HEADS_A // 2):
        pair = []
        for h in (2 * grp_i, 2 * grp_i + 1):
            j, g = h // group, h % group
            pair.append(outs[j][:, g * tq:(g + 1) * tq])
        o_ref[0, :, grp_i * LANES:(grp_i + 1) * LANES] = jnp.concatenate(pair, axis=0).T


def _dsa(pa, ki_all, k_all, v_all, *, pos0, s_real, tq, tk):
    b, t, _ = pa.shape
    t_pad = _round_up(t, tq)
    pa = jnp.pad(pa, ((0, 0), (0, t_pad - t), (0, 0)))
    s_pad = ki_all.shape[1]
    topk = min(TOPK_MAX, s_real // 4)
    kern = functools.partial(_dsa_kernel, tq=tq, tk=tk, pos0=pos0, t_real=t, s_real=s_real, topk=topk)
    kv_spec = pl.BlockSpec((1, s_pad, LANES), lambda bi, qi: (bi, 0, 0))
    out = pl.pallas_call(
        kern,
        grid=(b, t_pad // tq),
        in_specs=[pl.BlockSpec((1, tq, ATT_COLS), lambda bi, qi: (bi, qi, 0)), kv_spec, kv_spec, kv_spec],
        out_specs=pl.BlockSpec((1, tq, D_A), lambda bi, qi: (bi, qi, 0)),
        out_shape=jax.ShapeDtypeStruct((b, t_pad, D_A), F32),
        scratch_shapes=[pltpu.VMEM((s_pad // tk, tk, tq), jnp.int32), pltpu.VMEM((s_pad // tk, tk, tq), I16),
                        pltpu.VMEM((s_pad // tk, tk, tq), I16)],
        compiler_params=_cparams("parallel", "arbitrary"),
        name="dsa",
    )(pa, ki_all, k_all, v_all)
    return out[:, :t]


D_R = N_HEADS_R * HEAD_DIM_R
D_SHIFT = 3 * D_R + 2 * D_LORA
N_PAIR = D_R // LANES
N_VEC = 8


def _split2(x):
    hi = x.astype(BF16)
    lo = (x - hi.astype(F32)).astype(BF16)
    return hi, lo


def _mm(a, b, ca, cb):
    return lax.dot_general(a, b, (((ca,), (cb,)), ((), ())), preferred_element_type=F32)


def _rwkv_kernel(sh_ref, sh0_ref, mu_ref, vec_ref, wl_ref, s0_ref, y_ref, sout_ref, st_sc, row_sc, *, bb, c):
    ci = pl.program_id(1)
    n = HEAD_DIM_R
    c2 = 2 * c

    @pl.when(ci == 0)
    def _():
        st_sc[...] = s0_ref[...]
        row_sc[...] = sh0_ref[...]

    head0 = lax.broadcasted_iota(jnp.int32, (c, LANES), 1) < n
    r2 = lax.broadcasted_iota(jnp.int32, (c2, c2), 0)
    q2 = lax.broadcasted_iota(jnp.int32, (c2, c2), 1)
    same_head = (r2 // c) == (q2 // c)
    strict = same_head & (r2 > q2)
    incl = same_head & (r2 >= q2)
    eye2 = jnp.where(r2 == q2, 1.0, 0.0).astype(F32)
    rc = lax.broadcasted_iota(jnp.int32, (c, c), 0)
    cc = lax.broadcasted_iota(jnp.int32, (c, c), 1)
    tri = jnp.where(rc >= cc, 1.0, 0.0).astype(BF16)
    tri3 = jnp.concatenate([tri, tri, tri], axis=1)
    la = lax.broadcasted_iota(jnp.int32, (LANES, LANES), 0)
    lb = lax.broadcasted_iota(jnp.int32, (LANES, LANES), 1)
    seg = jnp.where((la // n) == (lb // n), 1.0, 0.0).astype(BF16)
    seg2 = jnp.concatenate([seg, seg], axis=0)
    first_row = lax.broadcasted_iota(jnp.int32, (c, D_SHIFT), 0) == 0

    def seg_sum(x):
        xh, xl = _split2(x)
        return jnp.dot(jnp.concatenate([xh, xl], axis=1), seg2, preferred_element_type=F32)

    def vec(i, sl=slice(None)):
        return vec_ref[i:i + 1, sl]

    n_double = 0
    while (2 << n_double) < c:
        n_double += 1

    def stack(zz):
        return jnp.concatenate([jnp.where(head0, zz, 0.0), jnp.where(head0, 0.0, zz)], axis=0)

    chains = [(bi, p) for bi in range(bb) for p in range(N_PAIR)]
    states = [st_sc[bi, p] for bi, p in chains]

    prep = []
    for bi in range(bb):
        sh = sh_ref[bi]
        prev = jnp.where(first_row, row_sc[bi], pltpu.roll(sh, 1, 0))
        x = sh + (prev - sh) * mu_ref[...]
        r = x[:, :D_R]
        k = x[:, D_R:2 * D_R]
        v = x[:, 2 * D_R:3 * D_R]
        xl = x[:, 3 * D_R:]
        lora_in = jnp.where(head0, jnp.tanh(xl), xl)
        wa = jnp.dot(lora_in.astype(BF16), wl_ref[...], preferred_element_type=F32)
        w_raw = vec(0) + wa[:, :D_R]
        a = jax.nn.sigmoid(vec(1) + wa[:, D_R:])
        lw = -jnp.exp(-jax.nn.softplus(-w_raw) - 0.5)
        kkv = k * vec(2)
        kk2 = kkv * kkv
        ss = jnp.concatenate([seg_sum(kk2[:, p * LANES:(p + 1) * LANES]) for p in range(N_PAIR)], axis=1)
        kk = kkv / jnp.maximum(jnp.sqrt(ss), 1e-12)
        k2 = k * (1.0 + (a - 1.0) * vec(3))
        l_hi = lw.astype(BF16)
        l_r1 = lw - l_hi.astype(F32)
        l_mid = l_r1.astype(BF16)
        l_lo = (l_r1 - l_mid.astype(F32)).astype(BF16)
        cum = jnp.dot(tri3, jnp.concatenate([l_hi, l_mid, l_lo], axis=0), preferred_element_type=F32)
        p_fwd = jnp.exp(cum)
        p_inv = jnp.exp(-cum)
        prep.append(dict(v=v, p_fwd=p_fwd, at=-kk * jnp.exp(cum - lw), bt=kk * a * p_inv, kt=k2 * p_inv,
                         rt=r * p_fwd, rk_prod=r * k2 * vec(4)))
    for bi in range(bb):
        row_sc[bi] = sh_ref[bi, c - 1:c, :]

    ops = []
    for bi, p in chains:
        sl = slice(p * LANES, (p + 1) * LANES)
        d = prep[bi]
        b_st, k_st, v_st = stack(d["bt"][:, sl]), stack(d["kt"][:, sl]), stack(d["v"][:, sl])
        ar = jnp.concatenate([stack(d["at"][:, sl]), stack(d["rt"][:, sl])], axis=0).astype(BF16)
        bk = jnp.concatenate([b_st, k_st], axis=0).astype(BF16)
        p_last = d["p_fwd"][c - 1:c, sl]
        ops.append(dict(ar=ar, bk=bk, v_b=v_st.astype(BF16), p_last=p_last,
                        b_dec=(b_st * p_last).astype(BF16), k_dec=(k_st * p_last).astype(BF16)))
    grams = [_mm(o["ar"], o["bk"], 1, 1) for o in ops]
    l_ab = [jnp.where(strict, g[:c2, :c2], 0.0) for g in grams]
    l_ak = [jnp.where(strict, g[:c2, c2:], 0.0).astype(BF16) for g in grams]
    m_rb = [jnp.where(incl, g[c2:, :c2], 0.0).astype(BF16) for g in grams]
    m_rk = [jnp.where(incl, g[c2:, c2:], 0.0).astype(BF16) for g in grams]
    inv = [eye2 + l for l in l_ab]
    lp = l_ab
    for _ in range(n_double):
        lpb = [l.astype(BF16) for l in lp]
        lp = [_mm(l, l, 1, 0) for l in lpb]
        inv = [x + _mm(x.astype(BF16), l.astype(BF16), 1, 0) for x, l in zip(inv, lp)]

    s_b = [s.astype(BF16) for s in states]
    z = [_mm(o["ar"][:c2], sb, 1, 1) + _mm(la, o["v_b"], 1, 0) for o, sb, la in zip(ops, s_b, l_ak)]
    u_b = [_mm(x.astype(BF16), zz.astype(BF16), 1, 0).astype(BF16) for x, zz in zip(inv, z)]
    y_st = [_mm(o["ar"][c2:], sb, 1, 1) + _mm(mb, u, 1, 0) + _mm(mk, o["v_b"], 1, 0)
            for o, sb, mb, mk, u in zip(ops, s_b, m_rb, m_rk, u_b)]
    new_states = [s * o["p_last"] + _mm(u, o["b_dec"], 0, 0) + _mm(o["v_b"], o["k_dec"], 0, 0)
                  for s, o, u in zip(states, ops, u_b)]
    for (bi, p), s_new, ys in zip(chains, new_states, y_st):
        sl = slice(p * LANES, (p + 1) * LANES)
        st_sc[bi, p] = s_new
        y = ys[:c] + ys[c:]
        dev = y - seg_sum(y) * (1.0 / n)
        var = seg_sum(dev * dev) * (1.0 / n)
        yn = dev * lax.rsqrt(var + GN_EPS) * vec(5, sl) + vec(6, sl)
        y_ref[bi, :, sl] = yn + seg_sum(prep[bi]["rk_prod"][:, sl]) * prep[bi]["v"][:, sl]

    @pl.when(ci == pl.num_programs(1) - 1)
    def _():
        sout_ref[...] = st_sc[...]


def _rwkv(sh, shift0, mu, w0, w2, a0, a2, k_k, k_a, r_k, ln_w, ln_b, state0, *, bb, c):
    b, t, _ = sh.shape
    n = HEAD_DIM_R
    eye = jnp.eye(2, dtype=F32)
    s0 = jnp.einsum('bphij,hg->bphigj', state0.reshape(b, N_PAIR, 2, n, n), eye).reshape(b, N_PAIR, LANES, LANES)
    vecs = jnp.stack([w0, a0, k_k, k_a, r_k.reshape(D_R), ln_w, ln_b, jnp.zeros_like(w0)])
    zero = jnp.zeros((D_LORA, D_R), F32)
    w_lora = jnp.concatenate([jnp.concatenate([w2, zero], axis=1), jnp.concatenate([zero, a2], axis=1)],
                             axis=0).astype(BF16)
    full = lambda shape: pl.BlockSpec(shape, lambda bi, ci: (0,) * len(shape))
    st_spec = pl.BlockSpec((bb, N_PAIR, LANES, LANES), lambda bi, ci: (bi, 0, 0, 0))
    y, s_out = pl.pallas_call(
        functools.partial(_rwkv_kernel, bb=bb, c=c),
        grid=(b // bb, t // c),
        in_specs=[pl.BlockSpec((bb, c, D_SHIFT), lambda bi, ci: (bi, ci, 0)),
                  pl.BlockSpec((bb, 1, D_SHIFT), lambda bi, ci: (bi, 0, 0)),
                  full((1, D_SHIFT)), full((N_VEC, D_R)), full((2 * D_LORA, 2 * D_R)), st_spec],
        out_specs=[pl.BlockSpec((bb, c, D_R), lambda bi, ci: (bi, ci, 0)), st_spec],
        out_shape=[jax.ShapeDtypeStruct((b, t, D_R), F32),
                   jax.ShapeDtypeStruct((b, N_PAIR, LANES, LANES), F32)],
        scratch_shapes=[pltpu.VMEM((bb, N_PAIR, LANES, LANES), F32), pltpu.VMEM((bb, 1, D_SHIFT), F32)],
        compiler_params=_cparams("parallel", "arbitrary"),
        name="rwkv",
    )(sh, shift0, mu.reshape(1, D_SHIFT), vecs, w_lora, s0)
    s_t = jnp.einsum('bphigj,hg->bphij', s_out.reshape(b, N_PAIR, 2, n, 2, n), eye)
    return y, s_t.reshape(b, N_HEADS_R, n, n)


def _mem_attn_kernel(q_ref, mk_ref, mv_ref, o_ref):
    scale = HEAD_DIM_M ** -0.5
    for h in range(N_HEADS_M):
        sl = slice(h * HEAD_DIM_M, (h + 1) * HEAD_DIM_M)
        q = q_ref[0, :, sl].astype(BF16)
        logits = lax.dot_general(q, mk_ref[0, :, sl], (((1,), (1,)), ((), ())), preferred_element_type=F32) * scale
        e = jnp.exp(logits - jnp.max(logits, axis=1, keepdims=True))
        p = e / jnp.sum(e, axis=1, keepdims=True)
        o_ref[0, :, sl] = jnp.dot(p.astype(BF16), mv_ref[0, :, sl], preferred_element_type=F32)


def _mem_attn(qm, mk, mv, tm):
    b, t, d = qm.shape
    nm = mk.shape[1]
    return pl.pallas_call(
        _mem_attn_kernel,
        grid=(b, t // tm),
        in_specs=[pl.BlockSpec((1, tm, d), lambda bi, i: (bi, i, 0)),
                  pl.BlockSpec((1, nm, d), lambda bi, i: (bi, 0, 0)),
                  pl.BlockSpec((1, nm, d), lambda bi, i: (bi, 0, 0))],
        out_specs=pl.BlockSpec((1, tm, d), lambda bi, i: (bi, i, 0)),
        out_shape=jax.ShapeDtypeStruct((b, t, d), F32),
        compiler_params=_cparams("parallel", "parallel"),
        name="mem_attn",
    )(qm, mk, mv)


def _merge_kernel(oa_ref, yr_ref, om_ref, zg_ref, x_ref, wa_ref, wr_ref, wm_ref, wo_ref, g_ref, xo_ref, ho_ref):
    d_br = oa_ref.shape[1]
    d = x_ref.shape[1]

    def branch(o, z, w_ref):
        return jnp.dot((o * jax.nn.silu(z)).astype(BF16), w_ref[...], preferred_element_type=F32)

    pa = branch(oa_ref[...], zg_ref[:, 0:d_br], wa_ref)
    pr = branch(yr_ref[...], zg_ref[:, d_br:2 * d_br], wr_ref)
    pm = branch(om_ref[...], zg_ref[:, 2 * d_br:3 * d_br], wm_ref)
    g0 = 3 * d_br
    merged = (jax.nn.sigmoid(zg_ref[:, g0:g0 + d]) * pa
              + jax.nn.sigmoid(zg_ref[:, g0 + d:g0 + 2 * d]) * pr
              + jax.nn.sigmoid(zg_ref[:, g0 + 2 * d:g0 + 3 * d]) * pm)
    xn = x_ref[...] + jnp.dot(merged.astype(BF16), wo_ref[...], preferred_element_type=F32)
    xo_ref[...] = xn
    y = xn * lax.rsqrt(jnp.mean(xn * xn, axis=-1, keepdims=True) + RMS_EPS)
    ho_ref[...] = y * g_ref[...]


def _merge(oa, yr, om, zg, x2d, wa, wr, wm, wo, g_next, tm):
    m, d = x2d.shape
    d_br = oa.shape[1]
    tm = min(tm, m)
    row = lambda w: pl.BlockSpec((tm, w), lambda i: (i, 0))
    full = lambda a: pl.BlockSpec(a.shape, lambda i: (0, 0))
    g2 = g_next.reshape(1, d)
    return pl.pallas_call(
        _merge_kernel,
        grid=(m // tm,),
        in_specs=[row(d_br), row(d_br), row(d_br), row(zg.shape[1]), row(d),
                  full(wa), full(wr), full(wm), full(wo), full(g2)],
        out_specs=[row(d), row(d)],
        out_shape=[jax.ShapeDtypeStruct((m, d), F32), jax.ShapeDtypeStruct((m, d), F32)],
        compiler_params=_cparams("parallel"),
        name="merge",
    )(oa, yr, om, zg, x2d, wa, wr, wm, wo, g2)


def _round_up(x, m):
    return (x + m - 1) // m * m


IN_SIZES = (D_A, D_KV, D_KV, N_IDX * D_IDX, D_IDX, N_IDX, D_A, D_SHIFT, D_R, N_HEADS_M * HEAD_DIM_M,
            N_HEADS_M * HEAD_DIM_M)
IN_OFFS = tuple(int(o) for o in np.cumsum((0,) + IN_SIZES))


def _repack_kernel(w_ref, att_ref, sh_ref, qm_ref, zg_ref):
    col = lambda i: w_ref[:, IN_OFFS[i]:IN_OFFS[i + 1]]
    rows = w_ref.shape[0]
    att_ref[:, :W_OFF + N_IDX] = w_ref[:, :IN_OFFS[6]].astype(BF16)
    att_ref[:, W_OFF + N_IDX:] = jnp.zeros((rows, ATT_COLS - W_OFF - N_IDX), BF16)
    sh_ref[...] = col(7).astype(BF16)
    qm_ref[...] = col(9).astype(BF16)
    zg_ref[:, :D_A] = col(6).astype(BF16)
    zg_ref[:, D_A:D_A + D_R] = col(8).astype(BF16)
    zg_ref[:, D_A + D_R:D_A + D_R + IN_SIZES[10]] = col(10).astype(BF16)
    zg_ref[:, D_A + D_R + IN_SIZES[10]:] = w_ref[:, IN_OFFS[11]:].astype(BF16)


def _split_w_in(w_in):
    d_model, d_in = w_in.shape
    n_zg = D_A + D_R + IN_SIZES[10] + d_in - IN_OFFS[11]
    tr = 128
    widths = (ATT_COLS, D_SHIFT, IN_SIZES[9], n_zg)
    return tuple(pl.pallas_call(
        _repack_kernel,
        grid=(d_model // tr,),
        in_specs=[pl.BlockSpec((tr, d_in), lambda i: (i, 0))],
        out_specs=[pl.BlockSpec((tr, w), lambda i: (i, 0)) for w in widths],
        out_shape=[jax.ShapeDtypeStruct((d_model, w), BF16) for w in widths],
        compiler_params=_cparams("parallel"),
        name="repack",
    )(w_in))


def _layer(x2d, h2d, b, t, pos0, past_k, past_v, past_ki, wkv0, shift0, mem_k, mem_v, lw, g_next, cfg):
    (w_att, w_sh, w_qm, w_zg, mu_shift, w0, w2, a0, a2, k_k, k_a, r_k, ln_w, ln_b, w_br_a, w_br_r, w_br_m,
     w_out) = lw
    tm = cfg["tm"]
    d_a = N_HEADS_A * HEAD_DIM_A
    d_kv = N_KV_A * HEAD_DIM_A
    d_r = N_HEADS_R * HEAD_DIM_R
    d_m = N_HEADS_M * HEAD_DIM_M

    tables = _rope_tables(pos0, t)
    pa, kb, vb, ki2 = _attn_proj(h2d, w_att, tables, t, min(tm, t))
    pa = pa.reshape(b, t, ATT_COLS)
    sh = _matmul(h2d, w_sh, tm, w_sh.shape[1]).reshape(b, t, -1)
    qm = _matmul(h2d, w_qm, tm, w_qm.shape[1]).reshape(b, t, d_m)
    zg = _matmul(h2d, w_zg, tm, cfg["tn_zg"])

    k_new = pa[..., K_OFF:K_OFF + D_KV].reshape(b, t, N_KV_A, HEAD_DIM_A)
    v_new = pa[..., V_OFF:V_OFF + D_KV].reshape(b, t, N_KV_A, HEAD_DIM_A)
    ki_new = pa[..., KI_OFF:KI_OFF + D_IDX]
    s_real = pos0 + t
    tq, tk = cfg["tq"], cfg["tk"]
    s_pad = _round_up(s_real, tk)

    def with_past(past, new):
        full = jnp.concatenate([past.reshape(b, pos0, LANES).astype(BF16), new.reshape(b, t, LANES)], axis=1)
        return jnp.pad(full, ((0, 0), (0, s_pad - s_real), (0, 0)))

    o_a = _dsa(pa, with_past(jnp.concatenate([past_ki, past_ki], axis=-1), ki2), with_past(past_k, kb),
               with_past(past_v, vb), pos0=pos0, s_real=s_real, tq=tq, tk=tk).reshape(b * t, d_a)

    y_r, wkv_t = _rwkv(sh, shift0, mu_shift, w0, w2, a0, a2, k_k, k_a, r_k, ln_w, ln_b, wkv0,
                       bb=cfg["bb"], c=CHUNK)
    y_r = y_r.reshape(b * t, d_r)

    o_m = _mem_attn(qm, mem_k, mem_v, min(tm, t)).reshape(b * t, d_m)

    x_new, h_next = _merge(o_a, y_r, o_m, zg, x2d, w_br_a, w_br_r, w_br_m, w_out, g_next, cfg["tm_merge"])
    return x_new, h_next, k_new, v_new, ki_new, wkv_t, sh[:, -1:]


PROMPT_CFG = dict(tm=512, tn_zg=2304, tq=128, tk=1024, bb=2, tm_merge=256)
SAMPLE_CFG = dict(tm=512, tn_zg=2304, tq=128, tk=1152, bb=2, tm_merge=256)


def kernel(x_prompt, x_sample, mem_prompt, cache_k, cache_v, cache_kidx, state_wkv, state_shift, cache_mem_k,
           cache_mem_v, norm_g, w_in, mu_shift, w0, w2, a0, a2, k_k, k_a, r_k, ln_x_w, ln_x_b, mem_norm_g,
           w_mem_kv, w_br_a, w_br_r, w_br_m, w_out, final_norm_g):
    depth = w_in.shape[0]
    bp, tp, d = x_prompt.shape
    bs, ts, _ = x_sample.shape
    n_mem = mem_prompt.shape[1]
    d_m = N_HEADS_M * HEAD_DIM_M
    past = cache_k.shape[2]

    xp = x_prompt.reshape(bp * tp, d)
    xs = x_sample.reshape(bs * ts, d)
    hp = _rmsnorm(xp, norm_g[0], F32, 512)
    hs = _rmsnorm(xs, norm_g[0], F32, 512)
    mem2d = mem_prompt.reshape(bp * n_mem, d)

    empty_kv = jnp.zeros((bp, 0, N_KV_A, HEAD_DIM_A), F32)
    empty_ki = jnp.zeros((bp, 0, D_IDX), F32)
    zero_state = jnp.zeros((bp, N_HEADS_R, HEAD_DIM_R, HEAD_DIM_R), F32)
    zero_shift = jnp.zeros((bp, 1, mu_shift.shape[1]), F32)

    outs_p = [[] for _ in range(7)]
    outs_s = [[] for _ in range(5)]
    for l in range(depth):
        g_next = norm_g[l + 1] if l + 1 < depth else final_norm_g
        lw = _split_w_in(w_in[l]) + (mu_shift[l], w0[l], w2[l], a0[l], a2[l], k_k[l], k_a[l], r_k[l], ln_x_w[l],
                                     ln_x_b[l], w_br_a[l].astype(BF16), w_br_r[l].astype(BF16),
                                     w_br_m[l].astype(BF16), w_out[l].astype(BF16))
        mem_h = _rmsnorm(mem2d, mem_norm_g[l], F32, 512)
        mkv = _matmul(mem_h, w_mem_kv[l].astype(BF16), 512, 2 * d_m).reshape(bp, n_mem, 2 * d_m)
        mk, mv = mkv[..., :d_m], mkv[..., d_m:]
        xp, hp, k_n, v_n, ki_n, wkv_n, sh_n = _layer(
            xp, hp, bp, tp, 0, empty_kv, empty_kv, empty_ki, zero_state, zero_shift,
            mk.astype(BF16), mv.astype(BF16), lw, g_next, PROMPT_CFG)
        for lst, val in zip(outs_p, (k_n, v_n, ki_n, wkv_n, sh_n,
                                     mk.reshape(bp, n_mem, N_HEADS_M, HEAD_DIM_M),
                                     mv.reshape(bp, n_mem, N_HEADS_M, HEAD_DIM_M))):
            lst.append(val)
        xs, hs, k_n, v_n, ki_n, wkv_n, sh_n = _layer(
            xs, hs, bs, ts, past, cache_k[l], cache_v[l], cache_kidx[l], state_wkv[l], state_shift[l],
            cache_mem_k[l].reshape(bs, n_mem, d_m).astype(BF16), cache_mem_v[l].reshape(bs, n_mem, d_m).astype(BF16),
            lw, g_next, SAMPLE_CFG)
        for lst, val in zip(outs_s, (k_n, v_n, ki_n, wkv_n, sh_n)):
            lst.append(val)

    y_prompt = hp.reshape(bp, tp, d)
    y_sample = hs.reshape(bs, ts, d)
    return (y_prompt, y_sample) + tuple(jnp.stack(o) for o in outs_p) + tuple(jnp.stack(o) for o in outs_s)
```

```python
import functools

import jax
import jax.numpy as jnp
import numpy as np
from jax import lax
from jax.experimental import pallas as pl
from jax.experimental.pallas import tpu as pltpu

F32 = jnp.float32
BF16 = jnp.bfloat16

CHUNK = 64
ROPE_THETA = 10000.0
RMS_EPS = 1e-6
NEG_INF = -1e30
N_HEADS_A = 8
HEAD_DIM_A = 64
N_KV_A = 2
N_IDX = 4
D_IDX = 64
TOPK_MAX = 256
N_HEADS_R = 8
HEAD_DIM_R = 64
D_LORA = 64
GN_EPS = 64e-5
N_HEADS_M = 4
HEAD_DIM_M = 128
N_BRANCH = 3

LANES = 128
VMEM_LIMIT = 48 * 1024 * 1024

_NEG_BITS = int(np.array(NEG_INF, np.float32).view(np.int32))
NEG_KEY = _NEG_BITS ^ 0x7FFFFFFF
SIGN_FLIP = -(2 ** 31)


def _cparams(*sem):
    return pltpu.CompilerParams(dimension_semantics=sem, vmem_limit_bytes=VMEM_LIMIT)


def _rmsnorm_kernel(x_ref, g_ref, o_ref):
    x = x_ref[...]
    y = x * lax.rsqrt(jnp.mean(x * x, axis=-1, keepdims=True) + RMS_EPS)
    o_ref[...] = (y * g_ref[...]).astype(o_ref.dtype)


def _rmsnorm(x2d, g, out_dtype, tm):
    m, d = x2d.shape
    tm = min(tm, m)
    return pl.pallas_call(
        _rmsnorm_kernel,
        grid=(m // tm,),
        in_specs=[pl.BlockSpec((tm, d), lambda i: (i, 0)), pl.BlockSpec((1, d), lambda i: (0, 0))],
        out_specs=pl.BlockSpec((tm, d), lambda i: (i, 0)),
        out_shape=jax.ShapeDtypeStruct((m, d), out_dtype),
        compiler_params=_cparams("parallel"),
        name="rmsnorm",
    )(x2d, g.reshape(1, d))


def _matmul_kernel(x_ref, w_ref, o_ref):
    o_ref[...] = jnp.dot(x_ref[...].astype(BF16), w_ref[...], preferred_element_type=F32)


def _matmul(x2d, w, tm, tn):
    m, k = x2d.shape
    n = w.shape[1]
    tm = min(tm, m)
    return pl.pallas_call(
        _matmul_kernel,
        grid=(n // tn, m // tm),
        in_specs=[pl.BlockSpec((tm, k), lambda j, i: (i, 0)), pl.BlockSpec((k, tn), lambda j, i: (0, j))],
        out_specs=pl.BlockSpec((tm, tn), lambda j, i: (i, j)),
        out_shape=jax.ShapeDtypeStruct((m, n), F32),
        compiler_params=_cparams("parallel", "parallel"),
        name="matmul",
    )(x2d, w)


D_A = N_HEADS_A * HEAD_DIM_A
D_KV = N_KV_A * HEAD_DIM_A
K_OFF = D_A
V_OFF = K_OFF + D_KV
QI_OFF = V_OFF + D_KV
KI_OFF = QI_OFF + N_IDX * D_IDX
W_OFF = KI_OFF + D_IDX
ATT_COLS = KI_OFF + LANES
_G_K = K_OFF // LANES
_G_V = V_OFF // LANES
_G_KI = KI_OFF // LANES


def _attn_proj_kernel(h_ref, w_ref, cos_ref, sa_ref, sb_ref, o_ref, kb_ref, vb_ref, ki2_ref):
    p = jnp.dot(h_ref[...].astype(BF16), w_ref[...], preferred_element_type=F32)
    cos, sa, sb = cos_ref[...], sa_ref[...], sb_ref[...]
    lane = lax.broadcasted_iota(jnp.int32, cos.shape, 1)
    for g in range(ATT_COLS // LANES):
        x = p[:, g * LANES:(g + 1) * LANES]
        if g == _G_V:
            o_ref[:, g * LANES:(g + 1) * LANES] = x
            vb_ref[...] = x.astype(BF16)
            continue
        y = x * cos + pltpu.roll(x, LANES - 32, 1) * sa + pltpu.roll(x, 32, 1) * sb
        if g == _G_KI:
            y = jnp.where(lane < D_IDX, y, x)
            ki2_ref[...] = jnp.where(lane < D_IDX, y, pltpu.roll(y, D_IDX, 1)).astype(BF16)
        if g == _G_K:
            kb_ref[...] = y.astype(BF16)
        o_ref[:, g * LANES:(g + 1) * LANES] = y


def _rope_tables(pos0, t):
    d = HEAD_DIM_A
    inv = ROPE_THETA ** (-jnp.arange(0, d, 2, dtype=F32) / d)
    ang = (pos0 + jnp.arange(t)).astype(F32)[:, None] * inv[None, :]
    cos, sin = jnp.cos(ang), jnp.sin(ang)
    zero = jnp.zeros_like(sin)
    cos_f = jnp.concatenate([cos] * 4, axis=-1)
    sin_a = jnp.concatenate([-sin, zero] * 2, axis=-1)
    sin_b = jnp.concatenate([zero, sin] * 2, axis=-1)
    return cos_f, sin_a, sin_b


def _attn_proj(h2d, w_att, tables, t, tm):
    m, k = h2d.shape
    nt = t // tm
    tab_spec = pl.BlockSpec((tm, LANES), lambda i: (i % nt, 0))
    lane_spec = pl.BlockSpec((tm, LANES), lambda i: (i, 0))
    lane_shape = jax.ShapeDtypeStruct((m, LANES), BF16)
    return pl.pallas_call(
        _attn_proj_kernel,
        grid=(m // tm,),
        in_specs=[pl.BlockSpec((tm, k), lambda i: (i, 0)), pl.BlockSpec((k, ATT_COLS), lambda i: (0, 0)),
                  tab_spec, tab_spec, tab_spec],
        out_specs=[pl.BlockSpec((tm, ATT_COLS), lambda i: (i, 0)), lane_spec, lane_spec, lane_spec],
        out_shape=[jax.ShapeDtypeStruct((m, ATT_COLS), F32), lane_shape, lane_shape, lane_shape],
        compiler_params=_cparams("parallel"),
        name="attn_proj",
    )(h2d, w_att, *tables)


I16 = jnp.int16
I16_MIN = -(2 ** 15)
I16_MAX = 2 ** 15 - 1
I16_ROWS = 16
I32_ROWS = 8
LOG2E = 1.4426950408889634


def _tree(parts, op):
    while len(parts) > 1:
        parts = [op(parts[i], parts[i + 1]) if i + 1 < len(parts) else parts[i] for i in range(0, len(parts), 2)]
    return parts[0]


def _fold_rows(x, rows, op):
    return _tree([x[g * rows:(g + 1) * rows] for g in range(x.shape[0] // rows)], op)


def _dsa_kernel(pa_ref, ki_ref, k_ref, v_ref, o_ref, key_sc, hi_sc, lo_sc, *, tq, tk, pos0, t_real, s_real, topk):
    qb = pl.program_id(1)
    q0 = pos0 + qb * tq
    n_adm = jnp.minimum(q0 + tq, s_real)
    n_kt = (n_adm + tk - 1) // tk
    hd = HEAD_DIM_A
    group = N_HEADS_A // N_KV_A
    cols = group * tq

    q_pos = q0 + lax.broadcasted_iota(jnp.int32, (1, tq), 1)
    key_limit = jnp.minimum((lax.shift_right_arithmetic(q_pos, CHUNK.bit_length() - 1) + 1) * CHUNK, s_real)
    key_iota = lax.broadcasted_iota(jnp.int32, (tk, tq), 0)
    low_half = lax.broadcasted_iota(jnp.int32, (tq, LANES), 1) < hd
    halves = (low_half, jnp.logical_not(low_half))
    real_q = qb * tq + lax.broadcasted_iota(jnp.int32, (1, tq), 1) < t_real

    pa = pa_ref[0]
    qi = jnp.concatenate(
        [jnp.where(halves[h % 2], pa[:, QI_OFF + (h // 2) * LANES:QI_OFF + (h // 2 + 1) * LANES] * (D_IDX ** -0.5), 0.0)
         for h in range(N_IDX)], axis=0).astype(BF16)
    w_t = pa[:, KI_OFF:KI_OFF + LANES].T * (N_IDX ** -0.5)
    w_rows = [w_t[D_IDX + h:D_IDX + h + 1, :] for h in range(N_IDX)]

    def score_body(kt, carry):
        start = pl.multiple_of(kt * tk, tk)
        s = lax.dot_general(ki_ref[0, pl.ds(start, tk), :], qi, (((1,), (1,)), ((), ())),
                            preferred_element_type=F32)
        score = jnp.zeros((tk, tq), F32)
        for h in range(N_IDX):
            score = score + jnp.maximum(s[:, h * tq:(h + 1) * tq], 0.0) * w_rows[h]
        score = score + 0.0
        score = jnp.where(start + key_iota < key_limit, score, NEG_INF)
        bits = pltpu.bitcast(score, jnp.int32)
        key = jnp.where(bits < 0, bits ^ 0x7FFFFFFF, bits)
        key_sc[kt] = key
        hi_sc[kt] = lax.shift_right_arithmetic(key, 16).astype(I16)
        return carry

    lax.fori_loop(0, n_kt, score_body, 0)

    def count(pred_fn):
        def body(kt, acc):
            hit = jnp.where(pred_fn(key_sc[kt], kt), 1, 0)
            return acc + _fold_rows(hit, I32_ROWS, jnp.add)

        acc = lax.fori_loop(0, n_kt, body, jnp.zeros((I32_ROWS, tq), jnp.int32))
        return jnp.sum(acc, axis=0, keepdims=True)

    def count16(ref, c_of_tile, cmp):
        def body(kt, acc):
            hit = jnp.where(cmp(ref[kt], c_of_tile(kt).astype(I16)), jnp.ones((), I16), jnp.zeros((), I16))
            return acc + _fold_rows(hit, I16_ROWS, jnp.add)

        acc = lax.fori_loop(0, n_kt, body, jnp.zeros((I16_ROWS, tq), I16))
        return jnp.sum(acc.astype(jnp.int32), axis=0, keepdims=True)

    def kth_digit(ref, kk):
        def body(i, prefix):
            cand = prefix | lax.shift_left(jnp.int32(1), 15 - i)
            cnt = count16(ref, lambda kt: cand + I16_MIN, lax.ge)
            return jnp.where(cnt >= kk, cand, prefix)

        return lax.fori_loop(0, 16, body, jnp.zeros((1, tq), jnp.int32))

    hi_thr = kth_digit(hi_sc, topk) + I16_MIN
    above = jnp.where(hi_thr < I16_MAX, count16(hi_sc, lambda kt: jnp.minimum(hi_thr + 1, I16_MAX), lax.ge), 0)

    def lo_body(kt, carry):
        key = key_sc[kt]
        in_bucket = lax.shift_right_arithmetic(key, 16) == hi_thr
        lo_sc[kt] = jnp.where(in_bucket, (key & 0xFFFF) + I16_MIN, I16_MIN).astype(I16)
        return carry

    lax.fori_loop(0, n_kt, lo_body, 0)
    lo_thr = kth_digit(lo_sc, topk - above)
    thr = jnp.maximum(lax.shift_left(hi_thr, 16) | lo_thr, NEG_KEY)
    cnt_gt = count(lambda key, kt: key > thr)
    cnt_ge = count(lambda key, kt: key >= thr)
    need = topk - cnt_gt
    real_thr = thr != NEG_KEY
    has_excess = jnp.max(jnp.where(real_thr & real_q & (cnt_ge > topk), 1, 0)) > 0

    n_idx_bits = max(1, int(np.ceil(np.log2(key_sc.shape[0] * tk))))

    def tie_search():
        def mark_body(kt, carry):
            lo_sc[kt] = jnp.where(key_sc[kt] == thr, key_iota, I16_MAX).astype(I16)
            return carry

        lax.fori_loop(0, n_kt, mark_body, 0)

        def jbody(i, jq):
            cand = jq | lax.shift_left(jnp.int32(1), n_idx_bits - 1 - i)
            cnt = count16(lo_sc, lambda kt: jnp.clip(cand - kt * tk, I16_MIN, I16_MAX), lax.lt)
            return jnp.where(cnt < need, cand, jq)

        return lax.fori_loop(0, n_idx_bits, jbody, jnp.zeros((1, tq), jnp.int32))

    j_cut = lax.cond(has_excess, tie_search, lambda: jnp.full((1, tq), 2 ** 30, jnp.int32))
    j_cut = jnp.where(real_thr, j_cut, -1)

    qs = []
    for j in range(N_KV_A):
        parts = []
        for g in range(group):
            h = j * group + g
            grp = pa[:, (h // 2) * LANES:(h // 2 + 1) * LANES] * (hd ** -0.5)
            if h % 2 != j:
                grp = pltpu.roll(grp, hd, 1)
            parts.append(jnp.where(halves[j], grp, 0.0))
        qs.append(jnp.concatenate(parts, axis=0).astype(BF16))

    def attn_body(kt, carry):
        start = pl.multiple_of(kt * tk, tk)
        key = key_sc[kt]
        sel = (key > thr) | ((key == thr) & (start + key_iota <= j_cut))
        bias = jnp.where(sel, 0.0, NEG_INF).astype(F32)
        bias = jnp.concatenate([bias] * group, axis=1)
        k_tile = k_ref[0, pl.ds(start, tk), :]
        heads = range(N_KV_A)
        logits = [lax.dot_general(k_tile, qs[j], (((1,), (1,)), ((), ())), preferred_element_type=F32) * LOG2E
                  + bias for j in heads]
        m_new = [jnp.maximum(carry[j][0], jnp.max(_fold_rows(logits[j], I32_ROWS, jnp.maximum), axis=0,
                                                   keepdims=True)) for j in heads]
        alpha = [jnp.exp2(carry[j][0] - m_new[j]) for j in heads]
        p = [jnp.exp2(logits[j] - m_new[j]) for j in heads]
        l_new = [carry[j][1] * alpha[j] + jnp.sum(_fold_rows(p[j], I32_ROWS, jnp.add), axis=0, keepdims=True)
                 for j in heads]
        v_tile = v_ref[0, pl.ds(start, tk), :]
        pv = [lax.dot_general(p[j].astype(BF16), v_tile, (((0,), (0,)), ((), ())), preferred_element_type=F32)
              for j in heads]
        acc = [carry[j][2] * alpha[j] + pv[j].T[j * hd:(j + 1) * hd] for j in heads]
        return tuple((m_new[j], l_new[j], acc[j]) for j in heads)

    init = tuple((jnp.full((1, cols), 0.1 * NEG_INF, F32), jnp.zeros((1, cols), F32),
                  jnp.zeros((hd, cols), F32)) for _ in range(N_KV_A))
    res = lax.fori_loop(0, n_kt, attn_body, init)
    outs = [acc / l_i for _, l_i, acc in res]
    for grp_i in range(N_HEADS_A // 2):
        pair = []
        for h in (2 * grp_i, 2 * grp_i + 1):
            j, g = h // group, h % group
            pair.append(outs[j][:, g * tq:(g + 1) * tq])
        o_ref[0, :, grp_i * LANES:(grp_i + 1) * LANES] = jnp.concatenate(pair, axis=0).T


def _dsa(pa, ki_all, k_all, v_all, *, pos0, s_real, tq, tk):
    b, t, _ = pa.shape
    assert tk < I16_MAX
    t_pad = _round_up(t, tq)
    pa = jnp.pad(pa, ((0, 0), (0, t_pad - t), (0, 0)))
    s_pad = ki_all.shape[1]
    topk = min(TOPK_MAX, s_real // 4)
    kern = functools.partial(_dsa_kernel, tq=tq, tk=tk, pos0=pos0, t_real=t, s_real=s_real, topk=topk)
    kv_spec = pl.BlockSpec((1, s_pad, LANES), lambda bi, qi: (bi, 0, 0))
    out = pl.pallas_call(
        kern,
        grid=(b, t_pad // tq),
        in_specs=[pl.BlockSpec((1, tq, ATT_COLS), lambda bi, qi: (bi, qi, 0)), kv_spec, kv_spec, kv_spec],
        out_specs=pl.BlockSpec((1, tq, D_A), lambda bi, qi: (bi, qi, 0)),
        out_shape=jax.ShapeDtypeStruct((b, t_pad, D_A), F32),
        scratch_shapes=[pltpu.VMEM((s_pad // tk, tk, tq), jnp.int32), pltpu.VMEM((s_pad // tk, tk, tq), I16),
                        pltpu.VMEM((s_pad // tk, tk, tq), I16)],
        compiler_params=_cparams("parallel", "arbitrary"),
        name="dsa",
    )(pa, ki_all, k_all, v_all)
    return out[:, :t]


D_R = N_HEADS_R * HEAD_DIM_R
D_SHIFT = 3 * D_R + 2 * D_LORA
N_PAIR = D_R // LANES
N_VEC = 8


def _split2(x):
    hi = x.astype(BF16)
    lo = (x - hi.astype(F32)).astype(BF16)
    return hi, lo


def _mm(a, b, ca, cb):
    return lax.dot_general(a, b, (((ca,), (cb,)), ((), ())), preferred_element_type=F32)


def _rwkv_kernel(sh_ref, sh0_ref, mu_ref, vec_ref, wl_ref, s0_ref, y_ref, sout_ref, st_sc, row_sc, *, bb, c):
    ci = pl.program_id(1)
    n = HEAD_DIM_R
    c2 = 2 * c

    @pl.when(ci == 0)
    def _():
        st_sc[...] = s0_ref[...]
        row_sc[...] = sh0_ref[...]

    head0 = lax.broadcasted_iota(jnp.int32, (c, LANES), 1) < n
    r2 = lax.broadcasted_iota(jnp.int32, (c2, c2), 0)
    q2 = lax.broadcasted_iota(jnp.int32, (c2, c2), 1)
    same_head = (r2 // c) == (q2 // c)
    strict = same_head & (r2 > q2)
    incl = same_head & (r2 >= q2)
    eye2 = jnp.where(r2 == q2, 1.0, 0.0).astype(F32)
    rc = lax.broadcasted_iota(jnp.int32, (c, c), 0)
    cc = lax.broadcasted_iota(jnp.int32, (c, c), 1)
    tri = jnp.where(rc >= cc, 1.0, 0.0).astype(BF16)
    tri3 = jnp.concatenate([tri, tri, tri], axis=1)
    la = lax.broadcasted_iota(jnp.int32, (LANES, LANES), 0)
    lb = lax.broadcasted_iota(jnp.int32, (LANES, LANES), 1)
    seg = jnp.where((la // n) == (lb // n), 1.0, 0.0).astype(BF16)
    seg2 = jnp.concatenate([seg, seg], axis=0)
    first_row = lax.broadcasted_iota(jnp.int32, (c, D_SHIFT), 0) == 0

    def seg_sum(x):
        xh, xl = _split2(x)
        return jnp.dot(jnp.concatenate([xh, xl], axis=1), seg2, preferred_element_type=F32)

    def vec(i, sl=slice(None)):
        return vec_ref[i:i + 1, sl]

    n_double = 0
    while (2 << n_double) < c:
        n_double += 1

    def stack(zz):
        return jnp.concatenate([jnp.where(head0, zz, 0.0), jnp.where(head0, 0.0, zz)], axis=0)

    chains = [(bi, p) for bi in range(bb) for p in range(N_PAIR)]
    states = [st_sc[bi, p] for bi, p in chains]

    prep = []
    for bi in range(bb):
        sh = sh_ref[bi]
        prev = jnp.where(first_row, row_sc[bi], pltpu.roll(sh, 1, 0))
        x = sh + (prev - sh) * mu_ref[...]
        r = x[:, :D_R]
        k = x[:, D_R:2 * D_R]
        v = x[:, 2 * D_R:3 * D_R]
        xl = x[:, 3 * D_R:]
        lora_in = jnp.where(head0, jnp.tanh(xl), xl)
        wa = jnp.dot(lora_in.astype(BF16), wl_ref[...], preferred_element_type=F32)
        w_raw = vec(0) + wa[:, :D_R]
        a = jax.nn.sigmoid(vec(1) + wa[:, D_R:])
        lw = -jnp.exp(-jax.nn.softplus(-w_raw) - 0.5)
        kkv = k * vec(2)
        kk2 = kkv * kkv
        ss = jnp.concatenate([seg_sum(kk2[:, p * LANES:(p + 1) * LANES]) for p in range(N_PAIR)], axis=1)
        kk = kkv / jnp.maximum(jnp.sqrt(ss), 1e-12)
        k2 = k * (1.0 + (a - 1.0) * vec(3))
        l_hi = lw.astype(BF16)
        l_r1 = lw - l_hi.astype(F32)
        l_mid = l_r1.astype(BF16)
        l_lo = (l_r1 - l_mid.astype(F32)).astype(BF16)
        cum = jnp.dot(tri3, jnp.concatenate([l_hi, l_mid, l_lo], axis=0), preferred_element_type=F32)
        p_fwd = jnp.exp(cum)
        p_inv = jnp.exp(-cum)
        prep.append(dict(v=v, p_fwd=p_fwd, at=-kk * jnp.exp(cum - lw), bt=kk * a * p_inv, kt=k2 * p_inv,
                         rt=r * p_fwd, rk_prod=r * k2 * vec(4)))
    for bi in range(bb):
        row_sc[bi] = sh_ref[bi, c - 1:c, :]

    ops = []
    for bi, p in chains:
        sl = slice(p * LANES, (p + 1) * LANES)
        d = prep[bi]
        b_st, k_st, v_st = stack(d["bt"][:, sl]), stack(d["kt"][:, sl]), stack(d["v"][:, sl])
        ar = jnp.concatenate([stack(d["at"][:, sl]), stack(d["rt"][:, sl])], axis=0).astype(BF16)
        bk = jnp.concatenate([b_st, k_st], axis=0).astype(BF16)
        p_last = d["p_fwd"][c - 1:c, sl]
        ops.append(dict(ar=ar, bk=bk, v_b=v_st.astype(BF16), p_last=p_last,
                        b_dec=(b_st * p_last).astype(BF16), k_dec=(k_st * p_last).astype(BF16)))
    grams = [_mm(o["ar"], o["bk"], 1, 1) for o in ops]
    l_ab = [jnp.where(strict, g[:c2, :c2], 0.0) for g in grams]
    l_ak = [jnp.where(strict, g[:c2, c2:], 0.0).astype(BF16) for g in grams]
    m_rb = [jnp.where(incl, g[c2:, :c2], 0.0).astype(BF16) for g in grams]
    m_rk = [jnp.where(incl, g[c2:, c2:], 0.0).astype(BF16) for g in grams]
    inv = [eye2 + l for l in l_ab]
    lp = l_ab
    for _ in range(n_double):
        lpb = [l.astype(BF16) for l in lp]
        lp = [_mm(l, l, 1, 0) for l in lpb]
        inv = [x + _mm(x.astype(BF16), l.astype(BF16), 1, 0) for x, l in zip(inv, lp)]

    s_b = [s.astype(BF16) for s in states]
    z = [_mm(o["ar"][:c2], sb, 1, 1) + _mm(la, o["v_b"], 1, 0) for o, sb, la in zip(ops, s_b, l_ak)]
    u_b = [_mm(x.astype(BF16), zz.astype(BF16), 1, 0).astype(BF16) for x, zz in zip(inv, z)]
    y_st = [_mm(o["ar"][c2:], sb, 1, 1) + _mm(mb, u, 1, 0) + _mm(mk, o["v_b"], 1, 0)
            for o, sb, mb, mk, u in zip(ops, s_b, m_rb, m_rk, u_b)]
    new_states = [s * o["p_last"] + _mm(u, o["b_dec"], 0, 0) + _mm(o["v_b"], o["k_dec"], 0, 0)
                  for s, o, u in zip(states, ops, u_b)]
    for (bi, p), s_new, ys in zip(chains, new_states, y_st):
        sl = slice(p * LANES, (p + 1) * LANES)
        st_sc[bi, p] = s_new
        y = ys[:c] + ys[c:]
        dev = y - seg_sum(y) * (1.0 / n)
        var = seg_sum(dev * dev) * (1.0 / n)
        yn = dev * lax.rsqrt(var + GN_EPS) * vec(5, sl) + vec(6, sl)
        y_ref[bi, :, sl] = yn + seg_sum(prep[bi]["rk_prod"][:, sl]) * prep[bi]["v"][:, sl]

    @pl.when(ci == pl.num_programs(1) - 1)
    def _():
        sout_ref[...] = st_sc[...]


def _rwkv(sh, shift0, mu, w0, w2, a0, a2, k_k, k_a, r_k, ln_w, ln_b, state0, *, bb, c):
    b, t, _ = sh.shape
    n = HEAD_DIM_R
    eye = jnp.eye(2, dtype=F32)
    s0 = jnp.einsum('bphij,hg->bphigj', state0.reshape(b, N_PAIR, 2, n, n), eye).reshape(b, N_PAIR, LANES, LANES)
    vecs = jnp.stack([w0, a0, k_k, k_a, r_k.reshape(D_R), ln_w, ln_b, jnp.zeros_like(w0)])
    zero = jnp.zeros((D_LORA, D_R), F32)
    w_lora = jnp.concatenate([jnp.concatenate([w2, zero], axis=1), jnp.concatenate([zero, a2], axis=1)],
                             axis=0).astype(BF16)
    full = lambda shape: pl.BlockSpec(shape, lambda bi, ci: (0,) * len(shape))
    st_spec = pl.BlockSpec((bb, N_PAIR, LANES, LANES), lambda bi, ci: (bi, 0, 0, 0))
    y, s_out = pl.pallas_call(
        functools.partial(_rwkv_kernel, bb=bb, c=c),
        grid=(b // bb, t // c),
        in_specs=[pl.BlockSpec((bb, c, D_SHIFT), lambda bi, ci: (bi, ci, 0)),
                  pl.BlockSpec((bb, 1, D_SHIFT), lambda bi, ci: (bi, 0, 0)),
                  full((1, D_SHIFT)), full((N_VEC, D_R)), full((2 * D_LORA, 2 * D_R)), st_spec],
        out_specs=[pl.BlockSpec((bb, c, D_R), lambda bi, ci: (bi, ci, 0)), st_spec],
        out_shape=[jax.ShapeDtypeStruct((b, t, D_R), F32),
                   jax.ShapeDtypeStruct((b, N_PAIR, LANES, LANES), F32)],
        scratch_shapes=[pltpu.VMEM((bb, N_PAIR, LANES, LANES), F32), pltpu.VMEM((bb, 1, D_SHIFT), F32)],
        compiler_params=_cparams("parallel", "arbitrary"),
        name="rwkv",
    )(sh, shift0, mu.reshape(1, D_SHIFT), vecs, w_lora, s0)
    s_t = jnp.einsum('bphigj,hg->bphij', s_out.reshape(b, N_PAIR, 2, n, 2, n), eye)
    return y, s_t.reshape(b, N_HEADS_R, n, n)


def _mem_attn_kernel(q_ref, mk_ref, mv_ref, o_ref):
    scale = HEAD_DIM_M ** -0.5
    for h in range(N_HEADS_M):
        sl = slice(h * HEAD_DIM_M, (h + 1) * HEAD_DIM_M)
        q = q_ref[0, :, sl].astype(BF16)
        logits = lax.dot_general(q, mk_ref[0, :, sl], (((1,), (1,)), ((), ())), preferred_element_type=F32) * scale
        e = jnp.exp(logits - jnp.max(logits, axis=1, keepdims=True))
        p = e / jnp.sum(e, axis=1, keepdims=True)
        o_ref[0, :, sl] = jnp.dot(p.astype(BF16), mv_ref[0, :, sl], preferred_element_type=F32)


def _mem_attn(qm, mk, mv, tm):
    b, t, d = qm.shape
    nm = mk.shape[1]
    return pl.pallas_call(
        _mem_attn_kernel,
        grid=(b, t // tm),
        in_specs=[pl.BlockSpec((1, tm, d), lambda bi, i: (bi, i, 0)),
                  pl.BlockSpec((1, nm, d), lambda bi, i: (bi, 0, 0)),
                  pl.BlockSpec((1, nm, d), lambda bi, i: (bi, 0, 0))],
        out_specs=pl.BlockSpec((1, tm, d), lambda bi, i: (bi, i, 0)),
        out_shape=jax.ShapeDtypeStruct((b, t, d), F32),
        compiler_params=_cparams("parallel", "parallel"),
        name="mem_attn",
    )(qm, mk, mv)


def _merge_kernel(oa_ref, yr_ref, om_ref, zg_ref, x_ref, wa_ref, wr_ref, wm_ref, wo_ref, g_ref, xo_ref, ho_ref):
    d_br = oa_ref.shape[1]
    d = x_ref.shape[1]

    def branch(o, z, w_ref):
        return jnp.dot((o * jax.nn.silu(z)).astype(BF16), w_ref[...], preferred_element_type=F32)

    pa = branch(oa_ref[...], zg_ref[:, 0:d_br], wa_ref)
    pr = branch(yr_ref[...], zg_ref[:, d_br:2 * d_br], wr_ref)
    pm = branch(om_ref[...], zg_ref[:, 2 * d_br:3 * d_br], wm_ref)
    g0 = 3 * d_br
    merged = (jax.nn.sigmoid(zg_ref[:, g0:g0 + d]) * pa
              + jax.nn.sigmoid(zg_ref[:, g0 + d:g0 + 2 * d]) * pr
              + jax.nn.sigmoid(zg_ref[:, g0 + 2 * d:g0 + 3 * d]) * pm)
    xn = x_ref[...] + jnp.dot(merged.astype(BF16), wo_ref[...], preferred_element_type=F32)
    xo_ref[...] = xn
    y = xn * lax.rsqrt(jnp.mean(xn * xn, axis=-1, keepdims=True) + RMS_EPS)
    ho_ref[...] = y * g_ref[...]


def _merge(oa, yr, om, zg, x2d, wa, wr, wm, wo, g_next, tm):
    m, d = x2d.shape
    d_br = oa.shape[1]
    tm = min(tm, m)
    row = lambda w: pl.BlockSpec((tm, w), lambda i: (i, 0))
    full = lambda a: pl.BlockSpec(a.shape, lambda i: (0, 0))
    g2 = g_next.reshape(1, d)
    return pl.pallas_call(
        _merge_kernel,
        grid=(m // tm,),
        in_specs=[row(d_br), row(d_br), row(d_br), row(zg.shape[1]), row(d),
                  full(wa), full(wr), full(wm), full(wo), full(g2)],
        out_specs=[row(d), row(d)],
        out_shape=[jax.ShapeDtypeStruct((m, d), F32), jax.ShapeDtypeStruct((m, d), F32)],
        compiler_params=_cparams("parallel"),
        name="merge",
    )(oa, yr, om, zg, x2d, wa, wr, wm, wo, g2)


def _round_up(x, m):
    return (x + m - 1) // m * m


IN_SIZES = (D_A, D_KV, D_KV, N_IDX * D_IDX, D_IDX, N_IDX, D_A, D_SHIFT, D_R, N_HEADS_M * HEAD_DIM_M,
            N_HEADS_M * HEAD_DIM_M)
IN_OFFS = tuple(int(o) for o in np.cumsum((0,) + IN_SIZES))


def _repack_kernel(w_ref, att_ref, sh_ref, qm_ref, zg_ref):
    col = lambda i: w_ref[:, IN_OFFS[i]:IN_OFFS[i + 1]]
    rows = w_ref.shape[0]
    att_ref[:, :W_OFF + N_IDX] = w_ref[:, :IN_OFFS[6]].astype(BF16)
    att_ref[:, W_OFF + N_IDX:] = jnp.zeros((rows, ATT_COLS - W_OFF - N_IDX), BF16)
    sh_ref[...] = col(7).astype(BF16)
    qm_ref[...] = col(9).astype(BF16)
    zg_ref[:, :D_A] = col(6).astype(BF16)
    zg_ref[:, D_A:D_A + D_R] = col(8).astype(BF16)
    zg_ref[:, D_A + D_R:D_A + D_R + IN_SIZES[10]] = col(10).astype(BF16)
    zg_ref[:, D_A + D_R + IN_SIZES[10]:] = w_ref[:, IN_OFFS[11]:].astype(BF16)


def _split_w_in(w_in):
    d_model, d_in = w_in.shape
    n_zg = D_A + D_R + IN_SIZES[10] + d_in - IN_OFFS[11]
    tr = 128
    widths = (ATT_COLS, D_SHIFT, IN_SIZES[9], n_zg)
    return tuple(pl.pallas_call(
        _repack_kernel,
        grid=(d_model // tr,),
        in_specs=[pl.BlockSpec((tr, d_in), lambda i: (i, 0))],
        out_specs=[pl.BlockSpec((tr, w), lambda i: (i, 0)) for w in widths],
        out_shape=[jax.ShapeDtypeStruct((d_model, w), BF16) for w in widths],
        compiler_params=_cparams("parallel"),
        name="repack",
    )(w_in))


def _layer(x2d, h2d, b, t, pos0, past_k, past_v, past_ki, wkv0, shift0, mem_k, mem_v, lw, g_next, cfg):
    (w_att, w_sh, w_qm, w_zg, mu_shift, w0, w2, a0, a2, k_k, k_a, r_k, ln_w, ln_b, w_br_a, w_br_r, w_br_m,
     w_out) = lw
    tm = cfg["tm"]
    d_a = N_HEADS_A * HEAD_DIM_A
    d_kv = N_KV_A * HEAD_DIM_A
    d_r = N_HEADS_R * HEAD_DIM_R
    d_m = N_HEADS_M * HEAD_DIM_M

    tables = _rope_tables(pos0, t)
    pa, kb, vb, ki2 = _attn_proj(h2d, w_att, tables, t, min(tm, t))
    pa = pa.reshape(b, t, ATT_COLS)
    sh = _matmul(h2d, w_sh, tm, w_sh.shape[1]).reshape(b, t, -1)
    qm = _matmul(h2d, w_qm, tm, w_qm.shape[1]).reshape(b, t, d_m)
    zg = _matmul(h2d, w_zg, tm, cfg["tn_zg"])

    k_new = pa[..., K_OFF:K_OFF + D_KV].reshape(b, t, N_KV_A, HEAD_DIM_A)
    v_new = pa[..., V_OFF:V_OFF + D_KV].reshape(b, t, N_KV_A, HEAD_DIM_A)
    ki_new = pa[..., KI_OFF:KI_OFF + D_IDX]
    s_real = pos0 + t
    tq, tk = cfg["tq"], cfg["tk"]
    s_pad = _round_up(s_real, tk)

    def with_past(past, new):
        full = jnp.concatenate([past.reshape(b, pos0, LANES).astype(BF16), new.reshape(b, t, LANES)], axis=1)
        return jnp.pad(full, ((0, 0), (0, s_pad - s_real), (0, 0)))

    o_a = _dsa(pa, with_past(jnp.concatenate([past_ki, past_ki], axis=-1), ki2), with_past(past_k, kb),
               with_past(past_v, vb), pos0=pos0, s_real=s_real, tq=tq, tk=tk).reshape(b * t, d_a)

    y_r, wkv_t = _rwkv(sh, shift0, mu_shift, w0, w2, a0, a2, k_k, k_a, r_k, ln_w, ln_b, wkv0,
                       bb=cfg["bb"], c=CHUNK)
    y_r = y_r.reshape(b * t, d_r)

    o_m = _mem_attn(qm, mem_k, mem_v, min(tm, t)).reshape(b * t, d_m)

    x_new, h_next = _merge(o_a, y_r, o_m, zg, x2d, w_br_a, w_br_r, w_br_m, w_out, g_next, cfg["tm_merge"])
    return x_new, h_next, k_new, v_new, ki_new, wkv_t, sh[:, -1:]


PROMPT_CFG = dict(tm=512, tn_zg=2304, tq=128, tk=1024, bb=2, tm_merge=256)
SAMPLE_CFG = dict(tm=512, tn_zg=2304, tq=128, tk=1152, bb=2, tm_merge=256)


def kernel(x_prompt, x_sample, mem_prompt, cache_k, cache_v, cache_kidx, state_wkv, state_shift, cache_mem_k,
           cache_mem_v, norm_g, w_in, mu_shift, w0, w2, a0, a2, k_k, k_a, r_k, ln_x_w, ln_x_b, mem_norm_g,
           w_mem_kv, w_br_a, w_br_r, w_br_m, w_out, final_norm_g):
    depth = w_in.shape[0]
    bp, tp, d = x_prompt.shape
    bs, ts, _ = x_sample.shape
    n_mem = mem_prompt.shape[1]
    d_m = N_HEADS_M * HEAD_DIM_M
    past = cache_k.shape[2]

    xp = x_prompt.reshape(bp * tp, d)
    xs = x_sample.reshape(bs * ts, d)
    hp = _rmsnorm(xp, norm_g[0], F32, 512)
    hs = _rmsnorm(xs, norm_g[0], F32, 512)
    mem2d = mem_prompt.reshape(bp * n_mem, d)

    empty_kv = jnp.zeros((bp, 0, N_KV_A, HEAD_DIM_A), F32)
    empty_ki = jnp.zeros((bp, 0, D_IDX), F32)
    zero_state = jnp.zeros((bp, N_HEADS_R, HEAD_DIM_R, HEAD_DIM_R), F32)
    zero_shift = jnp.zeros((bp, 1, mu_shift.shape[1]), F32)

    outs_p = [[] for _ in range(7)]
    outs_s = [[] for _ in range(5)]
    for l in range(depth):
        g_next = norm_g[l + 1] if l + 1 < depth else final_norm_g
        lw = _split_w_in(w_in[l]) + (mu_shift[l], w0[l], w2[l], a0[l], a2[l], k_k[l], k_a[l], r_k[l], ln_x_w[l],
                                     ln_x_b[l], w_br_a[l].astype(BF16), w_br_r[l].astype(BF16),
                                     w_br_m[l].astype(BF16), w_out[l].astype(BF16))
        mem_h = _rmsnorm(mem2d, mem_norm_g[l], F32, 512)
        mkv = _matmul(mem_h, w_mem_kv[l].astype(BF16), 512, 2 * d_m).reshape(bp, n_mem, 2 * d_m)
        mk, mv = mkv[..., :d_m], mkv[..., d_m:]
        xp, hp, k_n, v_n, ki_n, wkv_n, sh_n = _layer(
            xp, hp, bp, tp, 0, empty_kv, empty_kv, empty_ki, zero_state, zero_shift,
            mk.astype(BF16), mv.astype(BF16), lw, g_next, PROMPT_CFG)
        for lst, val in zip(outs_p, (k_n, v_n, ki_n, wkv_n, sh_n,
                                     mk.reshape(bp, n_mem, N_HEADS_M, HEAD_DIM_M),
                                     mv.reshape(bp, n_mem, N_HEADS_M, HEAD_DIM_M))):
            lst.append(val)
        xs, hs, k_n, v_n, ki_n, wkv_n, sh_n = _layer(
            xs, hs, bs, ts, past, cache_k[l], cache_v[l], cache_kidx[l], state_wkv[l], state_shift[l],
            cache_mem_k[l].reshape(bs, n_mem, d_m).astype(BF16), cache_mem_v[l].reshape(bs, n_mem, d_m).astype(BF16),
            lw, g_next, SAMPLE_CFG)
        for lst, val in zip(outs_s, (k_n, v_n, ki_n, wkv_n, sh_n)):
            lst.append(val)

    y_prompt = hp.reshape(bp, tp, d)
    y_sample = hs.reshape(bs, ts, d)
    return (y_prompt, y_sample) + tuple(jnp.stack(o) for o in outs_p) + tuple(jnp.stack(o) for o in outs_s)
```

```python
import functools

import jax
import jax.numpy as jnp
import numpy as np
from jax import lax
from jax.experimental import pallas as pl
from jax.experimental.pallas import tpu as pltpu

F32 = jnp.float32
BF16 = jnp.bfloat16

CHUNK = 64
ROPE_THETA = 10000.0
RMS_EPS = 1e-6
NEG_INF = -1e30
N_HEADS_A = 8
HEAD_DIM_A = 64
N_KV_A = 2
N_IDX = 4
D_IDX = 64
TOPK_MAX = 256
N_HEADS_R = 8
HEAD_DIM_R = 64
D_LORA = 64
GN_EPS = 64e-5
N_HEADS_M = 4
HEAD_DIM_M = 128
N_BRANCH = 3

LANES = 128
VMEM_LIMIT = 48 * 1024 * 1024

_NEG_BITS = int(np.array(NEG_INF, np.float32).view(np.int32))
NEG_KEY = _NEG_BITS ^ 0x7FFFFFFF
SIGN_FLIP = -(2 ** 31)


def _cparams(*sem):
    return pltpu.CompilerParams(dimension_semantics=sem, vmem_limit_bytes=VMEM_LIMIT)


def _rmsnorm_kernel(x_ref, g_ref, o_ref):
    x = x_ref[...]
    y = x * lax.rsqrt(jnp.mean(x * x, axis=-1, keepdims=True) + RMS_EPS)
    o_ref[...] = (y * g_ref[...]).astype(o_ref.dtype)


def _rmsnorm(x2d, g, out_dtype, tm):
    m, d = x2d.shape
    tm = min(tm, m)
    return pl.pallas_call(
        _rmsnorm_kernel,
        grid=(m // tm,),
        in_specs=[pl.BlockSpec((tm, d), lambda i: (i, 0)), pl.BlockSpec((1, d), lambda i: (0, 0))],
        out_specs=pl.BlockSpec((tm, d), lambda i: (i, 0)),
        out_shape=jax.ShapeDtypeStruct((m, d), out_dtype),
        compiler_params=_cparams("parallel"),
        name="rmsnorm",
    )(x2d, g.reshape(1, d))


def _matmul_kernel(x_ref, w_ref, o_ref):
    o_ref[...] = jnp.dot(x_ref[...].astype(BF16), w_ref[...], preferred_element_type=F32)


def _matmul(x2d, w, tm, tn):
    m, k = x2d.shape
    n = w.shape[1]
    tm = min(tm, m)
    return pl.pallas_call(
        _matmul_kernel,
        grid=(n // tn, m // tm),
        in_specs=[pl.BlockSpec((tm, k), lambda j, i: (i, 0)), pl.BlockSpec((k, tn), lambda j, i: (0, j))],
        out_specs=pl.BlockSpec((tm, tn), lambda j, i: (i, j)),
        out_shape=jax.ShapeDtypeStruct((m, n), F32),
        compiler_params=_cparams("parallel", "parallel"),
        name="matmul",
    )(x2d, w)


D_A = N_HEADS_A * HEAD_DIM_A
D_KV = N_KV_A * HEAD_DIM_A
K_OFF = D_A
V_OFF = K_OFF + D_KV
QI_OFF = V_OFF + D_KV
KI_OFF = QI_OFF + N_IDX * D_IDX
W_OFF = KI_OFF + D_IDX
ATT_COLS = KI_OFF + LANES
_G_K = K_OFF // LANES
_G_V = V_OFF // LANES
_G_KI = KI_OFF // LANES


def _attn_proj_kernel(h_ref, w_ref, cos_ref, sa_ref, sb_ref, o_ref, kb_ref, vb_ref, ki2_ref):
    p = jnp.dot(h_ref[...].astype(BF16), w_ref[...], preferred_element_type=F32)
    cos, sa, sb = cos_ref[...], sa_ref[...], sb_ref[...]
    lane = lax.broadcasted_iota(jnp.int32, cos.shape, 1)
    for g in range(ATT_COLS // LANES):
        x = p[:, g * LANES:(g + 1) * LANES]
        if g == _G_V:
            o_ref[:, g * LANES:(g + 1) * LANES] = x
            vb_ref[...] = x.astype(BF16)
            continue
        y = x * cos + pltpu.roll(x, LANES - 32, 1) * sa + pltpu.roll(x, 32, 1) * sb
        if g == _G_KI:
            y = jnp.where(lane < D_IDX, y, x)
            ki2_ref[...] = jnp.where(lane < D_IDX, y, pltpu.roll(y, D_IDX, 1)).astype(BF16)
        if g == _G_K:
            kb_ref[...] = y.astype(BF16)
        o_ref[:, g * LANES:(g + 1) * LANES] = y


def _rope_tables(pos0, t):
    d = HEAD_DIM_A
    inv = ROPE_THETA ** (-jnp.arange(0, d, 2, dtype=F32) / d)
    ang = (pos0 + jnp.arange(t)).astype(F32)[:, None] * inv[None, :]
    cos, sin = jnp.cos(ang), jnp.sin(ang)
    zero = jnp.zeros_like(sin)
    cos_f = jnp.concatenate([cos] * 4, axis=-1)
    sin_a = jnp.concatenate([-sin, zero] * 2, axis=-1)
    sin_b = jnp.concatenate([zero, sin] * 2, axis=-1)
    return cos_f, sin_a, sin_b


def _attn_proj(h2d, w_att, tables, t, tm):
    m, k = h2d.shape
    nt = t // tm
    tab_spec = pl.BlockSpec((tm, LANES), lambda i: (i % nt, 0))
    lane_spec = pl.BlockSpec((tm, LANES), lambda i: (i, 0))
    lane_shape = jax.ShapeDtypeStruct((m, LANES), BF16)
    return pl.pallas_call(
        _attn_proj_kernel,
        grid=(m // tm,),
        in_specs=[pl.BlockSpec((tm, k), lambda i: (i, 0)), pl.BlockSpec((k, ATT_COLS), lambda i: (0, 0)),
                  tab_spec, tab_spec, tab_spec],
        out_specs=[pl.BlockSpec((tm, ATT_COLS), lambda i: (i, 0)), lane_spec, lane_spec, lane_spec],
        out_shape=[jax.ShapeDtypeStruct((m, ATT_COLS), F32), lane_shape, lane_shape, lane_shape],
        compiler_params=_cparams("parallel"),
        name="attn_proj",
    )(h2d, w_att, *tables)


I16 = jnp.int16
I16_MIN = -(2 ** 15)
I16_MAX = 2 ** 15 - 1
I16_ROWS = 16
I32_ROWS = 8
LOG2E = 1.4426950408889634


def _tree(parts, op):
    while len(parts) > 1:
        parts = [op(parts[i], parts[i + 1]) if i + 1 < len(parts) else parts[i] for i in range(0, len(parts), 2)]
    return parts[0]


def _fold_rows(x, rows, op):
    return _tree([x[g * rows:(g + 1) * rows] for g in range(x.shape[0] // rows)], op)


def _dsa_kernel(pa_ref, ki_ref, k_ref, vt_ref, o_ref, key_sc, hi_sc, lo_sc, *, tq, tk, pos0, t_real, s_real, topk):
    qb = pl.program_id(1)
    q0 = pos0 + qb * tq
    n_adm = jnp.minimum(q0 + tq, s_real)
    n_kt = (n_adm + tk - 1) // tk
    hd = HEAD_DIM_A
    group = N_HEADS_A // N_KV_A
    cols = group * tq

    q_pos = q0 + lax.broadcasted_iota(jnp.int32, (1, tq), 1)
    key_limit = jnp.minimum((lax.shift_right_arithmetic(q_pos, CHUNK.bit_length() - 1) + 1) * CHUNK, s_real)
    key_iota = lax.broadcasted_iota(jnp.int32, (tk, tq), 0)
    low_half = lax.broadcasted_iota(jnp.int32, (tq, LANES), 1) < hd
    halves = (low_half, jnp.logical_not(low_half))
    real_q = qb * tq + lax.broadcasted_iota(jnp.int32, (1, tq), 1) < t_real

    pa = pa_ref[0]
    qi = jnp.concatenate(
        [jnp.where(halves[h % 2], pa[:, QI_OFF + (h // 2) * LANES:QI_OFF + (h // 2 + 1) * LANES] * (D_IDX ** -0.5), 0.0)
         for h in range(N_IDX)], axis=0).astype(BF16)
    w_t = pa[:, KI_OFF:KI_OFF + LANES].T * (N_IDX ** -0.5)
    w_rows = [w_t[D_IDX + h:D_IDX + h + 1, :] for h in range(N_IDX)]

    def score_body(kt, carry):
        start = pl.multiple_of(kt * tk, tk)
        s = lax.dot_general(ki_ref[0, pl.ds(start, tk), :], qi, (((1,), (1,)), ((), ())),
                            preferred_element_type=F32)
        score = jnp.zeros((tk, tq), F32)
        for h in range(N_IDX):
            score = score + jnp.maximum(s[:, h * tq:(h + 1) * tq], 0.0) * w_rows[h]
        score = score + 0.0
        score = jnp.where(start + key_iota < key_limit, score, NEG_INF)
        bits = pltpu.bitcast(score, jnp.int32)
        key = jnp.where(bits < 0, bits ^ 0x7FFFFFFF, bits)
        key_sc[kt] = key
        hi_sc[kt] = lax.shift_right_arithmetic(key, 16).astype(I16)
        return carry

    lax.fori_loop(0, n_kt, score_body, 0)

    def count(pred_fn):
        def body(kt, acc):
            hit = jnp.where(pred_fn(key_sc[kt], kt), 1, 0)
            return acc + _fold_rows(hit, I32_ROWS, jnp.add)

        acc = lax.fori_loop(0, n_kt, body, jnp.zeros((I32_ROWS, tq), jnp.int32))
        return jnp.sum(acc, axis=0, keepdims=True)

    def count16(ref, c_of_tile, cmp):
        def body(kt, acc):
            hit = jnp.where(cmp(ref[kt], c_of_tile(kt).astype(I16)), jnp.ones((), I16), jnp.zeros((), I16))
            return acc + _fold_rows(hit, I16_ROWS, jnp.add)

        acc = lax.fori_loop(0, n_kt, body, jnp.zeros((I16_ROWS, tq), I16))
        return jnp.sum(acc.astype(jnp.int32), axis=0, keepdims=True)

    def kth_digit(ref, kk):
        def body(i, prefix):
            cand = prefix | lax.shift_left(jnp.int32(1), 15 - i)
            cnt = count16(ref, lambda kt: cand + I16_MIN, lax.ge)
            return jnp.where(cnt >= kk, cand, prefix)

        return lax.fori_loop(0, 16, body, jnp.zeros((1, tq), jnp.int32))

    hi_thr = kth_digit(hi_sc, topk) + I16_MIN
    above = jnp.where(hi_thr < I16_MAX, count16(hi_sc, lambda kt: jnp.minimum(hi_thr + 1, I16_MAX), lax.ge), 0)

    def lo_body(kt, carry):
        key = key_sc[kt]
        in_bucket = lax.shift_right_arithmetic(key, 16) == hi_thr
        lo_sc[kt] = jnp.where(in_bucket, (key & 0xFFFF) + I16_MIN, I16_MIN).astype(I16)
        return carry

    lax.fori_loop(0, n_kt, lo_body, 0)
    lo_thr = kth_digit(lo_sc, topk - above)
    thr = jnp.maximum(lax.shift_left(hi_thr, 16) | lo_thr, NEG_KEY)
    cnt_gt = count(lambda key, kt: key > thr)
    cnt_ge = count(lambda key, kt: key >= thr)
    need = topk - cnt_gt
    real_thr = thr != NEG_KEY
    has_excess = jnp.max(jnp.where(real_thr & real_q & (cnt_ge > topk), 1, 0)) > 0

    n_idx_bits = max(1, int(np.ceil(np.log2(key_sc.shape[0] * tk))))

    def tie_search():
        def mark_body(kt, carry):
            lo_sc[kt] = jnp.where(key_sc[kt] == thr, key_iota, I16_MAX).astype(I16)
            return carry

        lax.fori_loop(0, n_kt, mark_body, 0)

        def jbody(i, jq):
            cand = jq | lax.shift_left(jnp.int32(1), n_idx_bits - 1 - i)
            cnt = count16(lo_sc, lambda kt: jnp.clip(cand - kt * tk, I16_MIN, I16_MAX), lax.lt)
            return jnp.where(cnt < need, cand, jq)

        return lax.fori_loop(0, n_idx_bits, jbody, jnp.zeros((1, tq), jnp.int32))

    j_cut = lax.cond(has_excess, tie_search, lambda: jnp.full((1, tq), 2 ** 30, jnp.int32))
    j_cut = jnp.where(real_thr, j_cut, -1)

    qs = []
    for j in range(N_KV_A):
        parts = []
        for g in range(group):
            h = j * group + g
            grp = pa[:, (h // 2) * LANES:(h // 2 + 1) * LANES] * (hd ** -0.5)
            if h % 2 != j:
                grp = pltpu.roll(grp, hd, 1)
            parts.append(jnp.where(halves[j], grp, 0.0))
        qs.append(jnp.concatenate(parts, axis=0).astype(BF16))

    def attn_body(kt, carry):
        start = pl.multiple_of(kt * tk, tk)
        key = key_sc[kt]
        sel = (key > thr) | ((key == thr) & (start + key_iota <= j_cut))
        bias = jnp.where(sel, 0.0, NEG_INF).astype(F32)
        bias = jnp.concatenate([bias] * group, axis=1)
        k_tile = k_ref[0, pl.ds(start, tk), :]
        heads = range(N_KV_A)
        logits = [lax.dot_general(k_tile, qs[j], (((1,), (1,)), ((), ())), preferred_element_type=F32) * LOG2E
                  + bias for j in heads]
        m_new = [jnp.maximum(carry[j][0], jnp.max(_fold_rows(logits[j], I32_ROWS, jnp.maximum), axis=0,
                                                   keepdims=True)) for j in heads]
        alpha = [jnp.exp2(carry[j][0] - m_new[j]) for j in heads]
        p = [jnp.exp2(logits[j] - m_new[j]) for j in heads]
        l_new = [carry[j][1] * alpha[j] + jnp.sum(_fold_rows(p[j], I32_ROWS, jnp.add), axis=0, keepdims=True)
                 for j in heads]
        acc = [carry[j][2] * alpha[j]
               + jnp.dot(vt_ref[0, j * hd:(j + 1) * hd, pl.ds(start, tk)], p[j].astype(BF16),
                         preferred_element_type=F32) for j in heads]
        return tuple((m_new[j], l_new[j], acc[j]) for j in heads)

    init = tuple((jnp.full((1, cols), 0.1 * NEG_INF, F32), jnp.zeros((1, cols), F32),
                  jnp.zeros((hd, cols), F32)) for _ in range(N_KV_A))
    res = lax.fori_loop(0, n_kt, attn_body, init)
    outs = [acc / l_i for _, l_i, acc in res]
    for grp_i in range(N_HEADS_A // 2):
        pair = []
        for h in (2 * grp_i, 2 * grp_i + 1):
            j, g = h // group, h % group
            pair.append(outs[j][:, g * tq:(g + 1) * tq])
        o_ref[0, :, grp_i * LANES:(grp_i + 1) * LANES] = jnp.concatenate(pair, axis=0).T


def _dsa(pa, ki_all, k_all, vt_all, *, pos0, s_real, tq, tk):
    b, t, _ = pa.shape
    assert tk < I16_MAX
    t_pad = _round_up(t, tq)
    pa = jnp.pad(pa, ((0, 0), (0, t_pad - t), (0, 0)))
    s_pad = ki_all.shape[1]
    topk = min(TOPK_MAX, s_real // 4)
    kern = functools.partial(_dsa_kernel, tq=tq, tk=tk, pos0=pos0, t_real=t, s_real=s_real, topk=topk)
    kv_spec = pl.BlockSpec((1, s_pad, LANES), lambda bi, qi: (bi, 0, 0))
    out = pl.pallas_call(
        kern,
        grid=(b, t_pad // tq),
        in_specs=[pl.BlockSpec((1, tq, ATT_COLS), lambda bi, qi: (bi, qi, 0)), kv_spec, kv_spec,
                  pl.BlockSpec((1, LANES, s_pad), lambda bi, qi: (bi, 0, 0))],
        out_specs=pl.BlockSpec((1, tq, D_A), lambda bi, qi: (bi, qi, 0)),
        out_shape=jax.ShapeDtypeStruct((b, t_pad, D_A), F32),
        scratch_shapes=[pltpu.VMEM((s_pad // tk, tk, tq), jnp.int32), pltpu.VMEM((s_pad // tk, tk, tq), I16),
                        pltpu.VMEM((s_pad // tk, tk, tq), I16)],
        compiler_params=_cparams("parallel", "arbitrary"),
        name="dsa",
    )(pa, ki_all, k_all, vt_all)
    return out[:, :t]


D_R = N_HEADS_R * HEAD_DIM_R
D_SHIFT = 3 * D_R + 2 * D_LORA
N_PAIR = D_R // LANES
N_VEC = 8


def _split2(x):
    hi = x.astype(BF16)
    lo = (x - hi.astype(F32)).astype(BF16)
    return hi, lo


def _mm(a, b, ca, cb):
    return lax.dot_general(a, b, (((ca,), (cb,)), ((), ())), preferred_element_type=F32)


def _rwkv_kernel(sh_ref, sh0_ref, mu_ref, vec_ref, wl_ref, s0_ref, y_ref, sout_ref, st_sc, row_sc, *, bb, c):
    ci = pl.program_id(1)
    n = HEAD_DIM_R
    c2 = 2 * c

    @pl.when(ci == 0)
    def _():
        st_sc[...] = s0_ref[...]
        row_sc[...] = sh0_ref[...]

    head0 = lax.broadcasted_iota(jnp.int32, (c, LANES), 1) < n
    r2 = lax.broadcasted_iota(jnp.int32, (c2, c2), 0)
    q2 = lax.broadcasted_iota(jnp.int32, (c2, c2), 1)
    same_head = (r2 // c) == (q2 // c)
    strict = same_head & (r2 > q2)
    incl = same_head & (r2 >= q2)
    eye2 = jnp.where(r2 == q2, 1.0, 0.0).astype(F32)
    rc = lax.broadcasted_iota(jnp.int32, (c, c), 0)
    cc = lax.broadcasted_iota(jnp.int32, (c, c), 1)
    tri = jnp.where(rc >= cc, 1.0, 0.0).astype(BF16)
    tri3 = jnp.concatenate([tri, tri, tri], axis=1)
    la = lax.broadcasted_iota(jnp.int32, (LANES, LANES), 0)
    lb = lax.broadcasted_iota(jnp.int32, (LANES, LANES), 1)
    seg = jnp.where((la // n) == (lb // n), 1.0, 0.0).astype(BF16)
    seg2 = jnp.concatenate([seg, seg], axis=0)
    first_row = lax.broadcasted_iota(jnp.int32, (c, D_SHIFT), 0) == 0

    def seg_sum(x):
        xh, xl = _split2(x)
        return jnp.dot(jnp.concatenate([xh, xl], axis=1), seg2, preferred_element_type=F32)

    def vec(i, sl=slice(None)):
        return vec_ref[i:i + 1, sl]

    n_double = 0
    while (2 << n_double) < c:
        n_double += 1

    def stack(zz):
        return jnp.concatenate([jnp.where(head0, zz, 0.0), jnp.where(head0, 0.0, zz)], axis=0)

    chains = [(bi, p) for bi in range(bb) for p in range(N_PAIR)]
    states = [st_sc[bi, p] for bi, p in chains]

    prep = []
    for bi in range(bb):
        sh = sh_ref[bi]
        prev = jnp.where(first_row, row_sc[bi], pltpu.roll(sh, 1, 0))
        x = sh + (prev - sh) * mu_ref[...]
        r = x[:, :D_R]
        k = x[:, D_R:2 * D_R]
        v = x[:, 2 * D_R:3 * D_R]
        xl = x[:, 3 * D_R:]
        lora_in = jnp.where(head0, jnp.tanh(xl), xl)
        wa = jnp.dot(lora_in.astype(BF16), wl_ref[...], preferred_element_type=F32)
        w_raw = vec(0) + wa[:, :D_R]
        a = jax.nn.sigmoid(vec(1) + wa[:, D_R:])
        lw = -jnp.exp(-jax.nn.softplus(-w_raw) - 0.5)
        kkv = k * vec(2)
        kk2 = kkv * kkv
        ss = jnp.concatenate([seg_sum(kk2[:, p * LANES:(p + 1) * LANES]) for p in range(N_PAIR)], axis=1)
        kk = kkv / jnp.maximum(jnp.sqrt(ss), 1e-12)
        k2 = k * (1.0 + (a - 1.0) * vec(3))
        l_hi = lw.astype(BF16)
        l_r1 = lw - l_hi.astype(F32)
        l_mid = l_r1.astype(BF16)
        l_lo = (l_r1 - l_mid.astype(F32)).astype(BF16)
        cum = jnp.dot(tri3, jnp.concatenate([l_hi, l_mid, l_lo], axis=0), preferred_element_type=F32)
        p_fwd = jnp.exp(cum)
        p_inv = jnp.exp(-cum)
        prep.append(dict(v=v, p_fwd=p_fwd, at=-kk * jnp.exp(cum - lw), bt=kk * a * p_inv, kt=k2 * p_inv,
                         rt=r * p_fwd, rk_prod=r * k2 * vec(4)))
    for bi in range(bb):
        row_sc[bi] = sh_ref[bi, c - 1:c, :]

    ops = []
    for bi, p in chains:
        sl = slice(p * LANES, (p + 1) * LANES)
        d = prep[bi]
        b_st, k_st, v_st = stack(d["bt"][:, sl]), stack(d["kt"][:, sl]), stack(d["v"][:, sl])
        ar = jnp.concatenate([stack(d["at"][:, sl]), stack(d["rt"][:, sl])], axis=0).astype(BF16)
        bk = jnp.concatenate([b_st, k_st], axis=0).astype(BF16)
        p_last = d["p_fwd"][c - 1:c, sl]
        ops.append(dict(ar=ar, bk=bk, v_b=v_st.astype(BF16), p_last=p_last,
                        b_dec=(b_st * p_last).astype(BF16), k_dec=(k_st * p_last).astype(BF16)))
    grams = [_mm(o["ar"], o["bk"], 1, 1) for o in ops]
    l_ab = [jnp.where(strict, g[:c2, :c2], 0.0) for g in grams]
    l_ak = [jnp.where(strict, g[:c2, c2:], 0.0).astype(BF16) for g in grams]
    m_rb = [jnp.where(incl, g[c2:, :c2], 0.0).astype(BF16) for g in grams]
    m_rk = [jnp.where(incl, g[c2:, c2:], 0.0).astype(BF16) for g in grams]
    inv = [eye2 + l for l in l_ab]
    lp = l_ab
    for _ in range(n_double):
        lpb = [l.astype(BF16) for l in lp]
        lp = [_mm(l, l, 1, 0) for l in lpb]
        inv = [x + _mm(x.astype(BF16), l.astype(BF16), 1, 0) for x, l in zip(inv, lp)]

    s_b = [s.astype(BF16) for s in states]
    z = [_mm(o["ar"][:c2], sb, 1, 1) + _mm(la, o["v_b"], 1, 0) for o, sb, la in zip(ops, s_b, l_ak)]
    u_b = [_mm(x.astype(BF16), zz.astype(BF16), 1, 0).astype(BF16) for x, zz in zip(inv, z)]
    y_st = [_mm(o["ar"][c2:], sb, 1, 1) + _mm(mb, u, 1, 0) + _mm(mk, o["v_b"], 1, 0)
            for o, sb, mb, mk, u in zip(ops, s_b, m_rb, m_rk, u_b)]
    new_states = [s * o["p_last"] + _mm(u, o["b_dec"], 0, 0) + _mm(o["v_b"], o["k_dec"], 0, 0)
                  for s, o, u in zip(states, ops, u_b)]
    for (bi, p), s_new, ys in zip(chains, new_states, y_st):
        sl = slice(p * LANES, (p + 1) * LANES)
        st_sc[bi, p] = s_new
        y = ys[:c] + ys[c:]
        dev = y - seg_sum(y) * (1.0 / n)
        var = seg_sum(dev * dev) * (1.0 / n)
        yn = dev * lax.rsqrt(var + GN_EPS) * vec(5, sl) + vec(6, sl)
        y_ref[bi, :, sl] = yn + seg_sum(prep[bi]["rk_prod"][:, sl]) * prep[bi]["v"][:, sl]

    @pl.when(ci == pl.num_programs(1) - 1)
    def _():
        sout_ref[...] = st_sc[...]


def _rwkv(sh, shift0, mu, w0, w2, a0, a2, k_k, k_a, r_k, ln_w, ln_b, state0, *, bb, c):
    b, t, _ = sh.shape
    n = HEAD_DIM_R
    eye = jnp.eye(2, dtype=F32)
    s0 = jnp.einsum('bphij,hg->bphigj', state0.reshape(b, N_PAIR, 2, n, n), eye).reshape(b, N_PAIR, LANES, LANES)
    vecs = jnp.stack([w0, a0, k_k, k_a, r_k.reshape(D_R), ln_w, ln_b, jnp.zeros_like(w0)])
    zero = jnp.zeros((D_LORA, D_R), F32)
    w_lora = jnp.concatenate([jnp.concatenate([w2, zero], axis=1), jnp.concatenate([zero, a2], axis=1)],
                             axis=0).astype(BF16)
    full = lambda shape: pl.BlockSpec(shape, lambda bi, ci: (0,) * len(shape))
    st_spec = pl.BlockSpec((bb, N_PAIR, LANES, LANES), lambda bi, ci: (bi, 0, 0, 0))
    y, s_out = pl.pallas_call(
        functools.partial(_rwkv_kernel, bb=bb, c=c),
        grid=(b // bb, t // c),
        in_specs=[pl.BlockSpec((bb, c, D_SHIFT), lambda bi, ci: (bi, ci, 0)),
                  pl.BlockSpec((bb, 1, D_SHIFT), lambda bi, ci: (bi, 0, 0)),
                  full((1, D_SHIFT)), full((N_VEC, D_R)), full((2 * D_LORA, 2 * D_R)), st_spec],
        out_specs=[pl.BlockSpec((bb, c, D_R), lambda bi, ci: (bi, ci, 0)), st_spec],
        out_shape=[jax.ShapeDtypeStruct((b, t, D_R), F32),
                   jax.ShapeDtypeStruct((b, N_PAIR, LANES, LANES), F32)],
        scratch_shapes=[pltpu.VMEM((bb, N_PAIR, LANES, LANES), F32), pltpu.VMEM((bb, 1, D_SHIFT), F32)],
        compiler_params=_cparams("parallel", "arbitrary"),
        name="rwkv",
    )(sh, shift0, mu.reshape(1, D_SHIFT), vecs, w_lora, s0)
    s_t = jnp.einsum('bphigj,hg->bphij', s_out.reshape(b, N_PAIR, 2, n, 2, n), eye)
    return y, s_t.reshape(b, N_HEADS_R, n, n)


def _mem_attn_kernel(q_ref, mk_ref, mv_ref, o_ref):
    scale = HEAD_DIM_M ** -0.5
    for h in range(N_HEADS_M):
        sl = slice(h * HEAD_DIM_M, (h + 1) * HEAD_DIM_M)
        q = q_ref[0, :, sl].astype(BF16)
        logits = lax.dot_general(q, mk_ref[0, :, sl], (((1,), (1,)), ((), ())), preferred_element_type=F32) * scale
        e = jnp.exp(logits - jnp.max(logits, axis=1, keepdims=True))
        p = e / jnp.sum(e, axis=1, keepdims=True)
        o_ref[0, :, sl] = jnp.dot(p.astype(BF16), mv_ref[0, :, sl], preferred_element_type=F32)


def _mem_attn(qm, mk, mv, tm):
    b, t, d = qm.shape
    nm = mk.shape[1]
    return pl.pallas_call(
        _mem_attn_kernel,
        grid=(b, t // tm),
        in_specs=[pl.BlockSpec((1, tm, d), lambda bi, i: (bi, i, 0)),
                  pl.BlockSpec((1, nm, d), lambda bi, i: (bi, 0, 0)),
                  pl.BlockSpec((1, nm, d), lambda bi, i: (bi, 0, 0))],
        out_specs=pl.BlockSpec((1, tm, d), lambda bi, i: (bi, i, 0)),
        out_shape=jax.ShapeDtypeStruct((b, t, d), F32),
        compiler_params=_cparams("parallel", "parallel"),
        name="mem_attn",
    )(qm, mk, mv)


def _merge_kernel(oa_ref, yr_ref, om_ref, zg_ref, x_ref, wa_ref, wr_ref, wm_ref, wo_ref, g_ref, xo_ref, ho_ref):
    d_br = oa_ref.shape[1]
    d = x_ref.shape[1]

    def branch(o, z, w_ref):
        return jnp.dot((o * jax.nn.silu(z)).astype(BF16), w_ref[...], preferred_element_type=F32)

    pa = branch(oa_ref[...], zg_ref[:, 0:d_br], wa_ref)
    pr = branch(yr_ref[...], zg_ref[:, d_br:2 * d_br], wr_ref)
    pm = branch(om_ref[...], zg_ref[:, 2 * d_br:3 * d_br], wm_ref)
    g0 = 3 * d_br
    merged = (jax.nn.sigmoid(zg_ref[:, g0:g0 + d]) * pa
              + jax.nn.sigmoid(zg_ref[:, g0 + d:g0 + 2 * d]) * pr
              + jax.nn.sigmoid(zg_ref[:, g0 + 2 * d:g0 + 3 * d]) * pm)
    xn = x_ref[...] + jnp.dot(merged.astype(BF16), wo_ref[...], preferred_element_type=F32)
    xo_ref[...] = xn
    y = xn * lax.rsqrt(jnp.mean(xn * xn, axis=-1, keepdims=True) + RMS_EPS)
    ho_ref[...] = y * g_ref[...]


def _merge(oa, yr, om, zg, x2d, wa, wr, wm, wo, g_next, tm):
    m, d = x2d.shape
    d_br = oa.shape[1]
    tm = min(tm, m)
    row = lambda w: pl.BlockSpec((tm, w), lambda i: (i, 0))
    full = lambda a: pl.BlockSpec(a.shape, lambda i: (0, 0))
    g2 = g_next.reshape(1, d)
    return pl.pallas_call(
        _merge_kernel,
        grid=(m // tm,),
        in_specs=[row(d_br), row(d_br), row(d_br), row(zg.shape[1]), row(d),
                  full(wa), full(wr), full(wm), full(wo), full(g2)],
        out_specs=[row(d), row(d)],
        out_shape=[jax.ShapeDtypeStruct((m, d), F32), jax.ShapeDtypeStruct((m, d), F32)],
        compiler_params=_cparams("parallel"),
        name="merge",
    )(oa, yr, om, zg, x2d, wa, wr, wm, wo, g2)


def _round_up(x, m):
    return (x + m - 1) // m * m


IN_SIZES = (D_A, D_KV, D_KV, N_IDX * D_IDX, D_IDX, N_IDX, D_A, D_SHIFT, D_R, N_HEADS_M * HEAD_DIM_M,
            N_HEADS_M * HEAD_DIM_M)
IN_OFFS = tuple(int(o) for o in np.cumsum((0,) + IN_SIZES))


def _repack_kernel(w_ref, att_ref, sh_ref, qm_ref, zg_ref):
    col = lambda i: w_ref[:, IN_OFFS[i]:IN_OFFS[i + 1]]
    rows = w_ref.shape[0]
    att_ref[:, :W_OFF + N_IDX] = w_ref[:, :IN_OFFS[6]].astype(BF16)
    att_ref[:, W_OFF + N_IDX:] = jnp.zeros((rows, ATT_COLS - W_OFF - N_IDX), BF16)
    sh_ref[...] = col(7).astype(BF16)
    qm_ref[...] = col(9).astype(BF16)
    zg_ref[:, :D_A] = col(6).astype(BF16)
    zg_ref[:, D_A:D_A + D_R] = col(8).astype(BF16)
    zg_ref[:, D_A + D_R:D_A + D_R + IN_SIZES[10]] = col(10).astype(BF16)
    zg_ref[:, D_A + D_R + IN_SIZES[10]:] = w_ref[:, IN_OFFS[11]:].astype(BF16)


def _split_w_in(w_in):
    d_model, d_in = w_in.shape
    n_zg = D_A + D_R + IN_SIZES[10] + d_in - IN_OFFS[11]
    tr = 128
    widths = (ATT_COLS, D_SHIFT, IN_SIZES[9], n_zg)
    return tuple(pl.pallas_call(
        _repack_kernel,
        grid=(d_model // tr,),
        in_specs=[pl.BlockSpec((tr, d_in), lambda i: (i, 0))],
        out_specs=[pl.BlockSpec((tr, w), lambda i: (i, 0)) for w in widths],
        out_shape=[jax.ShapeDtypeStruct((d_model, w), BF16) for w in widths],
        compiler_params=_cparams("parallel"),
        name="repack",
    )(w_in))


def _layer(x2d, h2d, b, t, pos0, past_k, past_v, past_ki, wkv0, shift0, mem_k, mem_v, lw, g_next, cfg):
    (w_att, w_sh, w_qm, w_zg, mu_shift, w0, w2, a0, a2, k_k, k_a, r_k, ln_w, ln_b, w_br_a, w_br_r, w_br_m,
     w_out) = lw
    tm = cfg["tm"]
    d_a = N_HEADS_A * HEAD_DIM_A
    d_kv = N_KV_A * HEAD_DIM_A
    d_r = N_HEADS_R * HEAD_DIM_R
    d_m = N_HEADS_M * HEAD_DIM_M

    tables = _rope_tables(pos0, t)
    pa, kb, vb, ki2 = _attn_proj(h2d, w_att, tables, t, min(tm, t))
    pa = pa.reshape(b, t, ATT_COLS)
    sh = _matmul(h2d, w_sh, tm, w_sh.shape[1]).reshape(b, t, -1)
    qm = _matmul(h2d, w_qm, tm, w_qm.shape[1]).reshape(b, t, d_m)
    zg = _matmul(h2d, w_zg, tm, cfg["tn_zg"])

    k_new = pa[..., K_OFF:K_OFF + D_KV].reshape(b, t, N_KV_A, HEAD_DIM_A)
    v_new = pa[..., V_OFF:V_OFF + D_KV].reshape(b, t, N_KV_A, HEAD_DIM_A)
    ki_new = pa[..., KI_OFF:KI_OFF + D_IDX]
    s_real = pos0 + t
    tq, tk = cfg["tq"], cfg["tk"]
    s_pad = _round_up(s_real, tk)

    def with_past(past, new):
        full = jnp.concatenate([past.reshape(b, pos0, LANES).astype(BF16), new.reshape(b, t, LANES)], axis=1)
        return jnp.pad(full, ((0, 0), (0, s_pad - s_real), (0, 0)))

    vt_all = with_past(past_v, vb).transpose(0, 2, 1)
    o_a = _dsa(pa, with_past(jnp.concatenate([past_ki, past_ki], axis=-1), ki2), with_past(past_k, kb),
               vt_all, pos0=pos0, s_real=s_real, tq=tq, tk=tk).reshape(b * t, d_a)

    y_r, wkv_t = _rwkv(sh, shift0, mu_shift, w0, w2, a0, a2, k_k, k_a, r_k, ln_w, ln_b, wkv0,
                       bb=cfg["bb"], c=CHUNK)
    y_r = y_r.reshape(b * t, d_r)

    o_m = _mem_attn(qm, mem_k, mem_v, min(tm, t)).reshape(b * t, d_m)

    x_new, h_next = _merge(o_a, y_r, o_m, zg, x2d, w_br_a, w_br_r, w_br_m, w_out, g_next, cfg["tm_merge"])
    return x_new, h_next, k_new, v_new, ki_new, wkv_t, sh[:, -1:]


PROMPT_CFG = dict(tm=512, tn_zg=2304, tq=128, tk=1024, bb=2, tm_merge=256)
SAMPLE_CFG = dict(tm=512, tn_zg=2304, tq=128, tk=1152, bb=2, tm_merge=256)


def kernel(x_prompt, x_sample, mem_prompt, cache_k, cache_v, cache_kidx, state_wkv, state_shift, cache_mem_k,
           cache_mem_v, norm_g, w_in, mu_shift, w0, w2, a0, a2, k_k, k_a, r_k, ln_x_w, ln_x_b, mem_norm_g,
           w_mem_kv, w_br_a, w_br_r, w_br_m, w_out, final_norm_g):
    depth = w_in.shape[0]
    bp, tp, d = x_prompt.shape
    bs, ts, _ = x_sample.shape
    n_mem = mem_prompt.shape[1]
    d_m = N_HEADS_M * HEAD_DIM_M
    past = cache_k.shape[2]

    xp = x_prompt.reshape(bp * tp, d)
    xs = x_sample.reshape(bs * ts, d)
    hp = _rmsnorm(xp, norm_g[0], F32, 512)
    hs = _rmsnorm(xs, norm_g[0], F32, 512)
    mem2d = mem_prompt.reshape(bp * n_mem, d)

    empty_kv = jnp.zeros((bp, 0, N_KV_A, HEAD_DIM_A), F32)
    empty_ki = jnp.zeros((bp, 0, D_IDX), F32)
    zero_state = jnp.zeros((bp, N_HEADS_R, HEAD_DIM_R, HEAD_DIM_R), F32)
    zero_shift = jnp.zeros((bp, 1, mu_shift.shape[1]), F32)

    outs_p = [[] for _ in range(7)]
    outs_s = [[] for _ in range(5)]
    for l in range(depth):
        g_next = norm_g[l + 1] if l + 1 < depth else final_norm_g
        lw = _split_w_in(w_in[l]) + (mu_shift[l], w0[l], w2[l], a0[l], a2[l], k_k[l], k_a[l], r_k[l], ln_x_w[l],
                                     ln_x_b[l], w_br_a[l].astype(BF16), w_br_r[l].astype(BF16),
                                     w_br_m[l].astype(BF16), w_out[l].astype(BF16))
        mem_h = _rmsnorm(mem2d, mem_norm_g[l], F32, 512)
        mkv = _matmul(mem_h, w_mem_kv[l].astype(BF16), 512, 2 * d_m).reshape(bp, n_mem, 2 * d_m)
        mk, mv = mkv[..., :d_m], mkv[..., d_m:]
        xp, hp, k_n, v_n, ki_n, wkv_n, sh_n = _layer(
            xp, hp, bp, tp, 0, empty_kv, empty_kv, empty_ki, zero_state, zero_shift,
            mk.astype(BF16), mv.astype(BF16), lw, g_next, PROMPT_CFG)
        for lst, val in zip(outs_p, (k_n, v_n, ki_n, wkv_n, sh_n,
                                     mk.reshape(bp, n_mem, N_HEADS_M, HEAD_DIM_M),
                                     mv.reshape(bp, n_mem, N_HEADS_M, HEAD_DIM_M))):
            lst.append(val)
        xs, hs, k_n, v_n, ki_n, wkv_n, sh_n = _layer(
            xs, hs, bs, ts, past, cache_k[l], cache_v[l], cache_kidx[l], state_wkv[l], state_shift[l],
            cache_mem_k[l].reshape(bs, n_mem, d_m).astype(BF16), cache_mem_v[l].reshape(bs, n_mem, d_m).astype(BF16),
            lw, g_next, SAMPLE_CFG)
        for lst, val in zip(outs_s, (k_n, v_n, ki_n, wkv_n, sh_n)):
            lst.append(val)

    y_prompt = hp.reshape(bp, tp, d)
    y_sample = hs.reshape(bs, ts, d)
    return (y_prompt, y_sample) + tuple(jnp.stack(o) for o in outs_p) + tuple(jnp.stack(o) for o in outs_s)
```

```python
import functools

import jax
import jax.numpy as jnp
import numpy as np
from jax import lax
from jax.experimental import pallas as pl
from jax.experimental.pallas import tpu as pltpu

F32 = jnp.float32
BF16 = jnp.bfloat16

CHUNK = 64
ROPE_THETA = 10000.0
RMS_EPS = 1e-6
NEG_INF = -1e30
N_HEADS_A = 8
HEAD_DIM_A = 64
N_KV_A = 2
N_IDX = 4
D_IDX = 64
TOPK_MAX = 256
N_HEADS_R = 8
HEAD_DIM_R = 64
D_LORA = 64
GN_EPS = 64e-5
N_HEADS_M = 4
HEAD_DIM_M = 128
N_BRANCH = 3

LANES = 128
VMEM_LIMIT = 48 * 1024 * 1024

_NEG_BITS = int(np.array(NEG_INF, np.float32).view(np.int32))
NEG_KEY = _NEG_BITS ^ 0x7FFFFFFF
SIGN_FLIP = -(2 ** 31)


def _cparams(*sem):
    return pltpu.CompilerParams(dimension_semantics=sem, vmem_limit_bytes=VMEM_LIMIT)


def _rmsnorm_kernel(x_ref, g_ref, o_ref):
    x = x_ref[...]
    y = x * lax.rsqrt(jnp.mean(x * x, axis=-1, keepdims=True) + RMS_EPS)
    o_ref[...] = (y * g_ref[...]).astype(o_ref.dtype)


def _rmsnorm(x2d, g, out_dtype, tm):
    m, d = x2d.shape
    tm = min(tm, m)
    return pl.pallas_call(
        _rmsnorm_kernel,
        grid=(m // tm,),
        in_specs=[pl.BlockSpec((tm, d), lambda i: (i, 0)), pl.BlockSpec((1, d), lambda i: (0, 0))],
        out_specs=pl.BlockSpec((tm, d), lambda i: (i, 0)),
        out_shape=jax.ShapeDtypeStruct((m, d), out_dtype),
        compiler_params=_cparams("parallel"),
        name="rmsnorm",
    )(x2d, g.reshape(1, d))


def _matmul_kernel(x_ref, w_ref, o_ref):
    o_ref[...] = jnp.dot(x_ref[...].astype(BF16), w_ref[...], preferred_element_type=F32)


def _matmul(x2d, w, tm, tn):
    m, k = x2d.shape
    n = w.shape[1]
    tm = min(tm, m)
    return pl.pallas_call(
        _matmul_kernel,
        grid=(n // tn, m // tm),
        in_specs=[pl.BlockSpec((tm, k), lambda j, i: (i, 0)), pl.BlockSpec((k, tn), lambda j, i: (0, j))],
        out_specs=pl.BlockSpec((tm, tn), lambda j, i: (i, j)),
        out_shape=jax.ShapeDtypeStruct((m, n), F32),
        compiler_params=_cparams("parallel", "parallel"),
        name="matmul",
    )(x2d, w)


D_A = N_HEADS_A * HEAD_DIM_A
D_KV = N_KV_A * HEAD_DIM_A
K_OFF = D_A
V_OFF = K_OFF + D_KV
QI_OFF = V_OFF + D_KV
KI_OFF = QI_OFF + N_IDX * D_IDX
W_OFF = KI_OFF + D_IDX
ATT_COLS = KI_OFF + LANES
_G_K = K_OFF // LANES
_G_V = V_OFF // LANES
_G_KI = KI_OFF // LANES


def _attn_proj_kernel(h_ref, w_ref, cos_ref, sa_ref, sb_ref, o_ref, kb_ref, vb_ref, ki2_ref):
    p = jnp.dot(h_ref[...].astype(BF16), w_ref[...], preferred_element_type=F32)
    cos, sa, sb = cos_ref[...], sa_ref[...], sb_ref[...]
    lane = lax.broadcasted_iota(jnp.int32, cos.shape, 1)
    for g in range(ATT_COLS // LANES):
        x = p[:, g * LANES:(g + 1) * LANES]
        if g == _G_V:
            o_ref[:, g * LANES:(g + 1) * LANES] = x
            vb_ref[...] = x.astype(BF16)
            continue
        y = x * cos + pltpu.roll(x, LANES - 32, 1) * sa + pltpu.roll(x, 32, 1) * sb
        if g == _G_KI:
            y = jnp.where(lane < D_IDX, y, x)
            ki2_ref[...] = jnp.where(lane < D_IDX, y, pltpu.roll(y, D_IDX, 1)).astype(BF16)
        if g == _G_K:
            kb_ref[...] = y.astype(BF16)
        o_ref[:, g * LANES:(g + 1) * LANES] = y


def _rope_tables(pos0, t):
    d = HEAD_DIM_A
    inv = ROPE_THETA ** (-jnp.arange(0, d, 2, dtype=F32) / d)
    ang = (pos0 + jnp.arange(t)).astype(F32)[:, None] * inv[None, :]
    cos, sin = jnp.cos(ang), jnp.sin(ang)
    zero = jnp.zeros_like(sin)
    cos_f = jnp.concatenate([cos] * 4, axis=-1)
    sin_a = jnp.concatenate([-sin, zero] * 2, axis=-1)
    sin_b = jnp.concatenate([zero, sin] * 2, axis=-1)
    return cos_f, sin_a, sin_b


def _attn_proj(h2d, w_att, tables, t, tm):
    m, k = h2d.shape
    nt = t // tm
    tab_spec = pl.BlockSpec((tm, LANES), lambda i: (i % nt, 0))
    lane_spec = pl.BlockSpec((tm, LANES), lambda i: (i, 0))
    lane_shape = jax.ShapeDtypeStruct((m, LANES), BF16)
    return pl.pallas_call(
        _attn_proj_kernel,
        grid=(m // tm,),
        in_specs=[pl.BlockSpec((tm, k), lambda i: (i, 0)), pl.BlockSpec((k, ATT_COLS), lambda i: (0, 0)),
                  tab_spec, tab_spec, tab_spec],
        out_specs=[pl.BlockSpec((tm, ATT_COLS), lambda i: (i, 0)), lane_spec, lane_spec, lane_spec],
        out_shape=[jax.ShapeDtypeStruct((m, ATT_COLS), F32), lane_shape, lane_shape, lane_shape],
        compiler_params=_cparams("parallel"),
        name="attn_proj",
    )(h2d, w_att, *tables)


I16 = jnp.int16
I16_MIN = -(2 ** 15)
I16_MAX = 2 ** 15 - 1
I16_ROWS = 16
I32_ROWS = 8
LOG2E = 1.4426950408889634


def _tree(parts, op):
    while len(parts) > 1:
        parts = [op(parts[i], parts[i + 1]) if i + 1 < len(parts) else parts[i] for i in range(0, len(parts), 2)]
    return parts[0]


def _fold_rows(x, rows, op):
    return _tree([x[g * rows:(g + 1) * rows] for g in range(x.shape[0] // rows)], op)


def _dsa_kernel(pa_ref, ki_ref, k_ref, vt_ref, o_ref, key_sc, hi_sc, lo_sc, *, tq, tk, pos0, t_real, s_real, topk):
    qb = pl.program_id(1)
    q0 = pos0 + qb * tq
    n_adm = jnp.minimum(q0 + tq, s_real)
    n_kt = (n_adm + tk - 1) // tk
    hd = HEAD_DIM_A
    group = N_HEADS_A // N_KV_A
    cols = group * tq

    q_pos = q0 + lax.broadcasted_iota(jnp.int32, (1, tq), 1)
    key_limit = jnp.minimum((lax.shift_right_arithmetic(q_pos, CHUNK.bit_length() - 1) + 1) * CHUNK, s_real)
    key_iota = lax.broadcasted_iota(jnp.int32, (tk, tq), 0)
    low_half = lax.broadcasted_iota(jnp.int32, (tq, LANES), 1) < hd
    halves = (low_half, jnp.logical_not(low_half))
    real_q = qb * tq + lax.broadcasted_iota(jnp.int32, (1, tq), 1) < t_real

    pa = pa_ref[0]
    qi = jnp.concatenate(
        [jnp.where(halves[h % 2], pa[:, QI_OFF + (h // 2) * LANES:QI_OFF + (h // 2 + 1) * LANES] * (D_IDX ** -0.5), 0.0)
         for h in range(N_IDX)], axis=0).astype(BF16)
    w_t = pa[:, KI_OFF:KI_OFF + LANES].T * (N_IDX ** -0.5)
    w_rows = [w_t[D_IDX + h:D_IDX + h + 1, :] for h in range(N_IDX)]

    def score_body(kt, carry):
        start = pl.multiple_of(kt * tk, tk)
        s = lax.dot_general(ki_ref[0, pl.ds(start, tk), :], qi, (((1,), (1,)), ((), ())),
                            preferred_element_type=F32)
        score = jnp.zeros((tk, tq), F32)
        for h in range(N_IDX):
            score = score + jnp.maximum(s[:, h * tq:(h + 1) * tq], 0.0) * w_rows[h]
        score = score + 0.0
        score = jnp.where(start + key_iota < key_limit, score, NEG_INF)
        bits = pltpu.bitcast(score, jnp.int32)
        key = jnp.where(bits < 0, bits ^ 0x7FFFFFFF, bits)
        key_sc[kt] = key
        hi_sc[kt] = lax.shift_right_arithmetic(key, 16).astype(I16)
        return carry

    lax.fori_loop(0, n_kt, score_body, 0)

    def count(pred_fn):
        def body(kt, acc):
            hit = jnp.where(pred_fn(key_sc[kt], kt), 1, 0)
            return acc + _fold_rows(hit, I32_ROWS, jnp.add)

        acc = lax.fori_loop(0, n_kt, body, jnp.zeros((I32_ROWS, tq), jnp.int32))
        return jnp.sum(acc, axis=0, keepdims=True)

    def count16(ref, c_of_tile, cmp):
        def body(kt, acc):
            hit = jnp.where(cmp(ref[kt], c_of_tile(kt).astype(I16)), jnp.ones((), I16), jnp.zeros((), I16))
            return acc + _fold_rows(pltpu.bitcast(hit, jnp.int32), I32_ROWS, jnp.add)

        acc = lax.fori_loop(0, n_kt, body, jnp.zeros((I32_ROWS, tq), jnp.int32))
        both = (acc & 0xFFFF) + lax.shift_right_logical(acc, 16)
        return jnp.sum(both, axis=0, keepdims=True)

    def kth_digit(ref, kk):
        def body(i, prefix):
            cand = prefix | lax.shift_left(jnp.int32(1), 15 - i)
            cnt = count16(ref, lambda kt: cand + I16_MIN, lax.ge)
            return jnp.where(cnt >= kk, cand, prefix)

        return lax.fori_loop(0, 16, body, jnp.zeros((1, tq), jnp.int32))

    hi_thr = kth_digit(hi_sc, topk) + I16_MIN
    above = jnp.where(hi_thr < I16_MAX, count16(hi_sc, lambda kt: jnp.minimum(hi_thr + 1, I16_MAX), lax.ge), 0)

    def lo_body(kt, carry):
        key = key_sc[kt]
        in_bucket = lax.shift_right_arithmetic(key, 16) == hi_thr
        lo_sc[kt] = jnp.where(in_bucket, (key & 0xFFFF) + I16_MIN, I16_MIN).astype(I16)
        return carry

    lax.fori_loop(0, n_kt, lo_body, 0)
    lo_thr = kth_digit(lo_sc, topk - above)
    thr = jnp.maximum(lax.shift_left(hi_thr, 16) | lo_thr, NEG_KEY)
    cnt_gt = count(lambda key, kt: key > thr)
    cnt_ge = count(lambda key, kt: key >= thr)
    need = topk - cnt_gt
    real_thr = thr != NEG_KEY
    has_excess = jnp.max(jnp.where(real_thr & real_q & (cnt_ge > topk), 1, 0)) > 0

    n_idx_bits = max(1, int(np.ceil(np.log2(key_sc.shape[0] * tk))))

    def tie_search():
        def mark_body(kt, carry):
            lo_sc[kt] = jnp.where(key_sc[kt] == thr, key_iota, I16_MAX).astype(I16)
            return carry

        lax.fori_loop(0, n_kt, mark_body, 0)

        def jbody(i, jq):
            cand = jq | lax.shift_left(jnp.int32(1), n_idx_bits - 1 - i)
            cnt = count16(lo_sc, lambda kt: jnp.clip(cand - kt * tk, I16_MIN, I16_MAX), lax.lt)
            return jnp.where(cnt < need, cand, jq)

        return lax.fori_loop(0, n_idx_bits, jbody, jnp.zeros((1, tq), jnp.int32))

    j_cut = lax.cond(has_excess, tie_search, lambda: jnp.full((1, tq), 2 ** 30, jnp.int32))
    j_cut = jnp.where(real_thr, j_cut, -1)

    qs = []
    for j in range(N_KV_A):
        parts = []
        for g in range(group):
            h = j * group + g
            grp = pa[:, (h // 2) * LANES:(h // 2 + 1) * LANES] * (hd ** -0.5)
            if h % 2 != j:
                grp = pltpu.roll(grp, hd, 1)
            parts.append(jnp.where(halves[j], grp, 0.0))
        qs.append(jnp.concatenate(parts, axis=0).astype(BF16))

    def attn_body(kt, carry):
        start = pl.multiple_of(kt * tk, tk)
        key = key_sc[kt]
        sel = (key > thr) | ((key == thr) & (start + key_iota <= j_cut))
        bias = jnp.where(sel, 0.0, NEG_INF).astype(F32)
        bias = jnp.concatenate([bias] * group, axis=1)
        k_tile = k_ref[0, pl.ds(start, tk), :]
        heads = range(N_KV_A)
        logits = [lax.dot_general(k_tile, qs[j], (((1,), (1,)), ((), ())), preferred_element_type=F32) * LOG2E
                  + bias for j in heads]
        m_new = [jnp.maximum(carry[j][0], jnp.max(_fold_rows(logits[j], I32_ROWS, jnp.maximum), axis=0,
                                                   keepdims=True)) for j in heads]
        alpha = [jnp.exp2(carry[j][0] - m_new[j]) for j in heads]
        p = [jnp.exp2(logits[j] - m_new[j]) for j in heads]
        l_new = [carry[j][1] * alpha[j] + jnp.sum(_fold_rows(p[j], I32_ROWS, jnp.add), axis=0, keepdims=True)
                 for j in heads]
        acc = [carry[j][2] * alpha[j]
               + jnp.dot(vt_ref[0, j * hd:(j + 1) * hd, pl.ds(start, tk)], p[j].astype(BF16),
                         preferred_element_type=F32) for j in heads]
        return tuple((m_new[j], l_new[j], acc[j]) for j in heads)

    init = tuple((jnp.full((1, cols), 0.1 * NEG_INF, F32), jnp.zeros((1, cols), F32),
                  jnp.zeros((hd, cols), F32)) for _ in range(N_KV_A))
    res = lax.fori_loop(0, n_kt, attn_body, init)
    outs = [acc / l_i for _, l_i, acc in res]
    for grp_i in range(N_HEADS_A // 2):
        pair = []
        for h in (2 * grp_i, 2 * grp_i + 1):
            j, g = h // group, h % group
            pair.append(outs[j][:, g * tq:(g + 1) * tq])
        o_ref[0, :, grp_i * LANES:(grp_i + 1) * LANES] = jnp.concatenate(pair, axis=0).T


def _dsa(pa, ki_all, k_all, vt_all, *, pos0, s_real, tq, tk):
    b, t, _ = pa.shape
    assert tk < I16_MAX
    t_pad = _round_up(t, tq)
    pa = jnp.pad(pa, ((0, 0), (0, t_pad - t), (0, 0)))
    s_pad = ki_all.shape[1]
    topk = min(TOPK_MAX, s_real // 4)
    kern = functools.partial(_dsa_kernel, tq=tq, tk=tk, pos0=pos0, t_real=t, s_real=s_real, topk=topk)
    kv_spec = pl.BlockSpec((1, s_pad, LANES), lambda bi, qi: (bi, 0, 0))
    out = pl.pallas_call(
        kern,
        grid=(b, t_pad // tq),
        in_specs=[pl.BlockSpec((1, tq, ATT_COLS), lambda bi, qi: (bi, qi, 0)), kv_spec, kv_spec,
                  pl.BlockSpec((1, LANES, s_pad), lambda bi, qi: (bi, 0, 0))],
        out_specs=pl.BlockSpec((1, tq, D_A), lambda bi, qi: (bi, qi, 0)),
        out_shape=jax.ShapeDtypeStruct((b, t_pad, D_A), F32),
        scratch_shapes=[pltpu.VMEM((s_pad // tk, tk, tq), jnp.int32), pltpu.VMEM((s_pad // tk, tk, tq), I16),
                        pltpu.VMEM((s_pad // tk, tk, tq), I16)],
        compiler_params=_cparams("parallel", "arbitrary"),
        name="dsa",
    )(pa, ki_all, k_all, vt_all)
    return out[:, :t]


D_R = N_HEADS_R * HEAD_DIM_R
D_SHIFT = 3 * D_R + 2 * D_LORA
N_PAIR = D_R // LANES
N_VEC = 8


def _split2(x):
    hi = x.astype(BF16)
    lo = (x - hi.astype(F32)).astype(BF16)
    return hi, lo


def _mm(a, b, ca, cb):
    return lax.dot_general(a, b, (((ca,), (cb,)), ((), ())), preferred_element_type=F32)


def _rwkv_kernel(sh_ref, sh0_ref, mu_ref, vec_ref, wl_ref, s0_ref, y_ref, sout_ref, st_sc, row_sc, *, bb, c):
    ci = pl.program_id(1)
    n = HEAD_DIM_R
    c2 = 2 * c

    @pl.when(ci == 0)
    def _():
        st_sc[...] = s0_ref[...]
        row_sc[...] = sh0_ref[...]

    head0 = lax.broadcasted_iota(jnp.int32, (c, LANES), 1) < n
    r2 = lax.broadcasted_iota(jnp.int32, (c2, c2), 0)
    q2 = lax.broadcasted_iota(jnp.int32, (c2, c2), 1)
    same_head = (r2 // c) == (q2 // c)
    strict = same_head & (r2 > q2)
    incl = same_head & (r2 >= q2)
    eye2 = jnp.where(r2 == q2, 1.0, 0.0).astype(F32)
    rc = lax.broadcasted_iota(jnp.int32, (c, c), 0)
    cc = lax.broadcasted_iota(jnp.int32, (c, c), 1)
    tri = jnp.where(rc >= cc, 1.0, 0.0).astype(BF16)
    tri3 = jnp.concatenate([tri, tri, tri], axis=1)
    la = lax.broadcasted_iota(jnp.int32, (LANES, LANES), 0)
    lb = lax.broadcasted_iota(jnp.int32, (LANES, LANES), 1)
    seg = jnp.where((la // n) == (lb // n), 1.0, 0.0).astype(BF16)
    seg2 = jnp.concatenate([seg, seg], axis=0)
    first_row = lax.broadcasted_iota(jnp.int32, (c, D_SHIFT), 0) == 0

    def seg_sum(x):
        xh, xl = _split2(x)
        return jnp.dot(jnp.concatenate([xh, xl], axis=1), seg2, preferred_element_type=F32)

    def vec(i, sl=slice(None)):
        return vec_ref[i:i + 1, sl]

    n_double = 0
    while (2 << n_double) < c:
        n_double += 1

    def stack(zz):
        return jnp.concatenate([jnp.where(head0, zz, 0.0), jnp.where(head0, 0.0, zz)], axis=0)

    chains = [(bi, p) for bi in range(bb) for p in range(N_PAIR)]
    states = [st_sc[bi, p] for bi, p in chains]

    prep = []
    for bi in range(bb):
        sh = sh_ref[bi]
        prev = jnp.where(first_row, row_sc[bi], pltpu.roll(sh, 1, 0))
        x = sh + (prev - sh) * mu_ref[...]
        r = x[:, :D_R]
        k = x[:, D_R:2 * D_R]
        v = x[:, 2 * D_R:3 * D_R]
        xl = x[:, 3 * D_R:]
        lora_in = jnp.where(head0, jnp.tanh(xl), xl)
        wa = jnp.dot(lora_in.astype(BF16), wl_ref[...], preferred_element_type=F32)
        w_raw = vec(0) + wa[:, :D_R]
        a = jax.nn.sigmoid(vec(1) + wa[:, D_R:])
        lw = -jnp.exp(-jax.nn.softplus(-w_raw) - 0.5)
        kkv = k * vec(2)
        kk2 = kkv * kkv
        ss = jnp.concatenate([seg_sum(kk2[:, p * LANES:(p + 1) * LANES]) for p in range(N_PAIR)], axis=1)
        kk = kkv / jnp.maximum(jnp.sqrt(ss), 1e-12)
        k2 = k * (1.0 + (a - 1.0) * vec(3))
        l_hi = lw.astype(BF16)
        l_r1 = lw - l_hi.astype(F32)
        l_mid = l_r1.astype(BF16)
        l_lo = (l_r1 - l_mid.astype(F32)).astype(BF16)
        cum = jnp.dot(tri3, jnp.concatenate([l_hi, l_mid, l_lo], axis=0), preferred_element_type=F32)
        p_fwd = jnp.exp(cum)
        p_inv = jnp.exp(-cum)
        prep.append(dict(v=v, p_fwd=p_fwd, at=-kk * jnp.exp(cum - lw), bt=kk * a * p_inv, kt=k2 * p_inv,
                         rt=r * p_fwd, rk_prod=r * k2 * vec(4)))
    for bi in range(bb):
        row_sc[bi] = sh_ref[bi, c - 1:c, :]

    ops = []
    for bi, p in chains:
        sl = slice(p * LANES, (p + 1) * LANES)
        d = prep[bi]
        b_st, k_st, v_st = stack(d["bt"][:, sl]), stack(d["kt"][:, sl]), stack(d["v"][:, sl])
        ar = jnp.concatenate([stack(d["at"][:, sl]), stack(d["rt"][:, sl])], axis=0).astype(BF16)
        bk = jnp.concatenate([b_st, k_st], axis=0).astype(BF16)
        p_last = d["p_fwd"][c - 1:c, sl]
        ops.append(dict(ar=ar, bk=bk, v_b=v_st.astype(BF16), p_last=p_last,
                        b_dec=(b_st * p_last).astype(BF16), k_dec=(k_st * p_last).astype(BF16)))
    grams = [_mm(o["ar"], o["bk"], 1, 1) for o in ops]
    l_ab = [jnp.where(strict, g[:c2, :c2], 0.0) for g in grams]
    l_ak = [jnp.where(strict, g[:c2, c2:], 0.0).astype(BF16) for g in grams]
    m_rb = [jnp.where(incl, g[c2:, :c2], 0.0).astype(BF16) for g in grams]
    m_rk = [jnp.where(incl, g[c2:, c2:], 0.0).astype(BF16) for g in grams]
    inv = [eye2 + l for l in l_ab]
    lp = l_ab
    for _ in range(n_double):
        lpb = [l.astype(BF16) for l in lp]
        lp = [_mm(l, l, 1, 0) for l in lpb]
        inv = [x + _mm(x.astype(BF16), l.astype(BF16), 1, 0) for x, l in zip(inv, lp)]

    s_b = [s.astype(BF16) for s in states]
    z = [_mm(o["ar"][:c2], sb, 1, 1) + _mm(la, o["v_b"], 1, 0) for o, sb, la in zip(ops, s_b, l_ak)]
    u_b = [_mm(x.astype(BF16), zz.astype(BF16), 1, 0).astype(BF16) for x, zz in zip(inv, z)]
    y_st = [_mm(o["ar"][c2:], sb, 1, 1) + _mm(mb, u, 1, 0) + _mm(mk, o["v_b"], 1, 0)
            for o, sb, mb, mk, u in zip(ops, s_b, m_rb, m_rk, u_b)]
    new_states = [s * o["p_last"] + _mm(u, o["b_dec"], 0, 0) + _mm(o["v_b"], o["k_dec"], 0, 0)
                  for s, o, u in zip(states, ops, u_b)]
    for (bi, p), s_new, ys in zip(chains, new_states, y_st):
        sl = slice(p * LANES, (p + 1) * LANES)
        st_sc[bi, p] = s_new
        y = ys[:c] + ys[c:]
        dev = y - seg_sum(y) * (1.0 / n)
        var = seg_sum(dev * dev) * (1.0 / n)
        yn = dev * lax.rsqrt(var + GN_EPS) * vec(5, sl) + vec(6, sl)
        y_ref[bi, :, sl] = yn + seg_sum(prep[bi]["rk_prod"][:, sl]) * prep[bi]["v"][:, sl]

    @pl.when(ci == pl.num_programs(1) - 1)
    def _():
        sout_ref[...] = st_sc[...]


def _rwkv(sh, shift0, mu, w0, w2, a0, a2, k_k, k_a, r_k, ln_w, ln_b, state0, *, bb, c):
    b, t, _ = sh.shape
    n = HEAD_DIM_R
    eye = jnp.eye(2, dtype=F32)
    s0 = jnp.einsum('bphij,hg->bphigj', state0.reshape(b, N_PAIR, 2, n, n), eye).reshape(b, N_PAIR, LANES, LANES)
    vecs = jnp.stack([w0, a0, k_k, k_a, r_k.reshape(D_R), ln_w, ln_b, jnp.zeros_like(w0)])
    zero = jnp.zeros((D_LORA, D_R), F32)
    w_lora = jnp.concatenate([jnp.concatenate([w2, zero], axis=1), jnp.concatenate([zero, a2], axis=1)],
                             axis=0).astype(BF16)
    full = lambda shape: pl.BlockSpec(shape, lambda bi, ci: (0,) * len(shape))
    st_spec = pl.BlockSpec((bb, N_PAIR, LANES, LANES), lambda bi, ci: (bi, 0, 0, 0))
    y, s_out = pl.pallas_call(
        functools.partial(_rwkv_kernel, bb=bb, c=c),
        grid=(b // bb, t // c),
        in_specs=[pl.BlockSpec((bb, c, D_SHIFT), lambda bi, ci: (bi, ci, 0)),
                  pl.BlockSpec((bb, 1, D_SHIFT), lambda bi, ci: (bi, 0, 0)),
                  full((1, D_SHIFT)), full((N_VEC, D_R)), full((2 * D_LORA, 2 * D_R)), st_spec],
        out_specs=[pl.BlockSpec((bb, c, D_R), lambda bi, ci: (bi, ci, 0)), st_spec],
        out_shape=[jax.ShapeDtypeStruct((b, t, D_R), F32),
                   jax.ShapeDtypeStruct((b, N_PAIR, LANES, LANES), F32)],
        scratch_shapes=[pltpu.VMEM((bb, N_PAIR, LANES, LANES), F32), pltpu.VMEM((bb, 1, D_SHIFT), F32)],
        compiler_params=_cparams("parallel", "arbitrary"),
        name="rwkv",
    )(sh, shift0, mu.reshape(1, D_SHIFT), vecs, w_lora, s0)
    s_t = jnp.einsum('bphigj,hg->bphij', s_out.reshape(b, N_PAIR, 2, n, 2, n), eye)
    return y, s_t.reshape(b, N_HEADS_R, n, n)


def _mem_attn_kernel(q_ref, mk_ref, mv_ref, o_ref):
    scale = HEAD_DIM_M ** -0.5
    for h in range(N_HEADS_M):
        sl = slice(h * HEAD_DIM_M, (h + 1) * HEAD_DIM_M)
        q = q_ref[0, :, sl].astype(BF16)
        logits = lax.dot_general(q, mk_ref[0, :, sl], (((1,), (1,)), ((), ())), preferred_element_type=F32) * scale
        e = jnp.exp(logits - jnp.max(logits, axis=1, keepdims=True))
        p = e / jnp.sum(e, axis=1, keepdims=True)
        o_ref[0, :, sl] = jnp.dot(p.astype(BF16), mv_ref[0, :, sl], preferred_element_type=F32)


def _mem_attn(qm, mk, mv, tm):
    b, t, d = qm.shape
    nm = mk.shape[1]
    return pl.pallas_call(
        _mem_attn_kernel,
        grid=(b, t // tm),
        in_specs=[pl.BlockSpec((1, tm, d), lambda bi, i: (bi, i, 0)),
                  pl.BlockSpec((1, nm, d), lambda bi, i: (bi, 0, 0)),
                  pl.BlockSpec((1, nm, d), lambda bi, i: (bi, 0, 0))],
        out_specs=pl.BlockSpec((1, tm, d), lambda bi, i: (bi, i, 0)),
        out_shape=jax.ShapeDtypeStruct((b, t, d), F32),
        compiler_params=_cparams("parallel", "parallel"),
        name="mem_attn",
    )(qm, mk, mv)


def _merge_kernel(oa_ref, yr_ref, om_ref, zg_ref, x_ref, wa_ref, wr_ref, wm_ref, wo_ref, g_ref, xo_ref, ho_ref):
    d_br = oa_ref.shape[1]
    d = x_ref.shape[1]

    def branch(o, z, w_ref):
        return jnp.dot((o * jax.nn.silu(z)).astype(BF16), w_ref[...], preferred_element_type=F32)

    pa = branch(oa_ref[...], zg_ref[:, 0:d_br], wa_ref)
    pr = branch(yr_ref[...], zg_ref[:, d_br:2 * d_br], wr_ref)
    pm = branch(om_ref[...], zg_ref[:, 2 * d_br:3 * d_br], wm_ref)
    g0 = 3 * d_br
    merged = (jax.nn.sigmoid(zg_ref[:, g0:g0 + d]) * pa
              + jax.nn.sigmoid(zg_ref[:, g0 + d:g0 + 2 * d]) * pr
              + jax.nn.sigmoid(zg_ref[:, g0 + 2 * d:g0 + 3 * d]) * pm)
    xn = x_ref[...] + jnp.dot(merged.astype(BF16), wo_ref[...], preferred_element_type=F32)
    xo_ref[...] = xn
    y = xn * lax.rsqrt(jnp.mean(xn * xn, axis=-1, keepdims=True) + RMS_EPS)
    ho_ref[...] = y * g_ref[...]


def _merge(oa, yr, om, zg, x2d, wa, wr, wm, wo, g_next, tm):
    m, d = x2d.shape
    d_br = oa.shape[1]
    tm = min(tm, m)
    row = lambda w: pl.BlockSpec((tm, w), lambda i: (i, 0))
    full = lambda a: pl.BlockSpec(a.shape, lambda i: (0, 0))
    g2 = g_next.reshape(1, d)
    return pl.pallas_call(
        _merge_kernel,
        grid=(m // tm,),
        in_specs=[row(d_br), row(d_br), row(d_br), row(zg.shape[1]), row(d),
                  full(wa), full(wr), full(wm), full(wo), full(g2)],
        out_specs=[row(d), row(d)],
        out_shape=[jax.ShapeDtypeStruct((m, d), F32), jax.ShapeDtypeStruct((m, d), F32)],
        compiler_params=_cparams("parallel"),
        name="merge",
    )(oa, yr, om, zg, x2d, wa, wr, wm, wo, g2)


def _round_up(x, m):
    return (x + m - 1) // m * m


IN_SIZES = (D_A, D_KV, D_KV, N_IDX * D_IDX, D_IDX, N_IDX, D_A, D_SHIFT, D_R, N_HEADS_M * HEAD_DIM_M,
            N_HEADS_M * HEAD_DIM_M)
IN_OFFS = tuple(int(o) for o in np.cumsum((0,) + IN_SIZES))


def _repack_kernel(w_ref, att_ref, sh_ref, qm_ref, zg_ref):
    col = lambda i: w_ref[:, IN_OFFS[i]:IN_OFFS[i + 1]]
    rows = w_ref.shape[0]
    att_ref[:, :W_OFF + N_IDX] = w_ref[:, :IN_OFFS[6]].astype(BF16)
    att_ref[:, W_OFF + N_IDX:] = jnp.zeros((rows, ATT_COLS - W_OFF - N_IDX), BF16)
    sh_ref[...] = col(7).astype(BF16)
    qm_ref[...] = col(9).astype(BF16)
    zg_ref[:, :D_A] = col(6).astype(BF16)
    zg_ref[:, D_A:D_A + D_R] = col(8).astype(BF16)
    zg_ref[:, D_A + D_R:D_A + D_R + IN_SIZES[10]] = col(10).astype(BF16)
    zg_ref[:, D_A + D_R + IN_SIZES[10]:] = w_ref[:, IN_OFFS[11]:].astype(BF16)


def _split_w_in(w_in):
    d_model, d_in = w_in.shape
    n_zg = D_A + D_R + IN_SIZES[10] + d_in - IN_OFFS[11]
    tr = 128
    widths = (ATT_COLS, D_SHIFT, IN_SIZES[9], n_zg)
    return tuple(pl.pallas_call(
        _repack_kernel,
        grid=(d_model // tr,),
        in_specs=[pl.BlockSpec((tr, d_in), lambda i: (i, 0))],
        out_specs=[pl.BlockSpec((tr, w), lambda i: (i, 0)) for w in widths],
        out_shape=[jax.ShapeDtypeStruct((d_model, w), BF16) for w in widths],
        compiler_params=_cparams("parallel"),
        name="repack",
    )(w_in))


def _layer(x2d, h2d, b, t, pos0, past_k, past_v, past_ki, wkv0, shift0, mem_k, mem_v, lw, g_next, cfg):
    (w_att, w_sh, w_qm, w_zg, mu_shift, w0, w2, a0, a2, k_k, k_a, r_k, ln_w, ln_b, w_br_a, w_br_r, w_br_m,
     w_out) = lw
    tm = cfg["tm"]
    d_a = N_HEADS_A * HEAD_DIM_A
    d_kv = N_KV_A * HEAD_DIM_A
    d_r = N_HEADS_R * HEAD_DIM_R
    d_m = N_HEADS_M * HEAD_DIM_M

    tables = _rope_tables(pos0, t)
    pa, kb, vb, ki2 = _attn_proj(h2d, w_att, tables, t, min(tm, t))
    pa = pa.reshape(b, t, ATT_COLS)
    sh = _matmul(h2d, w_sh, tm, w_sh.shape[1]).reshape(b, t, -1)
    qm = _matmul(h2d, w_qm, tm, w_qm.shape[1]).reshape(b, t, d_m)
    zg = _matmul(h2d, w_zg, tm, cfg["tn_zg"])

    k_new = pa[..., K_OFF:K_OFF + D_KV].reshape(b, t, N_KV_A, HEAD_DIM_A)
    v_new = pa[..., V_OFF:V_OFF + D_KV].reshape(b, t, N_KV_A, HEAD_DIM_A)
    ki_new = pa[..., KI_OFF:KI_OFF + D_IDX]
    s_real = pos0 + t
    tq, tk = cfg["tq"], cfg["tk"]
    s_pad = _round_up(s_real, tk)

    def with_past(past, new):
        full = jnp.concatenate([past.reshape(b, pos0, LANES).astype(BF16), new.reshape(b, t, LANES)], axis=1)
        return jnp.pad(full, ((0, 0), (0, s_pad - s_real), (0, 0)))

    vt_all = with_past(past_v, vb).transpose(0, 2, 1)
    o_a = _dsa(pa, with_past(jnp.concatenate([past_ki, past_ki], axis=-1), ki2), with_past(past_k, kb),
               vt_all, pos0=pos0, s_real=s_real, tq=tq, tk=tk).reshape(b * t, d_a)

    y_r, wkv_t = _rwkv(sh, shift0, mu_shift, w0, w2, a0, a2, k_k, k_a, r_k, ln_w, ln_b, wkv0,
                       bb=cfg["bb"], c=CHUNK)
    y_r = y_r.reshape(b * t, d_r)

    o_m = _mem_attn(qm, mem_k, mem_v, min(tm, t)).reshape(b * t, d_m)

    x_new, h_next = _merge(o_a, y_r, o_m, zg, x2d, w_br_a, w_br_r, w_br_m, w_out, g_next, cfg["tm_merge"])
    return x_new, h_next, k_new, v_new, ki_new, wkv_t, sh[:, -1:]


PROMPT_CFG = dict(tm=512, tn_zg=2304, tq=128, tk=1024, bb=2, tm_merge=256)
SAMPLE_CFG = dict(tm=512, tn_zg=2304, tq=128, tk=1152, bb=2, tm_merge=256)


def kernel(x_prompt, x_sample, mem_prompt, cache_k, cache_v, cache_kidx, state_wkv, state_shift, cache_mem_k,
           cache_mem_v, norm_g, w_in, mu_shift, w0, w2, a0, a2, k_k, k_a, r_k, ln_x_w, ln_x_b, mem_norm_g,
           w_mem_kv, w_br_a, w_br_r, w_br_m, w_out, final_norm_g):
    depth = w_in.shape[0]
    bp, tp, d = x_prompt.shape
    bs, ts, _ = x_sample.shape
    n_mem = mem_prompt.shape[1]
    d_m = N_HEADS_M * HEAD_DIM_M
    past = cache_k.shape[2]

    xp = x_prompt.reshape(bp * tp, d)
    xs = x_sample.reshape(bs * ts, d)
    hp = _rmsnorm(xp, norm_g[0], F32, 512)
    hs = _rmsnorm(xs, norm_g[0], F32, 512)
    mem2d = mem_prompt.reshape(bp * n_mem, d)

    empty_kv = jnp.zeros((bp, 0, N_KV_A, HEAD_DIM_A), F32)
    empty_ki = jnp.zeros((bp, 0, D_IDX), F32)
    zero_state = jnp.zeros((bp, N_HEADS_R, HEAD_DIM_R, HEAD_DIM_R), F32)
    zero_shift = jnp.zeros((bp, 1, mu_shift.shape[1]), F32)

    outs_p = [[] for _ in range(7)]
    outs_s = [[] for _ in range(5)]
    for l in range(depth):
        g_next = norm_g[l + 1] if l + 1 < depth else final_norm_g
        lw = _split_w_in(w_in[l]) + (mu_shift[l], w0[l], w2[l], a0[l], a2[l], k_k[l], k_a[l], r_k[l], ln_x_w[l],
                                     ln_x_b[l], w_br_a[l].astype(BF16), w_br_r[l].astype(BF16),
                                     w_br_m[l].astype(BF16), w_out[l].astype(BF16))
        mem_h = _rmsnorm(mem2d, mem_norm_g[l], F32, 512)
        mkv = _matmul(mem_h, w_mem_kv[l].astype(BF16), 512, 2 * d_m).reshape(bp, n_mem, 2 * d_m)
        mk, mv = mkv[..., :d_m], mkv[..., d_m:]
        xp, hp, k_n, v_n, ki_n, wkv_n, sh_n = _layer(
            xp, hp, bp, tp, 0, empty_kv, empty_kv, empty_ki, zero_state, zero_shift,
            mk.astype(BF16), mv.astype(BF16), lw, g_next, PROMPT_CFG)
        for lst, val in zip(outs_p, (k_n, v_n, ki_n, wkv_n, sh_n,
                                     mk.reshape(bp, n_mem, N_HEADS_M, HEAD_DIM_M),
                                     mv.reshape(bp, n_mem, N_HEADS_M, HEAD_DIM_M))):
            lst.append(val)
        xs, hs, k_n, v_n, ki_n, wkv_n, sh_n = _layer(
            xs, hs, bs, ts, past, cache_k[l], cache_v[l], cache_kidx[l], state_wkv[l], state_shift[l],
            cache_mem_k[l].reshape(bs, n_mem, d_m).astype(BF16), cache_mem_v[l].reshape(bs, n_mem, d_m).astype(BF16),
            lw, g_next, SAMPLE_CFG)
        for lst, val in zip(outs_s, (k_n, v_n, ki_n, wkv_n, sh_n)):
            lst.append(val)

    y_prompt = hp.reshape(bp, tp, d)
    y_sample = hs.reshape(bs, ts, d)
    return (y_prompt, y_sample) + tuple(jnp.stack(o) for o in outs_p) + tuple(jnp.stack(o) for o in outs_s)
```

```python
import functools

import jax
import jax.numpy as jnp
import numpy as np
from jax import lax
from jax.experimental import pallas as pl
from jax.experimental.pallas import tpu as pltpu

F32 = jnp.float32
BF16 = jnp.bfloat16

CHUNK = 64
ROPE_THETA = 10000.0
RMS_EPS = 1e-6
NEG_INF = -1e30
N_HEADS_A = 8
HEAD_DIM_A = 64
N_KV_A = 2
N_IDX = 4
D_IDX = 64
TOPK_MAX = 256
N_HEADS_R = 8
HEAD_DIM_R = 64
D_LORA = 64
GN_EPS = 64e-5
N_HEADS_M = 4
HEAD_DIM_M = 128
N_BRANCH = 3

LANES = 128
VMEM_LIMIT = 48 * 1024 * 1024

_NEG_BITS = int(np.array(NEG_INF, np.float32).view(np.int32))
NEG_KEY = _NEG_BITS ^ 0x7FFFFFFF
SIGN_FLIP = -(2 ** 31)


def _cparams(*sem):
    return pltpu.CompilerParams(dimension_semantics=sem, vmem_limit_bytes=VMEM_LIMIT)


def _rmsnorm_kernel(x_ref, g_ref, o_ref):
    x = x_ref[...]
    y = x * lax.rsqrt(jnp.mean(x * x, axis=-1, keepdims=True) + RMS_EPS)
    o_ref[...] = (y * g_ref[...]).astype(o_ref.dtype)


def _rmsnorm(x2d, g, out_dtype, tm):
    m, d = x2d.shape
    tm = min(tm, m)
    return pl.pallas_call(
        _rmsnorm_kernel,
        grid=(m // tm,),
        in_specs=[pl.BlockSpec((tm, d), lambda i: (i, 0)), pl.BlockSpec((1, d), lambda i: (0, 0))],
        out_specs=pl.BlockSpec((tm, d), lambda i: (i, 0)),
        out_shape=jax.ShapeDtypeStruct((m, d), out_dtype),
        compiler_params=_cparams("parallel"),
        name="rmsnorm",
    )(x2d, g.reshape(1, d))


def _matmul_kernel(x_ref, w_ref, o_ref):
    o_ref[...] = jnp.dot(x_ref[...].astype(BF16), w_ref[...], preferred_element_type=F32)


def _matmul(x2d, w, tm, tn):
    m, k = x2d.shape
    n = w.shape[1]
    tm = min(tm, m)
    return pl.pallas_call(
        _matmul_kernel,
        grid=(n // tn, m // tm),
        in_specs=[pl.BlockSpec((tm, k), lambda j, i: (i, 0)), pl.BlockSpec((k, tn), lambda j, i: (0, j))],
        out_specs=pl.BlockSpec((tm, tn), lambda j, i: (i, j)),
        out_shape=jax.ShapeDtypeStruct((m, n), F32),
        compiler_params=_cparams("parallel", "parallel"),
        name="matmul",
    )(x2d, w)


D_A = N_HEADS_A * HEAD_DIM_A
D_KV = N_KV_A * HEAD_DIM_A
K_OFF = D_A
V_OFF = K_OFF + D_KV
QI_OFF = V_OFF + D_KV
KI_OFF = QI_OFF + N_IDX * D_IDX
W_OFF = KI_OFF + D_IDX
ATT_COLS = KI_OFF + LANES
_G_K = K_OFF // LANES
_G_V = V_OFF // LANES
_G_KI = KI_OFF // LANES


def _attn_proj_kernel(h_ref, w_ref, cos_ref, sa_ref, sb_ref, o_ref, kb_ref, vb_ref, ki2_ref):
    p = jnp.dot(h_ref[...].astype(BF16), w_ref[...], preferred_element_type=F32)
    cos, sa, sb = cos_ref[...], sa_ref[...], sb_ref[...]
    lane = lax.broadcasted_iota(jnp.int32, cos.shape, 1)
    for g in range(ATT_COLS // LANES):
        x = p[:, g * LANES:(g + 1) * LANES]
        if g == _G_V:
            o_ref[:, g * LANES:(g + 1) * LANES] = x
            vb_ref[...] = x.astype(BF16)
            continue
        y = x * cos + pltpu.roll(x, LANES - 32, 1) * sa + pltpu.roll(x, 32, 1) * sb
        if g == _G_KI:
            y = jnp.where(lane < D_IDX, y, x)
            ki2_ref[...] = jnp.where(lane < D_IDX, y, pltpu.roll(y, D_IDX, 1)).astype(BF16)
        if g == _G_K:
            kb_ref[...] = y.astype(BF16)
        o_ref[:, g * LANES:(g + 1) * LANES] = y


def _rope_tables(pos0, t):
    d = HEAD_DIM_A
    inv = ROPE_THETA ** (-jnp.arange(0, d, 2, dtype=F32) / d)
    ang = (pos0 + jnp.arange(t)).astype(F32)[:, None] * inv[None, :]
    cos, sin = jnp.cos(ang), jnp.sin(ang)
    zero = jnp.zeros_like(sin)
    cos_f = jnp.concatenate([cos] * 4, axis=-1)
    sin_a = jnp.concatenate([-sin, zero] * 2, axis=-1)
    sin_b = jnp.concatenate([zero, sin] * 2, axis=-1)
    return cos_f, sin_a, sin_b


def _attn_proj(h2d, w_att, tables, t, tm):
    m, k = h2d.shape
    nt = t // tm
    tab_spec = pl.BlockSpec((tm, LANES), lambda i: (i % nt, 0))
    lane_spec = pl.BlockSpec((tm, LANES), lambda i: (i, 0))
    lane_shape = jax.ShapeDtypeStruct((m, LANES), BF16)
    return pl.pallas_call(
        _attn_proj_kernel,
        grid=(m // tm,),
        in_specs=[pl.BlockSpec((tm, k), lambda i: (i, 0)), pl.BlockSpec((k, ATT_COLS), lambda i: (0, 0)),
                  tab_spec, tab_spec, tab_spec],
        out_specs=[pl.BlockSpec((tm, ATT_COLS), lambda i: (i, 0)), lane_spec, lane_spec, lane_spec],
        out_shape=[jax.ShapeDtypeStruct((m, ATT_COLS), F32), lane_shape, lane_shape, lane_shape],
        compiler_params=_cparams("parallel"),
        name="attn_proj",
    )(h2d, w_att, *tables)


I16 = jnp.int16
I16_MIN = -(2 ** 15)
I16_MAX = 2 ** 15 - 1
I16_ROWS = 16
I32_ROWS = 8
LOG2E = 1.4426950408889634


def _tree(parts, op):
    while len(parts) > 1:
        parts = [op(parts[i], parts[i + 1]) if i + 1 < len(parts) else parts[i] for i in range(0, len(parts), 2)]
    return parts[0]


def _fold_rows(x, rows, op):
    return _tree([x[g * rows:(g + 1) * rows] for g in range(x.shape[0] // rows)], op)


def _dsa_kernel(pa_ref, ki_ref, k_ref, vt_ref, o_ref, key_sc, hi_sc, lo_sc, *, tq, tk, pos0, t_real, s_real, topk):
    qb = pl.program_id(1)
    q0 = pos0 + qb * tq
    n_adm = jnp.minimum(q0 + tq, s_real)
    n_kt = (n_adm + tk - 1) // tk
    hd = HEAD_DIM_A
    group = N_HEADS_A // N_KV_A
    cols = group * tq

    q_pos = q0 + lax.broadcasted_iota(jnp.int32, (1, tq), 1)
    key_limit = jnp.minimum((lax.shift_right_arithmetic(q_pos, CHUNK.bit_length() - 1) + 1) * CHUNK, s_real)
    key_iota = lax.broadcasted_iota(jnp.int32, (tk, tq), 0)
    low_half = lax.broadcasted_iota(jnp.int32, (tq, LANES), 1) < hd
    halves = (low_half, jnp.logical_not(low_half))
    real_q = qb * tq + lax.broadcasted_iota(jnp.int32, (1, tq), 1) < t_real

    pa = pa_ref[0]
    qi = jnp.concatenate(
        [jnp.where(halves[h % 2], pa[:, QI_OFF + (h // 2) * LANES:QI_OFF + (h // 2 + 1) * LANES] * (D_IDX ** -0.5), 0.0)
         for h in range(N_IDX)], axis=0).astype(BF16)
    w_t = pa[:, KI_OFF:KI_OFF + LANES].T * (N_IDX ** -0.5)
    w_rows = [w_t[D_IDX + h:D_IDX + h + 1, :] for h in range(N_IDX)]

    def score_body(kt, carry):
        start = pl.multiple_of(kt * tk, tk)
        s = lax.dot_general(ki_ref[0, pl.ds(start, tk), :], qi, (((1,), (1,)), ((), ())),
                            preferred_element_type=F32)
        score = jnp.zeros((tk, tq), F32)
        for h in range(N_IDX):
            score = score + jnp.maximum(s[:, h * tq:(h + 1) * tq], 0.0) * w_rows[h]
        score = score + 0.0
        score = jnp.where(start + key_iota < key_limit, score, NEG_INF)
        bits = pltpu.bitcast(score, jnp.int32)
        key = jnp.where(bits < 0, bits ^ 0x7FFFFFFF, bits)
        key_sc[kt] = key
        hi_sc[kt] = lax.shift_right_arithmetic(key, 16).astype(I16)
        return carry

    lax.fori_loop(0, n_kt, score_body, 0)

    def count(pred_fn):
        def body(kt, acc):
            hit = jnp.where(pred_fn(key_sc[kt], kt), 1, 0)
            return acc + _fold_rows(hit, I32_ROWS, jnp.add)

        acc = lax.fori_loop(0, n_kt, body, jnp.zeros((I32_ROWS, tq), jnp.int32))
        return jnp.sum(acc, axis=0, keepdims=True)

    def count16(ref, c_of_tile, cmp):
        def body(kt, acc):
            hit = jnp.where(cmp(ref[kt], c_of_tile(kt).astype(I16)), jnp.ones((), I16), jnp.zeros((), I16))
            return acc + _fold_rows(hit, I16_ROWS, jnp.add)

        acc = lax.fori_loop(0, n_kt, body, jnp.zeros((I16_ROWS, tq), I16))
        return jnp.sum(acc.astype(jnp.int32), axis=0, keepdims=True)

    def kth_digit(ref, kk):
        def body(i, prefix):
            cand = prefix | lax.shift_left(jnp.int32(1), 15 - i)
            cnt = count16(ref, lambda kt: cand + I16_MIN, lax.ge)
            return jnp.where(cnt >= kk, cand, prefix)

        return lax.fori_loop(0, 16, body, jnp.zeros((1, tq), jnp.int32))

    hi_thr = kth_digit(hi_sc, topk) + I16_MIN
    above = jnp.where(hi_thr < I16_MAX, count16(hi_sc, lambda kt: jnp.minimum(hi_thr + 1, I16_MAX), lax.ge), 0)

    def lo_body(kt, carry):
        key = key_sc[kt]
        in_bucket = lax.shift_right_arithmetic(key, 16) == hi_thr
        lo_sc[kt] = jnp.where(in_bucket, (key & 0xFFFF) + I16_MIN, I16_MIN).astype(I16)
        return carry

    lax.fori_loop(0, n_kt, lo_body, 0)
    lo_thr = kth_digit(lo_sc, topk - above)
    thr = jnp.maximum(lax.shift_left(hi_thr, 16) | lo_thr, NEG_KEY)
    cnt_gt = count(lambda key, kt: key > thr)
    cnt_ge = count(lambda key, kt: key >= thr)
    need = topk - cnt_gt
    real_thr = thr != NEG_KEY
    has_excess = jnp.max(jnp.where(real_thr & real_q & (cnt_ge > topk), 1, 0)) > 0

    n_idx_bits = max(1, int(np.ceil(np.log2(key_sc.shape[0] * tk))))

    def tie_search():
        def mark_body(kt, carry):
            lo_sc[kt] = jnp.where(key_sc[kt] == thr, key_iota, I16_MAX).astype(I16)
            return carry

        lax.fori_loop(0, n_kt, mark_body, 0)

        def jbody(i, jq):
            cand = jq | lax.shift_left(jnp.int32(1), n_idx_bits - 1 - i)
            cnt = count16(lo_sc, lambda kt: jnp.clip(cand - kt * tk, I16_MIN, I16_MAX), lax.lt)
            return jnp.where(cnt < need, cand, jq)

        return lax.fori_loop(0, n_idx_bits, jbody, jnp.zeros((1, tq), jnp.int32))

    j_cut = lax.cond(has_excess, tie_search, lambda: jnp.full((1, tq), 2 ** 30, jnp.int32))
    j_cut = jnp.where(real_thr, j_cut, -1)

    qs = []
    for j in range(N_KV_A):
        parts = []
        for g in range(group):
            h = j * group + g
            grp = pa[:, (h // 2) * LANES:(h // 2 + 1) * LANES] * (hd ** -0.5)
            if h % 2 != j:
                grp = pltpu.roll(grp, hd, 1)
            parts.append(jnp.where(halves[j], grp, 0.0))
        qs.append(jnp.concatenate(parts, axis=0).astype(BF16))

    def attn_body(kt, carry):
        start = pl.multiple_of(kt * tk, tk)
        key = key_sc[kt]
        sel = (key > thr) | ((key == thr) & (start + key_iota <= j_cut))
        bias = jnp.where(sel, 0.0, NEG_INF).astype(F32)
        bias = jnp.concatenate([bias] * group, axis=1)
        k_tile = k_ref[0, pl.ds(start, tk), :]
        heads = range(N_KV_A)
        logits = [lax.dot_general(k_tile, qs[j], (((1,), (1,)), ((), ())), preferred_element_type=F32) * LOG2E
                  + bias for j in heads]
        m_new = [jnp.maximum(carry[j][0], jnp.max(_fold_rows(logits[j], I32_ROWS, jnp.maximum), axis=0,
                                                   keepdims=True)) for j in heads]
        alpha = [jnp.exp2(carry[j][0] - m_new[j]) for j in heads]
        p = [jnp.exp2(logits[j] - m_new[j]) for j in heads]
        l_new = [carry[j][1] * alpha[j] + jnp.sum(_fold_rows(p[j], I32_ROWS, jnp.add), axis=0, keepdims=True)
                 for j in heads]
        acc = [carry[j][2] * alpha[j]
               + jnp.dot(vt_ref[0, j * hd:(j + 1) * hd, pl.ds(start, tk)], p[j].astype(BF16),
                         preferred_element_type=F32) for j in heads]
        return tuple((m_new[j], l_new[j], acc[j]) for j in heads)

    init = tuple((jnp.full((1, cols), 0.1 * NEG_INF, F32), jnp.zeros((1, cols), F32),
                  jnp.zeros((hd, cols), F32)) for _ in range(N_KV_A))
    res = lax.fori_loop(0, n_kt, attn_body, init)
    outs = [acc / l_i for _, l_i, acc in res]
    for grp_i in range(N_HEADS_A // 2):
        pair = []
        for h in (2 * grp_i, 2 * grp_i + 1):
            j, g = h // group, h % group
            pair.append(outs[j][:, g * tq:(g + 1) * tq])
        o_ref[0, :, grp_i * LANES:(grp_i + 1) * LANES] = jnp.concatenate(pair, axis=0).T


def _dsa(pa, ki_all, k_all, vt_all, *, pos0, s_real, tq, tk):
    b, t, _ = pa.shape
    assert tk < I16_MAX
    t_pad = _round_up(t, tq)
    pa = jnp.pad(pa, ((0, 0), (0, t_pad - t), (0, 0)))
    s_pad = ki_all.shape[1]
    topk = min(TOPK_MAX, s_real // 4)
    kern = functools.partial(_dsa_kernel, tq=tq, tk=tk, pos0=pos0, t_real=t, s_real=s_real, topk=topk)
    kv_spec = pl.BlockSpec((1, s_pad, LANES), lambda bi, qi: (bi, 0, 0))
    out = pl.pallas_call(
        kern,
        grid=(b, t_pad // tq),
        in_specs=[pl.BlockSpec((1, tq, ATT_COLS), lambda bi, qi: (bi, qi, 0)), kv_spec, kv_spec,
                  pl.BlockSpec((1, LANES, s_pad), lambda bi, qi: (bi, 0, 0))],
        out_specs=pl.BlockSpec((1, tq, D_A), lambda bi, qi: (bi, qi, 0)),
        out_shape=jax.ShapeDtypeStruct((b, t_pad, D_A), F32),
        scratch_shapes=[pltpu.VMEM((s_pad // tk, tk, tq), jnp.int32), pltpu.VMEM((s_pad // tk, tk, tq), I16),
                        pltpu.VMEM((s_pad // tk, tk, tq), I16)],
        compiler_params=_cparams("parallel", "arbitrary"),
        name="dsa",
    )(pa, ki_all, k_all, vt_all)
    return out[:, :t]


D_R = N_HEADS_R * HEAD_DIM_R
D_SHIFT = 3 * D_R + 2 * D_LORA
N_PAIR = D_R // LANES
N_VEC = 8


def _split2(x):
    hi = x.astype(BF16)
    lo = (x - hi.astype(F32)).astype(BF16)
    return hi, lo


def _mm(a, b, ca, cb):
    return lax.dot_general(a, b, (((ca,), (cb,)), ((), ())), preferred_element_type=F32)


def _rwkv_kernel(sh_ref, sh0_ref, mu_ref, vec_ref, wl_ref, s0_ref, y_ref, sout_ref, st_sc, row_sc, *, bb, c):
    ci = pl.program_id(1)
    n = HEAD_DIM_R
    c2 = 2 * c

    @pl.when(ci == 0)
    def _():
        st_sc[...] = s0_ref[...]
        row_sc[...] = sh0_ref[...]

    head0 = lax.broadcasted_iota(jnp.int32, (c, LANES), 1) < n
    r2 = lax.broadcasted_iota(jnp.int32, (c2, c2), 0)
    q2 = lax.broadcasted_iota(jnp.int32, (c2, c2), 1)
    same_head = (r2 // c) == (q2 // c)
    strict = same_head & (r2 > q2)
    incl = same_head & (r2 >= q2)
    eye2 = jnp.where(r2 == q2, 1.0, 0.0).astype(F32)
    rc = lax.broadcasted_iota(jnp.int32, (c, c), 0)
    cc = lax.broadcasted_iota(jnp.int32, (c, c), 1)
    tri = jnp.where(rc >= cc, 1.0, 0.0).astype(BF16)
    tri3 = jnp.concatenate([tri, tri, tri], axis=1)
    la = lax.broadcasted_iota(jnp.int32, (LANES, LANES), 0)
    lb = lax.broadcasted_iota(jnp.int32, (LANES, LANES), 1)
    seg = jnp.where((la // n) == (lb // n), 1.0, 0.0).astype(BF16)
    seg2 = jnp.concatenate([seg, seg], axis=0)
    first_row = lax.broadcasted_iota(jnp.int32, (c, D_SHIFT), 0) == 0

    def seg_sum(x):
        xh, xl = _split2(x)
        return jnp.dot(jnp.concatenate([xh, xl], axis=1), seg2, preferred_element_type=F32)

    def vec(i, sl=slice(None)):
        return vec_ref[i:i + 1, sl]

    n_double = 0
    while (2 << n_double) < c:
        n_double += 1

    def stack(zz):
        return jnp.concatenate([jnp.where(head0, zz, 0.0), jnp.where(head0, 0.0, zz)], axis=0)

    chains = [(bi, p) for bi in range(bb) for p in range(N_PAIR)]
    states = [st_sc[bi, p] for bi, p in chains]

    prep = []
    for bi in range(bb):
        sh = sh_ref[bi]
        prev = jnp.where(first_row, row_sc[bi], pltpu.roll(sh, 1, 0))
        x = sh + (prev - sh) * mu_ref[...]
        r = x[:, :D_R]
        k = x[:, D_R:2 * D_R]
        v = x[:, 2 * D_R:3 * D_R]
        xl = x[:, 3 * D_R:]
        lora_in = jnp.where(head0, jnp.tanh(xl), xl)
        wa = jnp.dot(lora_in.astype(BF16), wl_ref[...], preferred_element_type=F32)
        w_raw = vec(0) + wa[:, :D_R]
        a = jax.nn.sigmoid(vec(1) + wa[:, D_R:])
        lw = -jnp.exp(-jax.nn.softplus(-w_raw) - 0.5)
        kkv = k * vec(2)
        kk2 = kkv * kkv
        ss = jnp.concatenate([seg_sum(kk2[:, p * LANES:(p + 1) * LANES]) for p in range(N_PAIR)], axis=1)
        kk = kkv / jnp.maximum(jnp.sqrt(ss), 1e-12)
        k2 = k * (1.0 + (a - 1.0) * vec(3))
        l_hi = lw.astype(BF16)
        l_r1 = lw - l_hi.astype(F32)
        l_mid = l_r1.astype(BF16)
        l_lo = (l_r1 - l_mid.astype(F32)).astype(BF16)
        cum = jnp.dot(tri3, jnp.concatenate([l_hi, l_mid, l_lo], axis=0), preferred_element_type=F32)
        p_fwd = jnp.exp(cum)
        p_inv = jnp.exp(-cum)
        prep.append(dict(v=v, p_fwd=p_fwd, at=-kk * jnp.exp(cum - lw), bt=kk * a * p_inv, kt=k2 * p_inv,
                         rt=r * p_fwd, rk_prod=r * k2 * vec(4)))
    for bi in range(bb):
        row_sc[bi] = sh_ref[bi, c - 1:c, :]

    ops = []
    for bi, p in chains:
        sl = slice(p * LANES, (p + 1) * LANES)
        d = prep[bi]
        b_st, k_st, v_st = stack(d["bt"][:, sl]), stack(d["kt"][:, sl]), stack(d["v"][:, sl])
        ar = jnp.concatenate([stack(d["at"][:, sl]), stack(d["rt"][:, sl])], axis=0).astype(BF16)
        bk = jnp.concatenate([b_st, k_st], axis=0).astype(BF16)
        p_last = d["p_fwd"][c - 1:c, sl]
        ops.append(dict(ar=ar, bk=bk, v_b=v_st.astype(BF16), p_last=p_last,
                        b_dec=(b_st * p_last).astype(BF16), k_dec=(k_st * p_last).astype(BF16)))
    grams = [_mm(o["ar"], o["bk"], 1, 1) for o in ops]
    l_ab = [jnp.where(strict, g[:c2, :c2], 0.0) for g in grams]
    l_ak = [jnp.where(strict, g[:c2, c2:], 0.0).astype(BF16) for g in grams]
    m_rb = [jnp.where(incl, g[c2:, :c2], 0.0).astype(BF16) for g in grams]
    m_rk = [jnp.where(incl, g[c2:, c2:], 0.0).astype(BF16) for g in grams]
    inv = [eye2 + l for l in l_ab]
    lp = l_ab
    for _ in range(n_double):
        lpb = [l.astype(BF16) for l in lp]
        lp = [_mm(l, l, 1, 0) for l in lpb]
        inv = [x + _mm(x.astype(BF16), l.astype(BF16), 1, 0) for x, l in zip(inv, lp)]

    s_b = [s.astype(BF16) for s in states]
    z = [_mm(o["ar"][:c2], sb, 1, 1) + _mm(la, o["v_b"], 1, 0) for o, sb, la in zip(ops, s_b, l_ak)]
    u_b = [_mm(x.astype(BF16), zz.astype(BF16), 1, 0).astype(BF16) for x, zz in zip(inv, z)]
    y_st = [_mm(o["ar"][c2:], sb, 1, 1) + _mm(mb, u, 1, 0) + _mm(mk, o["v_b"], 1, 0)
            for o, sb, mb, mk, u in zip(ops, s_b, m_rb, m_rk, u_b)]
    new_states = [s * o["p_last"] + _mm(u, o["b_dec"], 0, 0) + _mm(o["v_b"], o["k_dec"], 0, 0)
                  for s, o, u in zip(states, ops, u_b)]
    for (bi, p), s_new, ys in zip(chains, new_states, y_st):
        sl = slice(p * LANES, (p + 1) * LANES)
        st_sc[bi, p] = s_new
        y = ys[:c] + ys[c:]
        dev = y - seg_sum(y) * (1.0 / n)
        var = seg_sum(dev * dev) * (1.0 / n)
        yn = dev * lax.rsqrt(var + GN_EPS) * vec(5, sl) + vec(6, sl)
        y_ref[bi, :, sl] = yn + seg_sum(prep[bi]["rk_prod"][:, sl]) * prep[bi]["v"][:, sl]

    @pl.when(ci == pl.num_programs(1) - 1)
    def _():
        sout_ref[...] = st_sc[...]


def _rwkv(sh, shift0, mu, w0, w2, a0, a2, k_k, k_a, r_k, ln_w, ln_b, state0, *, bb, c):
    b, t, _ = sh.shape
    n = HEAD_DIM_R
    eye = jnp.eye(2, dtype=F32)
    s0 = jnp.einsum('bphij,hg->bphigj', state0.reshape(b, N_PAIR, 2, n, n), eye).reshape(b, N_PAIR, LANES, LANES)
    vecs = jnp.stack([w0, a0, k_k, k_a, r_k.reshape(D_R), ln_w, ln_b, jnp.zeros_like(w0)])
    zero = jnp.zeros((D_LORA, D_R), F32)
    w_lora = jnp.concatenate([jnp.concatenate([w2, zero], axis=1), jnp.concatenate([zero, a2], axis=1)],
                             axis=0).astype(BF16)
    full = lambda shape: pl.BlockSpec(shape, lambda bi, ci: (0,) * len(shape))
    st_spec = pl.BlockSpec((bb, N_PAIR, LANES, LANES), lambda bi, ci: (bi, 0, 0, 0))
    y, s_out = pl.pallas_call(
        functools.partial(_rwkv_kernel, bb=bb, c=c),
        grid=(b // bb, t // c),
        in_specs=[pl.BlockSpec((bb, c, D_SHIFT), lambda bi, ci: (bi, ci, 0)),
                  pl.BlockSpec((bb, 1, D_SHIFT), lambda bi, ci: (bi, 0, 0)),
                  full((1, D_SHIFT)), full((N_VEC, D_R)), full((2 * D_LORA, 2 * D_R)), st_spec],
        out_specs=[pl.BlockSpec((bb, c, D_R), lambda bi, ci: (bi, ci, 0)), st_spec],
        out_shape=[jax.ShapeDtypeStruct((b, t, D_R), F32),
                   jax.ShapeDtypeStruct((b, N_PAIR, LANES, LANES), F32)],
        scratch_shapes=[pltpu.VMEM((bb, N_PAIR, LANES, LANES), F32), pltpu.VMEM((bb, 1, D_SHIFT), F32)],
        compiler_params=_cparams("parallel", "arbitrary"),
        name="rwkv",
    )(sh, shift0, mu.reshape(1, D_SHIFT), vecs, w_lora, s0)
    s_t = jnp.einsum('bphigj,hg->bphij', s_out.reshape(b, N_PAIR, 2, n, 2, n), eye)
    return y, s_t.reshape(b, N_HEADS_R, n, n)


def _mem_attn_kernel(q_ref, mk_ref, mv_ref, o_ref):
    scale = HEAD_DIM_M ** -0.5
    for h in range(N_HEADS_M):
        sl = slice(h * HEAD_DIM_M, (h + 1) * HEAD_DIM_M)
        q = q_ref[0, :, sl].astype(BF16)
        logits = lax.dot_general(q, mk_ref[0, :, sl], (((1,), (1,)), ((), ())), preferred_element_type=F32) * scale
        e = jnp.exp(logits - jnp.max(logits, axis=1, keepdims=True))
        p = e / jnp.sum(e, axis=1, keepdims=True)
        o_ref[0, :, sl] = jnp.dot(p.astype(BF16), mv_ref[0, :, sl], preferred_element_type=F32)


def _mem_attn(qm, mk, mv, tm):
    b, t, d = qm.shape
    nm = mk.shape[1]
    return pl.pallas_call(
        _mem_attn_kernel,
        grid=(b, t // tm),
        in_specs=[pl.BlockSpec((1, tm, d), lambda bi, i: (bi, i, 0)),
                  pl.BlockSpec((1, nm, d), lambda bi, i: (bi, 0, 0)),
                  pl.BlockSpec((1, nm, d), lambda bi, i: (bi, 0, 0))],
        out_specs=pl.BlockSpec((1, tm, d), lambda bi, i: (bi, i, 0)),
        out_shape=jax.ShapeDtypeStruct((b, t, d), F32),
        compiler_params=_cparams("parallel", "parallel"),
        name="mem_attn",
    )(qm, mk, mv)


def _merge_kernel(oa_ref, yr_ref, om_ref, zg_ref, x_ref, wa_ref, wr_ref, wm_ref, wo_ref, g_ref, xo_ref, ho_ref):
    d_br = oa_ref.shape[1]
    d = x_ref.shape[1]

    def branch(o, z, w_ref):
        return jnp.dot((o * jax.nn.silu(z)).astype(BF16), w_ref[...], preferred_element_type=F32)

    pa = branch(oa_ref[...], zg_ref[:, 0:d_br], wa_ref)
    pr = branch(yr_ref[...], zg_ref[:, d_br:2 * d_br], wr_ref)
    pm = branch(om_ref[...], zg_ref[:, 2 * d_br:3 * d_br], wm_ref)
    g0 = 3 * d_br
    merged = (jax.nn.sigmoid(zg_ref[:, g0:g0 + d]) * pa
              + jax.nn.sigmoid(zg_ref[:, g0 + d:g0 + 2 * d]) * pr
              + jax.nn.sigmoid(zg_ref[:, g0 + 2 * d:g0 + 3 * d]) * pm)
    xn = x_ref[...] + jnp.dot(merged.astype(BF16), wo_ref[...], preferred_element_type=F32)
    xo_ref[...] = xn
    y = xn * lax.rsqrt(jnp.mean(xn * xn, axis=-1, keepdims=True) + RMS_EPS)
    ho_ref[...] = y * g_ref[...]


def _merge(oa, yr, om, zg, x2d, wa, wr, wm, wo, g_next, tm):
    m, d = x2d.shape
    d_br = oa.shape[1]
    tm = min(tm, m)
    row = lambda w: pl.BlockSpec((tm, w), lambda i: (i, 0))
    full = lambda a: pl.BlockSpec(a.shape, lambda i: (0, 0))
    g2 = g_next.reshape(1, d)
    return pl.pallas_call(
        _merge_kernel,
        grid=(m // tm,),
        in_specs=[row(d_br), row(d_br), row(d_br), row(zg.shape[1]), row(d),
                  full(wa), full(wr), full(wm), full(wo), full(g2)],
        out_specs=[row(d), row(d)],
        out_shape=[jax.ShapeDtypeStruct((m, d), F32), jax.ShapeDtypeStruct((m, d), F32)],
        compiler_params=_cparams("parallel"),
        name="merge",
    )(oa, yr, om, zg, x2d, wa, wr, wm, wo, g2)


def _round_up(x, m):
    return (x + m - 1) // m * m


IN_SIZES = (D_A, D_KV, D_KV, N_IDX * D_IDX, D_IDX, N_IDX, D_A, D_SHIFT, D_R, N_HEADS_M * HEAD_DIM_M,
            N_HEADS_M * HEAD_DIM_M)
IN_OFFS = tuple(int(o) for o in np.cumsum((0,) + IN_SIZES))


def _repack_kernel(w_ref, att_ref, sh_ref, qm_ref, zg_ref):
    col = lambda i: w_ref[:, IN_OFFS[i]:IN_OFFS[i + 1]]
    rows = w_ref.shape[0]
    att_ref[:, :W_OFF + N_IDX] = w_ref[:, :IN_OFFS[6]].astype(BF16)
    att_ref[:, W_OFF + N_IDX:] = jnp.zeros((rows, ATT_COLS - W_OFF - N_IDX), BF16)
    sh_ref[...] = col(7).astype(BF16)
    qm_ref[...] = col(9).astype(BF16)
    zg_ref[:, :D_A] = col(6).astype(BF16)
    zg_ref[:, D_A:D_A + D_R] = col(8).astype(BF16)
    zg_ref[:, D_A + D_R:D_A + D_R + IN_SIZES[10]] = col(10).astype(BF16)
    zg_ref[:, D_A + D_R + IN_SIZES[10]:] = w_ref[:, IN_OFFS[11]:].astype(BF16)


def _split_w_in(w_in):
    d_model, d_in = w_in.shape
    n_zg = D_A + D_R + IN_SIZES[10] + d_in - IN_OFFS[11]
    tr = 128
    widths = (ATT_COLS, D_SHIFT, IN_SIZES[9], n_zg)
    return tuple(pl.pallas_call(
        _repack_kernel,
        grid=(d_model // tr,),
        in_specs=[pl.BlockSpec((tr, d_in), lambda i: (i, 0))],
        out_specs=[pl.BlockSpec((tr, w), lambda i: (i, 0)) for w in widths],
        out_shape=[jax.ShapeDtypeStruct((d_model, w), BF16) for w in widths],
        compiler_params=_cparams("parallel"),
        name="repack",
    )(w_in))


def _layer(x2d, h2d, b, t, pos0, past_k, past_v, past_ki, wkv0, shift0, mem_k, mem_v, lw, g_next, cfg):
    (w_att, w_sh, w_qm, w_zg, mu_shift, w0, w2, a0, a2, k_k, k_a, r_k, ln_w, ln_b, w_br_a, w_br_r, w_br_m,
     w_out) = lw
    tm = cfg["tm"]
    d_a = N_HEADS_A * HEAD_DIM_A
    d_kv = N_KV_A * HEAD_DIM_A
    d_r = N_HEADS_R * HEAD_DIM_R
    d_m = N_HEADS_M * HEAD_DIM_M

    tables = _rope_tables(pos0, t)
    pa, kb, vb, ki2 = _attn_proj(h2d, w_att, tables, t, min(tm, t))
    pa = pa.reshape(b, t, ATT_COLS)
    sh = _matmul(h2d, w_sh, tm, w_sh.shape[1]).reshape(b, t, -1)
    qm = _matmul(h2d, w_qm, tm, w_qm.shape[1]).reshape(b, t, d_m)
    zg = _matmul(h2d, w_zg, tm, cfg["tn_zg"])

    k_new = pa[..., K_OFF:K_OFF + D_KV].reshape(b, t, N_KV_A, HEAD_DIM_A)
    v_new = pa[..., V_OFF:V_OFF + D_KV].reshape(b, t, N_KV_A, HEAD_DIM_A)
    ki_new = pa[..., KI_OFF:KI_OFF + D_IDX]
    s_real = pos0 + t
    tq, tk = cfg["tq"], cfg["tk"]
    s_pad = _round_up(s_real, tk)

    def with_past(past, new):
        full = jnp.concatenate([past.reshape(b, pos0, LANES).astype(BF16), new.reshape(b, t, LANES)], axis=1)
        return jnp.pad(full, ((0, 0), (0, s_pad - s_real), (0, 0)))

    vt_all = with_past(past_v, vb).transpose(0, 2, 1)
    o_a = _dsa(pa, with_past(jnp.concatenate([past_ki, past_ki], axis=-1), ki2), with_past(past_k, kb),
               vt_all, pos0=pos0, s_real=s_real, tq=tq, tk=tk).reshape(b * t, d_a)

    y_r, wkv_t = _rwkv(sh, shift0, mu_shift, w0, w2, a0, a2, k_k, k_a, r_k, ln_w, ln_b, wkv0,
                       bb=cfg["bb"], c=CHUNK)
    y_r = y_r.reshape(b * t, d_r)

    o_m = _mem_attn(qm, mem_k, mem_v, min(tm, t)).reshape(b * t, d_m)

    x_new, h_next = _merge(o_a, y_r, o_m, zg, x2d, w_br_a, w_br_r, w_br_m, w_out, g_next, cfg["tm_merge"])
    return x_new, h_next, k_new, v_new, ki_new, wkv_t, sh[:, -1:]


PROMPT_CFG = dict(tm=512, tn_zg=2304, tq=256, tk=1024, bb=2, tm_merge=256)
SAMPLE_CFG = dict(tm=512, tn_zg=2304, tq=128, tk=1152, bb=2, tm_merge=256)


def kernel(x_prompt, x_sample, mem_prompt, cache_k, cache_v, cache_kidx, state_wkv, state_shift, cache_mem_k,
           cache_mem_v, norm_g, w_in, mu_shift, w0, w2, a0, a2, k_k, k_a, r_k, ln_x_w, ln_x_b, mem_norm_g,
           w_mem_kv, w_br_a, w_br_r, w_br_m, w_out, final_norm_g):
    depth = w_in.shape[0]
    bp, tp, d = x_prompt.shape
    bs, ts, _ = x_sample.shape
    n_mem = mem_prompt.shape[1]
    d_m = N_HEADS_M * HEAD_DIM_M
    past = cache_k.shape[2]

    xp = x_prompt.reshape(bp * tp, d)
    xs = x_sample.reshape(bs * ts, d)
    hp = _rmsnorm(xp, norm_g[0], F32, 512)
    hs = _rmsnorm(xs, norm_g[0], F32, 512)
    mem2d = mem_prompt.reshape(bp * n_mem, d)

    empty_kv = jnp.zeros((bp, 0, N_KV_A, HEAD_DIM_A), F32)
    empty_ki = jnp.zeros((bp, 0, D_IDX), F32)
    zero_state = jnp.zeros((bp, N_HEADS_R, HEAD_DIM_R, HEAD_DIM_R), F32)
    zero_shift = jnp.zeros((bp, 1, mu_shift.shape[1]), F32)

    outs_p = [[] for _ in range(7)]
    outs_s = [[] for _ in range(5)]
    for l in range(depth):
        g_next = norm_g[l + 1] if l + 1 < depth else final_norm_g
        lw = _split_w_in(w_in[l]) + (mu_shift[l], w0[l], w2[l], a0[l], a2[l], k_k[l], k_a[l], r_k[l], ln_x_w[l],
                                     ln_x_b[l], w_br_a[l].astype(BF16), w_br_r[l].astype(BF16),
                                     w_br_m[l].astype(BF16), w_out[l].astype(BF16))
        mem_h = _rmsnorm(mem2d, mem_norm_g[l], F32, 512)
        mkv = _matmul(mem_h, w_mem_kv[l].astype(BF16), 512, 2 * d_m).reshape(bp, n_mem, 2 * d_m)
        mk, mv = mkv[..., :d_m], mkv[..., d_m:]
        xp, hp, k_n, v_n, ki_n, wkv_n, sh_n = _layer(
            xp, hp, bp, tp, 0, empty_kv, empty_kv, empty_ki, zero_state, zero_shift,
            mk.astype(BF16), mv.astype(BF16), lw, g_next, PROMPT_CFG)
        for lst, val in zip(outs_p, (k_n, v_n, ki_n, wkv_n, sh_n,
                                     mk.reshape(bp, n_mem, N_HEADS_M, HEAD_DIM_M),
                                     mv.reshape(bp, n_mem, N_HEADS_M, HEAD_DIM_M))):
            lst.append(val)
        xs, hs, k_n, v_n, ki_n, wkv_n, sh_n = _layer(
            xs, hs, bs, ts, past, cache_k[l], cache_v[l], cache_kidx[l], state_wkv[l], state_shift[l],
            cache_mem_k[l].reshape(bs, n_mem, d_m).astype(BF16), cache_mem_v[l].reshape(bs, n_mem, d_m).astype(BF16),
            lw, g_next, SAMPLE_CFG)
        for lst, val in zip(outs_s, (k_n, v_n, ki_n, wkv_n, sh_n)):
            lst.append(val)

    y_prompt = hp.reshape(bp, tp, d)
    y_sample = hs.reshape(bs, ts, d)
    return (y_prompt, y_sample) + tuple(jnp.stack(o) for o in outs_p) + tuple(jnp.stack(o) for o in outs_s)
```

```python
import functools

import jax
import jax.numpy as jnp
import numpy as np
from jax import lax
from jax.experimental import pallas as pl
from jax.experimental.pallas import tpu as pltpu

F32 = jnp.float32
BF16 = jnp.bfloat16

CHUNK = 64
ROPE_THETA = 10000.0
RMS_EPS = 1e-6
NEG_INF = -1e30
N_HEADS_A = 8
HEAD_DIM_A = 64
N_KV_A = 2
N_IDX = 4
D_IDX = 64
TOPK_MAX = 256
N_HEADS_R = 8
HEAD_DIM_R = 64
D_LORA = 64
GN_EPS = 64e-5
N_HEADS_M = 4
HEAD_DIM_M = 128
N_BRANCH = 3

LANES = 128
VMEM_LIMIT = 56 * 1024 * 1024

_NEG_BITS = int(np.array(NEG_INF, np.float32).view(np.int32))
NEG_KEY = _NEG_BITS ^ 0x7FFFFFFF
SIGN_FLIP = -(2 ** 31)


def _cparams(*sem):
    return pltpu.CompilerParams(dimension_semantics=sem, vmem_limit_bytes=VMEM_LIMIT)


def _rmsnorm_kernel(x_ref, g_ref, o_ref):
    x = x_ref[...]
    y = x * lax.rsqrt(jnp.mean(x * x, axis=-1, keepdims=True) + RMS_EPS)
    o_ref[...] = (y * g_ref[...]).astype(o_ref.dtype)


def _rmsnorm(x2d, g, out_dtype, tm):
    m, d = x2d.shape
    tm = min(tm, m)
    return pl.pallas_call(
        _rmsnorm_kernel,
        grid=(m // tm,),
        in_specs=[pl.BlockSpec((tm, d), lambda i: (i, 0)), pl.BlockSpec((1, d), lambda i: (0, 0))],
        out_specs=pl.BlockSpec((tm, d), lambda i: (i, 0)),
        out_shape=jax.ShapeDtypeStruct((m, d), out_dtype),
        compiler_params=_cparams("parallel"),
        name="rmsnorm",
    )(x2d, g.reshape(1, d))


def _matmul_kernel(x_ref, w_ref, o_ref):
    o_ref[...] = jnp.dot(x_ref[...].astype(BF16), w_ref[...], preferred_element_type=F32)


def _matmul(x2d, w, tm, tn):
    m, k = x2d.shape
    n = w.shape[1]
    tm = min(tm, m)
    return pl.pallas_call(
        _matmul_kernel,
        grid=(n // tn, m // tm),
        in_specs=[pl.BlockSpec((tm, k), lambda j, i: (i, 0)), pl.BlockSpec((k, tn), lambda j, i: (0, j))],
        out_specs=pl.BlockSpec((tm, tn), lambda j, i: (i, j)),
        out_shape=jax.ShapeDtypeStruct((m, n), F32),
        compiler_params=_cparams("parallel", "parallel"),
        name="matmul",
    )(x2d, w)


D_A = N_HEADS_A * HEAD_DIM_A
D_KV = N_KV_A * HEAD_DIM_A
K_OFF = D_A
V_OFF = K_OFF + D_KV
QI_OFF = V_OFF + D_KV
KI_OFF = QI_OFF + N_IDX * D_IDX
W_OFF = KI_OFF + D_IDX
ATT_COLS = KI_OFF + LANES
_G_K = K_OFF // LANES
_G_V = V_OFF // LANES
_G_KI = KI_OFF // LANES


def _attn_proj_kernel(h_ref, w_ref, cos_ref, sa_ref, sb_ref, o_ref, kb_ref, vb_ref, ki2_ref):
    p = jnp.dot(h_ref[...].astype(BF16), w_ref[...], preferred_element_type=F32)
    cos, sa, sb = cos_ref[...], sa_ref[...], sb_ref[...]
    lane = lax.broadcasted_iota(jnp.int32, cos.shape, 1)
    for g in range(ATT_COLS // LANES):
        x = p[:, g * LANES:(g + 1) * LANES]
        if g == _G_V:
            o_ref[:, g * LANES:(g + 1) * LANES] = x
            vb_ref[...] = x.astype(BF16)
            continue
        y = x * cos + pltpu.roll(x, LANES - 32, 1) * sa + pltpu.roll(x, 32, 1) * sb
        if g == _G_KI:
            y = jnp.where(lane < D_IDX, y, x)
            ki2_ref[...] = jnp.where(lane < D_IDX, y, pltpu.roll(y, D_IDX, 1)).astype(BF16)
        if g == _G_K:
            kb_ref[...] = y.astype(BF16)
        o_ref[:, g * LANES:(g + 1) * LANES] = y


def _rope_tables(pos0, t):
    d = HEAD_DIM_A
    inv = ROPE_THETA ** (-jnp.arange(0, d, 2, dtype=F32) / d)
    ang = (pos0 + jnp.arange(t)).astype(F32)[:, None] * inv[None, :]
    cos, sin = jnp.cos(ang), jnp.sin(ang)
    zero = jnp.zeros_like(sin)
    cos_f = jnp.concatenate([cos] * 4, axis=-1)
    sin_a = jnp.concatenate([-sin, zero] * 2, axis=-1)
    sin_b = jnp.concatenate([zero, sin] * 2, axis=-1)
    return cos_f, sin_a, sin_b


def _attn_proj(h2d, w_att, tables, t, tm):
    m, k = h2d.shape
    nt = t // tm
    tab_spec = pl.BlockSpec((tm, LANES), lambda i: (i % nt, 0))
    lane_spec = pl.BlockSpec((tm, LANES), lambda i: (i, 0))
    lane_shape = jax.ShapeDtypeStruct((m, LANES), BF16)
    return pl.pallas_call(
        _attn_proj_kernel,
        grid=(m // tm,),
        in_specs=[pl.BlockSpec((tm, k), lambda i: (i, 0)), pl.BlockSpec((k, ATT_COLS), lambda i: (0, 0)),
                  tab_spec, tab_spec, tab_spec],
        out_specs=[pl.BlockSpec((tm, ATT_COLS), lambda i: (i, 0)), lane_spec, lane_spec, lane_spec],
        out_shape=[jax.ShapeDtypeStruct((m, ATT_COLS), F32), lane_shape, lane_shape, lane_shape],
        compiler_params=_cparams("parallel"),
        name="attn_proj",
    )(h2d, w_att, *tables)


I16 = jnp.int16
I16_MIN = -(2 ** 15)
I16_MAX = 2 ** 15 - 1
I16_ROWS = 16
I32_ROWS = 8
LOG2E = 1.4426950408889634


def _tree(parts, op):
    while len(parts) > 1:
        parts = [op(parts[i], parts[i + 1]) if i + 1 < len(parts) else parts[i] for i in range(0, len(parts), 2)]
    return parts[0]


def _fold_rows(x, rows, op):
    return _tree([x[g * rows:(g + 1) * rows] for g in range(x.shape[0] // rows)], op)


def _dsa_kernel(pa_ref, ki_ref, k_ref, vt_ref, o_ref, key_sc, dig_sc, *, tq, tk, pos0, t_real, s_real, topk):
    qb = pl.program_id(1)
    q0 = pos0 + qb * tq
    n_adm = jnp.minimum(q0 + tq, s_real)
    n_kt = (n_adm + tk - 1) // tk
    hd = HEAD_DIM_A
    group = N_HEADS_A // N_KV_A
    cols = group * tq

    q_pos = q0 + lax.broadcasted_iota(jnp.int32, (1, tq), 1)
    key_limit = jnp.minimum((lax.shift_right_arithmetic(q_pos, CHUNK.bit_length() - 1) + 1) * CHUNK, s_real)
    key_iota = lax.broadcasted_iota(jnp.int32, (tk, tq), 0)
    low_half = lax.broadcasted_iota(jnp.int32, (tq, LANES), 1) < hd
    halves = (low_half, jnp.logical_not(low_half))
    real_q = qb * tq + lax.broadcasted_iota(jnp.int32, (1, tq), 1) < t_real

    pa = pa_ref[0]
    qi = jnp.concatenate(
        [jnp.where(halves[h % 2], pa[:, QI_OFF + (h // 2) * LANES:QI_OFF + (h // 2 + 1) * LANES] * (D_IDX ** -0.5), 0.0)
         for h in range(N_IDX)], axis=0).astype(BF16)
    w_t = pa[:, KI_OFF:KI_OFF + LANES].T * (N_IDX ** -0.5)
    w_rows = [w_t[D_IDX + h:D_IDX + h + 1, :] for h in range(N_IDX)]

    def score_body(kt, carry):
        start = pl.multiple_of(kt * tk, tk)
        s = lax.dot_general(ki_ref[0, pl.ds(start, tk), :], qi, (((1,), (1,)), ((), ())),
                            preferred_element_type=F32)
        score = jnp.zeros((tk, tq), F32)
        for h in range(N_IDX):
            score = score + jnp.maximum(s[:, h * tq:(h + 1) * tq], 0.0) * w_rows[h]
        score = score + 0.0
        score = jnp.where(start + key_iota < key_limit, score, NEG_INF)
        bits = pltpu.bitcast(score, jnp.int32)
        key = jnp.where(bits < 0, bits ^ 0x7FFFFFFF, bits)
        key_sc[kt] = key
        dig_sc[kt] = lax.shift_right_arithmetic(key, 16).astype(I16)
        return carry

    lax.fori_loop(0, n_kt, score_body, 0)

    def count(pred_fn):
        def body(kt, acc):
            hit = jnp.where(pred_fn(key_sc[kt], kt), 1, 0)
            return acc + _fold_rows(hit, I32_ROWS, jnp.add)

        acc = lax.fori_loop(0, n_kt, body, jnp.zeros((I32_ROWS, tq), jnp.int32))
        return jnp.sum(acc, axis=0, keepdims=True)

    def count16(ref, c_of_tile, cmp):
        def body(kt, acc):
            hit = jnp.where(cmp(ref[kt], c_of_tile(kt).astype(I16)), jnp.ones((), I16), jnp.zeros((), I16))
            return acc + _fold_rows(hit, I16_ROWS, jnp.add)

        acc = lax.fori_loop(0, n_kt, body, jnp.zeros((I16_ROWS, tq), I16))
        return jnp.sum(acc.astype(jnp.int32), axis=0, keepdims=True)

    def kth_digit(ref, kk):
        def body(i, prefix):
            cand = prefix | lax.shift_left(jnp.int32(1), 15 - i)
            cnt = count16(ref, lambda kt: cand + I16_MIN, lax.ge)
            return jnp.where(cnt >= kk, cand, prefix)

        return lax.fori_loop(0, 16, body, jnp.zeros((1, tq), jnp.int32))

    hi_thr = kth_digit(dig_sc, topk) + I16_MIN
    above = jnp.where(hi_thr < I16_MAX, count16(dig_sc, lambda kt: jnp.minimum(hi_thr + 1, I16_MAX), lax.ge), 0)

    def lo_body(kt, carry):
        key = key_sc[kt]
        in_bucket = lax.shift_right_arithmetic(key, 16) == hi_thr
        dig_sc[kt] = jnp.where(in_bucket, (key & 0xFFFF) + I16_MIN, I16_MIN).astype(I16)
        return carry

    lax.fori_loop(0, n_kt, lo_body, 0)
    lo_thr = kth_digit(dig_sc, topk - above)
    thr = jnp.maximum(lax.shift_left(hi_thr, 16) | lo_thr, NEG_KEY)
    cnt_gt = count(lambda key, kt: key > thr)
    cnt_ge = count(lambda key, kt: key >= thr)
    need = topk - cnt_gt
    real_thr = thr != NEG_KEY
    has_excess = jnp.max(jnp.where(real_thr & real_q & (cnt_ge > topk), 1, 0)) > 0

    n_idx_bits = max(1, int(np.ceil(np.log2(key_sc.shape[0] * tk))))

    def tie_search():
        def mark_body(kt, carry):
            dig_sc[kt] = jnp.where(key_sc[kt] == thr, key_iota, I16_MAX).astype(I16)
            return carry

        lax.fori_loop(0, n_kt, mark_body, 0)

        def jbody(i, jq):
            cand = jq | lax.shift_left(jnp.int32(1), n_idx_bits - 1 - i)
            cnt = count16(dig_sc, lambda kt: jnp.clip(cand - kt * tk, I16_MIN, I16_MAX), lax.lt)
            return jnp.where(cnt < need, cand, jq)

        return lax.fori_loop(0, n_idx_bits, jbody, jnp.zeros((1, tq), jnp.int32))

    j_cut = lax.cond(has_excess, tie_search, lambda: jnp.full((1, tq), 2 ** 30, jnp.int32))
    j_cut = jnp.where(real_thr, j_cut, -1)

    qs = []
    for j in range(N_KV_A):
        parts = []
        for g in range(group):
            h = j * group + g
            grp = pa[:, (h // 2) * LANES:(h // 2 + 1) * LANES] * (hd ** -0.5)
            if h % 2 != j:
                grp = pltpu.roll(grp, hd, 1)
            parts.append(jnp.where(halves[j], grp, 0.0))
        qs.append(jnp.concatenate(parts, axis=0).astype(BF16))

    def attn_body(kt, carry):
        start = pl.multiple_of(kt * tk, tk)
        key = key_sc[kt]
        sel = (key > thr) | ((key == thr) & (start + key_iota <= j_cut))
        bias = jnp.where(sel, 0.0, NEG_INF).astype(F32)
        bias = jnp.concatenate([bias] * group, axis=1)
        k_tile = k_ref[0, pl.ds(start, tk), :]
        heads = range(N_KV_A)
        logits = [lax.dot_general(k_tile, qs[j], (((1,), (1,)), ((), ())), preferred_element_type=F32) * LOG2E
                  + bias for j in heads]
        m_new = [jnp.maximum(carry[j][0], jnp.max(_fold_rows(logits[j], I32_ROWS, jnp.maximum), axis=0,
                                                   keepdims=True)) for j in heads]
        alpha = [jnp.exp2(carry[j][0] - m_new[j]) for j in heads]
        p = [jnp.exp2(logits[j] - m_new[j]) for j in heads]
        l_new = [carry[j][1] * alpha[j] + jnp.sum(_fold_rows(p[j], I32_ROWS, jnp.add), axis=0, keepdims=True)
                 for j in heads]
        acc = [carry[j][2] * alpha[j]
               + jnp.dot(vt_ref[0, j * hd:(j + 1) * hd, pl.ds(start, tk)], p[j].astype(BF16),
                         preferred_element_type=F32) for j in heads]
        return tuple((m_new[j], l_new[j], acc[j]) for j in heads)

    init = tuple((jnp.full((1, cols), 0.1 * NEG_INF, F32), jnp.zeros((1, cols), F32),
                  jnp.zeros((hd, cols), F32)) for _ in range(N_KV_A))
    res = lax.fori_loop(0, n_kt, attn_body, init)
    outs = [acc / l_i for _, l_i, acc in res]
    for grp_i in range(N_HEADS_A // 2):
        pair = []
        for h in (2 * grp_i, 2 * grp_i + 1):
            j, g = h // group, h % group
            pair.append(outs[j][:, g * tq:(g + 1) * tq])
        o_ref[0, :, grp_i * LANES:(grp_i + 1) * LANES] = jnp.concatenate(pair, axis=0).T


def _dsa(pa, ki_all, k_all, vt_all, *, pos0, s_real, tq, tk):
    b, t, _ = pa.shape
    assert tk < I16_MAX
    t_pad = _round_up(t, tq)
    pa = jnp.pad(pa, ((0, 0), (0, t_pad - t), (0, 0)))
    s_pad = ki_all.shape[1]
    topk = min(TOPK_MAX, s_real // 4)
    kern = functools.partial(_dsa_kernel, tq=tq, tk=tk, pos0=pos0, t_real=t, s_real=s_real, topk=topk)
    kv_spec = pl.BlockSpec((1, s_pad, LANES), lambda bi, qi: (bi, 0, 0), pipeline_mode=pl.Buffered(1))
    out = pl.pallas_call(
        kern,
        grid=(b, t_pad // tq),
        in_specs=[pl.BlockSpec((1, tq, ATT_COLS), lambda bi, qi: (bi, qi, 0)), kv_spec, kv_spec,
                  pl.BlockSpec((1, LANES, s_pad), lambda bi, qi: (bi, 0, 0), pipeline_mode=pl.Buffered(1))],
        out_specs=pl.BlockSpec((1, tq, D_A), lambda bi, qi: (bi, qi, 0)),
        out_shape=jax.ShapeDtypeStruct((b, t_pad, D_A), F32),
        scratch_shapes=[pltpu.VMEM((s_pad // tk, tk, tq), jnp.int32), pltpu.VMEM((s_pad // tk, tk, tq), I16)],
        compiler_params=_cparams("parallel", "arbitrary"),
        name="dsa",
    )(pa, ki_all, k_all, vt_all)
    return out[:, :t]


D_R = N_HEADS_R * HEAD_DIM_R
D_SHIFT = 3 * D_R + 2 * D_LORA
N_PAIR = D_R // LANES
N_VEC = 8


def _split2(x):
    hi = x.astype(BF16)
    lo = (x - hi.astype(F32)).astype(BF16)
    return hi, lo


def _mm(a, b, ca, cb):
    return lax.dot_general(a, b, (((ca,), (cb,)), ((), ())), preferred_element_type=F32)


def _rwkv_kernel(sh_ref, sh0_ref, mu_ref, vec_ref, wl_ref, s0_ref, y_ref, sout_ref, st_sc, row_sc, *, bb, c):
    ci = pl.program_id(1)
    n = HEAD_DIM_R
    c2 = 2 * c

    @pl.when(ci == 0)
    def _():
        st_sc[...] = s0_ref[...]
        row_sc[...] = sh0_ref[...]

    head0 = lax.broadcasted_iota(jnp.int32, (c, LANES), 1) < n
    r2 = lax.broadcasted_iota(jnp.int32, (c2, c2), 0)
    q2 = lax.broadcasted_iota(jnp.int32, (c2, c2), 1)
    same_head = (r2 // c) == (q2 // c)
    strict = same_head & (r2 > q2)
    incl = same_head & (r2 >= q2)
    eye2 = jnp.where(r2 == q2, 1.0, 0.0).astype(F32)
    rc = lax.broadcasted_iota(jnp.int32, (c, c), 0)
    cc = lax.broadcasted_iota(jnp.int32, (c, c), 1)
    tri = jnp.where(rc >= cc, 1.0, 0.0).astype(BF16)
    tri3 = jnp.concatenate([tri, tri, tri], axis=1)
    la = lax.broadcasted_iota(jnp.int32, (LANES, LANES), 0)
    lb = lax.broadcasted_iota(jnp.int32, (LANES, LANES), 1)
    seg = jnp.where((la // n) == (lb // n), 1.0, 0.0).astype(BF16)
    seg2 = jnp.concatenate([seg, seg], axis=0)
    first_row = lax.broadcasted_iota(jnp.int32, (c, D_SHIFT), 0) == 0

    def seg_sum(x):
        xh, xl = _split2(x)
        return jnp.dot(jnp.concatenate([xh, xl], axis=1), seg2, preferred_element_type=F32)

    def vec(i, sl=slice(None)):
        return vec_ref[i:i + 1, sl]

    n_double = 0
    while (2 << n_double) < c:
        n_double += 1

    def stack(zz):
        return jnp.concatenate([jnp.where(head0, zz, 0.0), jnp.where(head0, 0.0, zz)], axis=0)

    chains = [(bi, p) for bi in range(bb) for p in range(N_PAIR)]
    states = [st_sc[bi, p] for bi, p in chains]

    prep = []
    for bi in range(bb):
        sh = sh_ref[bi]
        prev = jnp.where(first_row, row_sc[bi], pltpu.roll(sh, 1, 0))
        x = sh + (prev - sh) * mu_ref[...]
        r = x[:, :D_R]
        k = x[:, D_R:2 * D_R]
        v = x[:, 2 * D_R:3 * D_R]
        xl = x[:, 3 * D_R:]
        lora_in = jnp.where(head0, jnp.tanh(xl), xl)
        wa = jnp.dot(lora_in.astype(BF16), wl_ref[...], preferred_element_type=F32)
        w_raw = vec(0) + wa[:, :D_R]
        a = jax.nn.sigmoid(vec(1) + wa[:, D_R:])
        lw = -jnp.exp(-jax.nn.softplus(-w_raw) - 0.5)
        kkv = k * vec(2)
        kk2 = kkv * kkv
        ss = jnp.concatenate([seg_sum(kk2[:, p * LANES:(p + 1) * LANES]) for p in range(N_PAIR)], axis=1)
        kk = kkv / jnp.maximum(jnp.sqrt(ss), 1e-12)
        k2 = k * (1.0 + (a - 1.0) * vec(3))
        l_hi = lw.astype(BF16)
        l_r1 = lw - l_hi.astype(F32)
        l_mid = l_r1.astype(BF16)
        l_lo = (l_r1 - l_mid.astype(F32)).astype(BF16)
        cum = jnp.dot(tri3, jnp.concatenate([l_hi, l_mid, l_lo], axis=0), preferred_element_type=F32)
        p_fwd = jnp.exp(cum)
        p_inv = jnp.exp(-cum)
        prep.append(dict(v=v, p_fwd=p_fwd, at=-kk * jnp.exp(cum - lw), bt=kk * a * p_inv, kt=k2 * p_inv,
                         rt=r * p_fwd, rk_prod=r * k2 * vec(4)))
    for bi in range(bb):
        row_sc[bi] = sh_ref[bi, c - 1:c, :]

    ops = []
    for bi, p in chains:
        sl = slice(p * LANES, (p + 1) * LANES)
        d = prep[bi]
        b_st, k_st, v_st = stack(d["bt"][:, sl]), stack(d["kt"][:, sl]), stack(d["v"][:, sl])
        ar = jnp.concatenate([stack(d["at"][:, sl]), stack(d["rt"][:, sl])], axis=0).astype(BF16)
        bk = jnp.concatenate([b_st, k_st], axis=0).astype(BF16)
        p_last = d["p_fwd"][c - 1:c, sl]
        ops.append(dict(ar=ar, bk=bk, v_b=v_st.astype(BF16), p_last=p_last,
                        b_dec=(b_st * p_last).astype(BF16), k_dec=(k_st * p_last).astype(BF16)))
    grams = [_mm(o["ar"], o["bk"], 1, 1) for o in ops]
    l_ab = [jnp.where(strict, g[:c2, :c2], 0.0) for g in grams]
    l_ak = [jnp.where(strict, g[:c2, c2:], 0.0).astype(BF16) for g in grams]
    m_rb = [jnp.where(incl, g[c2:, :c2], 0.0).astype(BF16) for g in grams]
    m_rk = [jnp.where(incl, g[c2:, c2:], 0.0).astype(BF16) for g in grams]
    inv = [eye2 + l for l in l_ab]
    lp = l_ab
    for _ in range(n_double):
        lpb = [l.astype(BF16) for l in lp]
        lp = [_mm(l, l, 1, 0) for l in lpb]
        inv = [x + _mm(x.astype(BF16), l.astype(BF16), 1, 0) for x, l in zip(inv, lp)]

    s_b = [s.astype(BF16) for s in states]
    z = [_mm(o["ar"][:c2], sb, 1, 1) + _mm(la, o["v_b"], 1, 0) for o, sb, la in zip(ops, s_b, l_ak)]
    u_b = [_mm(x.astype(BF16), zz.astype(BF16), 1, 0).astype(BF16) for x, zz in zip(inv, z)]
    y_st = [_mm(o["ar"][c2:], sb, 1, 1) + _mm(mb, u, 1, 0) + _mm(mk, o["v_b"], 1, 0)
            for o, sb, mb, mk, u in zip(ops, s_b, m_rb, m_rk, u_b)]
    new_states = [s * o["p_last"] + _mm(u, o["b_dec"], 0, 0) + _mm(o["v_b"], o["k_dec"], 0, 0)
                  for s, o, u in zip(states, ops, u_b)]
    for (bi, p), s_new, ys in zip(chains, new_states, y_st):
        sl = slice(p * LANES, (p + 1) * LANES)
        st_sc[bi, p] = s_new
        y = ys[:c] + ys[c:]
        dev = y - seg_sum(y) * (1.0 / n)
        var = seg_sum(dev * dev) * (1.0 / n)
        yn = dev * lax.rsqrt(var + GN_EPS) * vec(5, sl) + vec(6, sl)
        y_ref[bi, :, sl] = yn + seg_sum(prep[bi]["rk_prod"][:, sl]) * prep[bi]["v"][:, sl]

    @pl.when(ci == pl.num_programs(1) - 1)
    def _():
        sout_ref[...] = st_sc[...]


def _rwkv(sh, shift0, mu, w0, w2, a0, a2, k_k, k_a, r_k, ln_w, ln_b, state0, *, bb, c):
    b, t, _ = sh.shape
    n = HEAD_DIM_R
    eye = jnp.eye(2, dtype=F32)
    s0 = jnp.einsum('bphij,hg->bphigj', state0.reshape(b, N_PAIR, 2, n, n), eye).reshape(b, N_PAIR, LANES, LANES)
    vecs = jnp.stack([w0, a0, k_k, k_a, r_k.reshape(D_R), ln_w, ln_b, jnp.zeros_like(w0)])
    zero = jnp.zeros((D_LORA, D_R), F32)
    w_lora = jnp.concatenate([jnp.concatenate([w2, zero], axis=1), jnp.concatenate([zero, a2], axis=1)],
                             axis=0).astype(BF16)
    full = lambda shape: pl.BlockSpec(shape, lambda bi, ci: (0,) * len(shape))
    st_spec = pl.BlockSpec((bb, N_PAIR, LANES, LANES), lambda bi, ci: (bi, 0, 0, 0))
    y, s_out = pl.pallas_call(
        functools.partial(_rwkv_kernel, bb=bb, c=c),
        grid=(b // bb, t // c),
        in_specs=[pl.BlockSpec((bb, c, D_SHIFT), lambda bi, ci: (bi, ci, 0)),
                  pl.BlockSpec((bb, 1, D_SHIFT), lambda bi, ci: (bi, 0, 0)),
                  full((1, D_SHIFT)), full((N_VEC, D_R)), full((2 * D_LORA, 2 * D_R)), st_spec],
        out_specs=[pl.BlockSpec((bb, c, D_R), lambda bi, ci: (bi, ci, 0)), st_spec],
        out_shape=[jax.ShapeDtypeStruct((b, t, D_R), F32),
                   jax.ShapeDtypeStruct((b, N_PAIR, LANES, LANES), F32)],
        scratch_shapes=[pltpu.VMEM((bb, N_PAIR, LANES, LANES), F32), pltpu.VMEM((bb, 1, D_SHIFT), F32)],
        compiler_params=_cparams("parallel", "arbitrary"),
        name="rwkv",
    )(sh, shift0, mu.reshape(1, D_SHIFT), vecs, w_lora, s0)
    s_t = jnp.einsum('bphigj,hg->bphij', s_out.reshape(b, N_PAIR, 2, n, 2, n), eye)
    return y, s_t.reshape(b, N_HEADS_R, n, n)


def _mem_attn_kernel(q_ref, mk_ref, mv_ref, o_ref):
    scale = HEAD_DIM_M ** -0.5
    for h in range(N_HEADS_M):
        sl = slice(h * HEAD_DIM_M, (h + 1) * HEAD_DIM_M)
        q = q_ref[0, :, sl].astype(BF16)
        logits = lax.dot_general(q, mk_ref[0, :, sl], (((1,), (1,)), ((), ())), preferred_element_type=F32) * scale
        e = jnp.exp(logits - jnp.max(logits, axis=1, keepdims=True))
        p = e / jnp.sum(e, axis=1, keepdims=True)
        o_ref[0, :, sl] = jnp.dot(p.astype(BF16), mv_ref[0, :, sl], preferred_element_type=F32)


def _mem_attn(qm, mk, mv, tm):
    b, t, d = qm.shape
    nm = mk.shape[1]
    return pl.pallas_call(
        _mem_attn_kernel,
        grid=(b, t // tm),
        in_specs=[pl.BlockSpec((1, tm, d), lambda bi, i: (bi, i, 0)),
                  pl.BlockSpec((1, nm, d), lambda bi, i: (bi, 0, 0)),
                  pl.BlockSpec((1, nm, d), lambda bi, i: (bi, 0, 0))],
        out_specs=pl.BlockSpec((1, tm, d), lambda bi, i: (bi, i, 0)),
        out_shape=jax.ShapeDtypeStruct((b, t, d), F32),
        compiler_params=_cparams("parallel", "parallel"),
        name="mem_attn",
    )(qm, mk, mv)


def _merge_kernel(oa_ref, yr_ref, om_ref, zg_ref, x_ref, wa_ref, wr_ref, wm_ref, wo_ref, g_ref, xo_ref, ho_ref):
    d_br = oa_ref.shape[1]
    d = x_ref.shape[1]

    def branch(o, z, w_ref):
        return jnp.dot((o * jax.nn.silu(z)).astype(BF16), w_ref[...], preferred_element_type=F32)

    pa = branch(oa_ref[...], zg_ref[:, 0:d_br], wa_ref)
    pr = branch(yr_ref[...], zg_ref[:, d_br:2 * d_br], wr_ref)
    pm = branch(om_ref[...], zg_ref[:, 2 * d_br:3 * d_br], wm_ref)
    g0 = 3 * d_br
    merged = (jax.nn.sigmoid(zg_ref[:, g0:g0 + d]) * pa
              + jax.nn.sigmoid(zg_ref[:, g0 + d:g0 + 2 * d]) * pr
              + jax.nn.sigmoid(zg_ref[:, g0 + 2 * d:g0 + 3 * d]) * pm)
    xn = x_ref[...] + jnp.dot(merged.astype(BF16), wo_ref[...], preferred_element_type=F32)
    xo_ref[...] = xn
    y = xn * lax.rsqrt(jnp.mean(xn * xn, axis=-1, keepdims=True) + RMS_EPS)
    ho_ref[...] = y * g_ref[...]


def _merge(oa, yr, om, zg, x2d, wa, wr, wm, wo, g_next, tm):
    m, d = x2d.shape
    d_br = oa.shape[1]
    tm = min(tm, m)
    row = lambda w: pl.BlockSpec((tm, w), lambda i: (i, 0))
    full = lambda a: pl.BlockSpec(a.shape, lambda i: (0, 0))
    g2 = g_next.reshape(1, d)
    return pl.pallas_call(
        _merge_kernel,
        grid=(m // tm,),
        in_specs=[row(d_br), row(d_br), row(d_br), row(zg.shape[1]), row(d),
                  full(wa), full(wr), full(wm), full(wo), full(g2)],
        out_specs=[row(d), row(d)],
        out_shape=[jax.ShapeDtypeStruct((m, d), F32), jax.ShapeDtypeStruct((m, d), F32)],
        compiler_params=_cparams("parallel"),
        name="merge",
    )(oa, yr, om, zg, x2d, wa, wr, wm, wo, g2)


def _round_up(x, m):
    return (x + m - 1) // m * m


IN_SIZES = (D_A, D_KV, D_KV, N_IDX * D_IDX, D_IDX, N_IDX, D_A, D_SHIFT, D_R, N_HEADS_M * HEAD_DIM_M,
            N_HEADS_M * HEAD_DIM_M)
IN_OFFS = tuple(int(o) for o in np.cumsum((0,) + IN_SIZES))


def _repack_kernel(w_ref, att_ref, sh_ref, qm_ref, zg_ref):
    col = lambda i: w_ref[:, IN_OFFS[i]:IN_OFFS[i + 1]]
    rows = w_ref.shape[0]
    att_ref[:, :W_OFF + N_IDX] = w_ref[:, :IN_OFFS[6]].astype(BF16)
    att_ref[:, W_OFF + N_IDX:] = jnp.zeros((rows, ATT_COLS - W_OFF - N_IDX), BF16)
    sh_ref[...] = col(7).astype(BF16)
    qm_ref[...] = col(9).astype(BF16)
    zg_ref[:, :D_A] = col(6).astype(BF16)
    zg_ref[:, D_A:D_A + D_R] = col(8).astype(BF16)
    zg_ref[:, D_A + D_R:D_A + D_R + IN_SIZES[10]] = col(10).astype(BF16)
    zg_ref[:, D_A + D_R + IN_SIZES[10]:] = w_ref[:, IN_OFFS[11]:].astype(BF16)


def _split_w_in(w_in):
    d_model, d_in = w_in.shape
    n_zg = D_A + D_R + IN_SIZES[10] + d_in - IN_OFFS[11]
    tr = 128
    widths = (ATT_COLS, D_SHIFT, IN_SIZES[9], n_zg)
    return tuple(pl.pallas_call(
        _repack_kernel,
        grid=(d_model // tr,),
        in_specs=[pl.BlockSpec((tr, d_in), lambda i: (i, 0))],
        out_specs=[pl.BlockSpec((tr, w), lambda i: (i, 0)) for w in widths],
        out_shape=[jax.ShapeDtypeStruct((d_model, w), BF16) for w in widths],
        compiler_params=_cparams("parallel"),
        name="repack",
    )(w_in))


def _layer(x2d, h2d, b, t, pos0, past_k, past_v, past_ki, wkv0, shift0, mem_k, mem_v, lw, g_next, cfg):
    (w_att, w_sh, w_qm, w_zg, mu_shift, w0, w2, a0, a2, k_k, k_a, r_k, ln_w, ln_b, w_br_a, w_br_r, w_br_m,
     w_out) = lw
    tm = cfg["tm"]
    d_a = N_HEADS_A * HEAD_DIM_A
    d_kv = N_KV_A * HEAD_DIM_A
    d_r = N_HEADS_R * HEAD_DIM_R
    d_m = N_HEADS_M * HEAD_DIM_M

    tables = _rope_tables(pos0, t)
    pa, kb, vb, ki2 = _attn_proj(h2d, w_att, tables, t, min(tm, t))
    pa = pa.reshape(b, t, ATT_COLS)
    sh = _matmul(h2d, w_sh, tm, w_sh.shape[1]).reshape(b, t, -1)
    qm = _matmul(h2d, w_qm, tm, w_qm.shape[1]).reshape(b, t, d_m)
    zg = _matmul(h2d, w_zg, tm, cfg["tn_zg"])

    k_new = pa[..., K_OFF:K_OFF + D_KV].reshape(b, t, N_KV_A, HEAD_DIM_A)
    v_new = pa[..., V_OFF:V_OFF + D_KV].reshape(b, t, N_KV_A, HEAD_DIM_A)
    ki_new = pa[..., KI_OFF:KI_OFF + D_IDX]
    s_real = pos0 + t
    tq, tk = cfg["tq"], cfg["tk"]
    s_pad = _round_up(s_real, tk)

    def with_past(past, new):
        full = jnp.concatenate([past.reshape(b, pos0, LANES).astype(BF16), new.reshape(b, t, LANES)], axis=1)
        return jnp.pad(full, ((0, 0), (0, s_pad - s_real), (0, 0)))

    vt_all = with_past(past_v, vb).transpose(0, 2, 1)
    o_a = _dsa(pa, with_past(jnp.concatenate([past_ki, past_ki], axis=-1), ki2), with_past(past_k, kb),
               vt_all, pos0=pos0, s_real=s_real, tq=tq, tk=tk).reshape(b * t, d_a)

    y_r, wkv_t = _rwkv(sh, shift0, mu_shift, w0, w2, a0, a2, k_k, k_a, r_k, ln_w, ln_b, wkv0,
                       bb=cfg["bb"], c=CHUNK)
    y_r = y_r.reshape(b * t, d_r)

    o_m = _mem_attn(qm, mem_k, mem_v, min(tm, t)).reshape(b * t, d_m)

    x_new, h_next = _merge(o_a, y_r, o_m, zg, x2d, w_br_a, w_br_r, w_br_m, w_out, g_next, cfg["tm_merge"])
    return x_new, h_next, k_new, v_new, ki_new, wkv_t, sh[:, -1:]


PROMPT_CFG = dict(tm=512, tn_zg=2304, tq=512, tk=512, bb=2, tm_merge=256)
SAMPLE_CFG = dict(tm=512, tn_zg=2304, tq=128, tk=1152, bb=2, tm_merge=256)


def kernel(x_prompt, x_sample, mem_prompt, cache_k, cache_v, cache_kidx, state_wkv, state_shift, cache_mem_k,
           cache_mem_v, norm_g, w_in, mu_shift, w0, w2, a0, a2, k_k, k_a, r_k, ln_x_w, ln_x_b, mem_norm_g,
           w_mem_kv, w_br_a, w_br_r, w_br_m, w_out, final_norm_g):
    depth = w_in.shape[0]
    bp, tp, d = x_prompt.shape
    bs, ts, _ = x_sample.shape
    n_mem = mem_prompt.shape[1]
    d_m = N_HEADS_M * HEAD_DIM_M
    past = cache_k.shape[2]

    xp = x_prompt.reshape(bp * tp, d)
    xs = x_sample.reshape(bs * ts, d)
    hp = _rmsnorm(xp, norm_g[0], F32, 512)
    hs = _rmsnorm(xs, norm_g[0], F32, 512)
    mem2d = mem_prompt.reshape(bp * n_mem, d)

    empty_kv = jnp.zeros((bp, 0, N_KV_A, HEAD_DIM_A), F32)
    empty_ki = jnp.zeros((bp, 0, D_IDX), F32)
    zero_state = jnp.zeros((bp, N_HEADS_R, HEAD_DIM_R, HEAD_DIM_R), F32)
    zero_shift = jnp.zeros((bp, 1, mu_shift.shape[1]), F32)

    outs_p = [[] for _ in range(7)]
    outs_s = [[] for _ in range(5)]
    for l in range(depth):
        g_next = norm_g[l + 1] if l + 1 < depth else final_norm_g
        lw = _split_w_in(w_in[l]) + (mu_shift[l], w0[l], w2[l], a0[l], a2[l], k_k[l], k_a[l], r_k[l], ln_x_w[l],
                                     ln_x_b[l], w_br_a[l].astype(BF16), w_br_r[l].astype(BF16),
                                     w_br_m[l].astype(BF16), w_out[l].astype(BF16))
        mem_h = _rmsnorm(mem2d, mem_norm_g[l], F32, 512)
        mkv = _matmul(mem_h, w_mem_kv[l].astype(BF16), 512, 2 * d_m).reshape(bp, n_mem, 2 * d_m)
        mk, mv = mkv[..., :d_m], mkv[..., d_m:]
        xp, hp, k_n, v_n, ki_n, wkv_n, sh_n = _layer(
            xp, hp, bp, tp, 0, empty_kv, empty_kv, empty_ki, zero_state, zero_shift,
            mk.astype(BF16), mv.astype(BF16), lw, g_next, PROMPT_CFG)
        for lst, val in zip(outs_p, (k_n, v_n, ki_n, wkv_n, sh_n,
                                     mk.reshape(bp, n_mem, N_HEADS_M, HEAD_DIM_M),
                                     mv.reshape(bp, n_mem, N_HEADS_M, HEAD_DIM_M))):
            lst.append(val)
        xs, hs, k_n, v_n, ki_n, wkv_n, sh_n = _layer(
            xs, hs, bs, ts, past, cache_k[l], cache_v[l], cache_kidx[l], state_wkv[l], state_shift[l],
            cache_mem_k[l].reshape(bs, n_mem, d_m).astype(BF16), cache_mem_v[l].reshape(bs, n_mem, d_m).astype(BF16),
            lw, g_next, SAMPLE_CFG)
        for lst, val in zip(outs_s, (k_n, v_n, ki_n, wkv_n, sh_n)):
            lst.append(val)

    y_prompt = hp.reshape(bp, tp, d)
    y_sample = hs.reshape(bs, ts, d)
    return (y_prompt, y_sample) + tuple(jnp.stack(o) for o in outs_p) + tuple(jnp.stack(o) for o in outs_s)
```

```python
import functools

import jax
import jax.numpy as jnp
import numpy as np
from jax import lax
from jax.experimental import pallas as pl
from jax.experimental.pallas import tpu as pltpu

F32 = jnp.float32
BF16 = jnp.bfloat16

CHUNK = 64
ROPE_THETA = 10000.0
RMS_EPS = 1e-6
NEG_INF = -1e30
N_HEADS_A = 8
HEAD_DIM_A = 64
N_KV_A = 2
N_IDX = 4
D_IDX = 64
TOPK_MAX = 256
N_HEADS_R = 8
HEAD_DIM_R = 64
D_LORA = 64
GN_EPS = 64e-5
N_HEADS_M = 4
HEAD_DIM_M = 128
N_BRANCH = 3

LANES = 128
VMEM_LIMIT = 56 * 1024 * 1024

_NEG_BITS = int(np.array(NEG_INF, np.float32).view(np.int32))
NEG_KEY = _NEG_BITS ^ 0x7FFFFFFF
SIGN_FLIP = -(2 ** 31)


def _cparams(*sem):
    return pltpu.CompilerParams(dimension_semantics=sem, vmem_limit_bytes=VMEM_LIMIT)


def _rmsnorm_kernel(x_ref, g_ref, o_ref):
    x = x_ref[...]
    y = x * lax.rsqrt(jnp.mean(x * x, axis=-1, keepdims=True) + RMS_EPS)
    o_ref[...] = (y * g_ref[...]).astype(o_ref.dtype)


def _rmsnorm(x2d, g, out_dtype, tm):
    m, d = x2d.shape
    tm = min(tm, m)
    return pl.pallas_call(
        _rmsnorm_kernel,
        grid=(m // tm,),
        in_specs=[pl.BlockSpec((tm, d), lambda i: (i, 0)), pl.BlockSpec((1, d), lambda i: (0, 0))],
        out_specs=pl.BlockSpec((tm, d), lambda i: (i, 0)),
        out_shape=jax.ShapeDtypeStruct((m, d), out_dtype),
        compiler_params=_cparams("parallel"),
        name="rmsnorm",
    )(x2d, g.reshape(1, d))


def _matmul_kernel(x_ref, w_ref, o_ref):
    o_ref[...] = jnp.dot(x_ref[...].astype(BF16), w_ref[...], preferred_element_type=F32)


def _matmul(x2d, w, tm, tn):
    m, k = x2d.shape
    n = w.shape[1]
    tm = min(tm, m)
    return pl.pallas_call(
        _matmul_kernel,
        grid=(n // tn, m // tm),
        in_specs=[pl.BlockSpec((tm, k), lambda j, i: (i, 0)), pl.BlockSpec((k, tn), lambda j, i: (0, j))],
        out_specs=pl.BlockSpec((tm, tn), lambda j, i: (i, j)),
        out_shape=jax.ShapeDtypeStruct((m, n), F32),
        compiler_params=_cparams("parallel", "parallel"),
        name="matmul",
    )(x2d, w)


D_A = N_HEADS_A * HEAD_DIM_A
D_KV = N_KV_A * HEAD_DIM_A
K_OFF = D_A
V_OFF = K_OFF + D_KV
QI_OFF = V_OFF + D_KV
KI_OFF = QI_OFF + N_IDX * D_IDX
W_OFF = KI_OFF + D_IDX
ATT_COLS = KI_OFF + LANES
_G_K = K_OFF // LANES
_G_V = V_OFF // LANES
_G_KI = KI_OFF // LANES


def _attn_proj_kernel(h_ref, w_ref, cos_ref, sa_ref, sb_ref, o_ref, kb_ref, vb_ref, ki2_ref):
    p = jnp.dot(h_ref[...].astype(BF16), w_ref[...], preferred_element_type=F32)
    cos, sa, sb = cos_ref[...], sa_ref[...], sb_ref[...]
    lane = lax.broadcasted_iota(jnp.int32, cos.shape, 1)
    for g in range(ATT_COLS // LANES):
        x = p[:, g * LANES:(g + 1) * LANES]
        if g == _G_V:
            o_ref[:, g * LANES:(g + 1) * LANES] = x
            vb_ref[...] = x.astype(BF16)
            continue
        y = x * cos + pltpu.roll(x, LANES - 32, 1) * sa + pltpu.roll(x, 32, 1) * sb
        if g == _G_KI:
            y = jnp.where(lane < D_IDX, y, x)
            ki2_ref[...] = jnp.where(lane < D_IDX, y, pltpu.roll(y, D_IDX, 1)).astype(BF16)
        if g == _G_K:
            kb_ref[...] = y.astype(BF16)
        o_ref[:, g * LANES:(g + 1) * LANES] = y


def _rope_tables(pos0, t):
    d = HEAD_DIM_A
    inv = ROPE_THETA ** (-jnp.arange(0, d, 2, dtype=F32) / d)
    ang = (pos0 + jnp.arange(t)).astype(F32)[:, None] * inv[None, :]
    cos, sin = jnp.cos(ang), jnp.sin(ang)
    zero = jnp.zeros_like(sin)
    cos_f = jnp.concatenate([cos] * 4, axis=-1)
    sin_a = jnp.concatenate([-sin, zero] * 2, axis=-1)
    sin_b = jnp.concatenate([zero, sin] * 2, axis=-1)
    return cos_f, sin_a, sin_b


def _attn_proj(h2d, w_att, tables, t, tm):
    m, k = h2d.shape
    nt = t // tm
    tab_spec = pl.BlockSpec((tm, LANES), lambda i: (i % nt, 0))
    lane_spec = pl.BlockSpec((tm, LANES), lambda i: (i, 0))
    lane_shape = jax.ShapeDtypeStruct((m, LANES), BF16)
    return pl.pallas_call(
        _attn_proj_kernel,
        grid=(m // tm,),
        in_specs=[pl.BlockSpec((tm, k), lambda i: (i, 0)), pl.BlockSpec((k, ATT_COLS), lambda i: (0, 0)),
                  tab_spec, tab_spec, tab_spec],
        out_specs=[pl.BlockSpec((tm, ATT_COLS), lambda i: (i, 0)), lane_spec, lane_spec, lane_spec],
        out_shape=[jax.ShapeDtypeStruct((m, ATT_COLS), F32), lane_shape, lane_shape, lane_shape],
        compiler_params=_cparams("parallel"),
        name="attn_proj",
    )(h2d, w_att, *tables)


I16 = jnp.int16
I16_MIN = -(2 ** 15)
I16_MAX = 2 ** 15 - 1
I16_ROWS = 16
I32_ROWS = 8
LOG2E = 1.4426950408889634


def _tree(parts, op):
    while len(parts) > 1:
        parts = [op(parts[i], parts[i + 1]) if i + 1 < len(parts) else parts[i] for i in range(0, len(parts), 2)]
    return parts[0]


def _fold_rows(x, rows, op):
    return _tree([x[g * rows:(g + 1) * rows] for g in range(x.shape[0] // rows)], op)


def _dsa_kernel(pa_ref, ki_ref, k_ref, vt_ref, o_ref, key_sc, dig_sc, *, tq, tk, pos0, t_real, s_real, topk):
    qb = pl.program_id(1)
    q0 = pos0 + qb * tq
    n_adm = jnp.minimum(q0 + tq, s_real)
    n_kt = (n_adm + tk - 1) // tk
    hd = HEAD_DIM_A
    group = N_HEADS_A // N_KV_A
    cols = group * tq

    q_pos = q0 + lax.broadcasted_iota(jnp.int32, (1, tq), 1)
    key_limit = jnp.minimum((lax.shift_right_arithmetic(q_pos, CHUNK.bit_length() - 1) + 1) * CHUNK, s_real)
    key_iota = lax.broadcasted_iota(jnp.int32, (tk, tq), 0)
    low_half = lax.broadcasted_iota(jnp.int32, (tq, LANES), 1) < hd
    halves = (low_half, jnp.logical_not(low_half))
    real_q = qb * tq + lax.broadcasted_iota(jnp.int32, (1, tq), 1) < t_real

    pa = pa_ref[0]
    qi = jnp.concatenate(
        [jnp.where(halves[h % 2], pa[:, QI_OFF + (h // 2) * LANES:QI_OFF + (h // 2 + 1) * LANES] * (D_IDX ** -0.5), 0.0)
         for h in range(N_IDX)], axis=0).astype(BF16)
    w_t = pa[:, KI_OFF:KI_OFF + LANES].T * (N_IDX ** -0.5)
    w_rows = [w_t[D_IDX + h:D_IDX + h + 1, :] for h in range(N_IDX)]

    def score_body(kt, carry):
        start = pl.multiple_of(kt * tk, tk)
        s = lax.dot_general(ki_ref[0, pl.ds(start, tk), :], qi, (((1,), (1,)), ((), ())),
                            preferred_element_type=F32)
        score = jnp.zeros((tk, tq), F32)
        for h in range(N_IDX):
            score = score + jnp.maximum(s[:, h * tq:(h + 1) * tq], 0.0) * w_rows[h]
        score = score + 0.0
        score = jnp.where(start + key_iota < key_limit, score, NEG_INF)
        bits = pltpu.bitcast(score, jnp.int32)
        key = jnp.where(bits < 0, bits ^ 0x7FFFFFFF, bits)
        key_sc[kt] = key
        dig_sc[kt] = lax.shift_right_arithmetic(key, 16).astype(I16)
        return carry

    lax.fori_loop(0, n_kt, score_body, 0)

    def count(pred_fn):
        def body(kt, acc):
            hit = jnp.where(pred_fn(key_sc[kt], kt), 1, 0)
            return acc + _fold_rows(hit, I32_ROWS, jnp.add)

        acc = lax.fori_loop(0, n_kt, body, jnp.zeros((I32_ROWS, tq), jnp.int32))
        return jnp.sum(acc, axis=0, keepdims=True)

    def count16(ref, c_of_tile, cmp):
        def body(kt, acc):
            hit = jnp.where(cmp(ref[kt], c_of_tile(kt).astype(I16)), jnp.ones((), I16), jnp.zeros((), I16))
            return acc + _fold_rows(hit, I16_ROWS, jnp.add)

        acc = lax.fori_loop(0, n_kt, body, jnp.zeros((I16_ROWS, tq), I16))
        return jnp.sum(acc.astype(jnp.int32), axis=0, keepdims=True)

    def kth_digit(ref, kk):
        def body(i, prefix):
            cand = prefix | lax.shift_left(jnp.int32(1), 15 - i)
            cnt = count16(ref, lambda kt: cand + I16_MIN, lax.ge)
            return jnp.where(cnt >= kk, cand, prefix)

        return lax.fori_loop(0, 16, body, jnp.zeros((1, tq), jnp.int32))

    hi_thr = kth_digit(dig_sc, topk) + I16_MIN
    above = jnp.where(hi_thr < I16_MAX, count16(dig_sc, lambda kt: jnp.minimum(hi_thr + 1, I16_MAX), lax.ge), 0)

    def lo_body(kt, carry):
        key = key_sc[kt]
        in_bucket = lax.shift_right_arithmetic(key, 16) == hi_thr
        dig_sc[kt] = jnp.where(in_bucket, (key & 0xFFFF) + I16_MIN, I16_MIN).astype(I16)
        return carry

    lax.fori_loop(0, n_kt, lo_body, 0)
    lo_thr = kth_digit(dig_sc, topk - above)
    thr = jnp.maximum(lax.shift_left(hi_thr, 16) | lo_thr, NEG_KEY)
    cnt_gt = count(lambda key, kt: key > thr)
    cnt_ge = count(lambda key, kt: key >= thr)
    need = topk - cnt_gt
    real_thr = thr != NEG_KEY
    has_excess = jnp.max(jnp.where(real_thr & real_q & (cnt_ge > topk), 1, 0)) > 0

    n_idx_bits = max(1, int(np.ceil(np.log2(key_sc.shape[0] * tk))))

    def tie_search():
        def mark_body(kt, carry):
            dig_sc[kt] = jnp.where(key_sc[kt] == thr, key_iota, I16_MAX).astype(I16)
            return carry

        lax.fori_loop(0, n_kt, mark_body, 0)

        def jbody(i, jq):
            cand = jq | lax.shift_left(jnp.int32(1), n_idx_bits - 1 - i)
            cnt = count16(dig_sc, lambda kt: jnp.clip(cand - kt * tk, I16_MIN, I16_MAX), lax.lt)
            return jnp.where(cnt < need, cand, jq)

        return lax.fori_loop(0, n_idx_bits, jbody, jnp.zeros((1, tq), jnp.int32))

    j_cut = lax.cond(has_excess, tie_search, lambda: jnp.full((1, tq), 2 ** 30, jnp.int32))
    j_cut = jnp.where(real_thr, j_cut, -1)

    qs = []
    for j in range(N_KV_A):
        parts = []
        for g in range(group):
            h = j * group + g
            grp = pa[:, (h // 2) * LANES:(h // 2 + 1) * LANES] * (hd ** -0.5)
            if h % 2 != j:
                grp = pltpu.roll(grp, hd, 1)
            parts.append(jnp.where(halves[j], grp, 0.0))
        qs.append(jnp.concatenate(parts, axis=0).astype(BF16))

    def attn_body(kt, carry):
        start = pl.multiple_of(kt * tk, tk)
        key = key_sc[kt]
        sel = (key > thr) | ((key == thr) & (start + key_iota <= j_cut))
        bias = jnp.where(sel, 0.0, NEG_INF).astype(F32)
        bias = jnp.concatenate([bias] * group, axis=1)
        k_tile = k_ref[0, pl.ds(start, tk), :]
        heads = range(N_KV_A)
        logits = [lax.dot_general(k_tile, qs[j], (((1,), (1,)), ((), ())), preferred_element_type=F32) * LOG2E
                  + bias for j in heads]
        m_new = [jnp.maximum(carry[j][0], jnp.max(_fold_rows(logits[j], I32_ROWS, jnp.maximum), axis=0,
                                                   keepdims=True)) for j in heads]
        alpha = [jnp.exp2(carry[j][0] - m_new[j]) for j in heads]
        p = [jnp.exp2(logits[j] - m_new[j]) for j in heads]
        l_new = [carry[j][1] * alpha[j] + jnp.sum(_fold_rows(p[j], I32_ROWS, jnp.add), axis=0, keepdims=True)
                 for j in heads]
        acc = [carry[j][2] * alpha[j]
               + jnp.dot(vt_ref[0, j * hd:(j + 1) * hd, pl.ds(start, tk)], p[j].astype(BF16),
                         preferred_element_type=F32) for j in heads]
        return tuple((m_new[j], l_new[j], acc[j]) for j in heads)

    init = tuple((jnp.full((1, cols), 0.1 * NEG_INF, F32), jnp.zeros((1, cols), F32),
                  jnp.zeros((hd, cols), F32)) for _ in range(N_KV_A))
    res = lax.fori_loop(0, n_kt, attn_body, init)
    outs = [acc / l_i for _, l_i, acc in res]
    for grp_i in range(N_HEADS_A // 2):
        pair = []
        for h in (2 * grp_i, 2 * grp_i + 1):
            j, g = h // group, h % group
            pair.append(outs[j][:, g * tq:(g + 1) * tq])
        o_ref[0, :, grp_i * LANES:(grp_i + 1) * LANES] = jnp.concatenate(pair, axis=0).T


def _dsa(pa, ki_all, k_all, vt_all, *, pos0, s_real, tq, tk, kv_buffers):
    b, t, _ = pa.shape
    assert tk < I16_MAX
    t_pad = _round_up(t, tq)
    pa = jnp.pad(pa, ((0, 0), (0, t_pad - t), (0, 0)))
    s_pad = ki_all.shape[1]
    topk = min(TOPK_MAX, s_real // 4)
    kern = functools.partial(_dsa_kernel, tq=tq, tk=tk, pos0=pos0, t_real=t, s_real=s_real, topk=topk)
    kv_mode = pl.Buffered(kv_buffers)
    kv_spec = pl.BlockSpec((1, s_pad, LANES), lambda bi, qi: (bi, 0, 0), pipeline_mode=kv_mode)
    out = pl.pallas_call(
        kern,
        grid=(b, t_pad // tq),
        in_specs=[pl.BlockSpec((1, tq, ATT_COLS), lambda bi, qi: (bi, qi, 0)), kv_spec, kv_spec,
                  pl.BlockSpec((1, LANES, s_pad), lambda bi, qi: (bi, 0, 0), pipeline_mode=kv_mode)],
        out_specs=pl.BlockSpec((1, tq, D_A), lambda bi, qi: (bi, qi, 0)),
        out_shape=jax.ShapeDtypeStruct((b, t_pad, D_A), F32),
        scratch_shapes=[pltpu.VMEM((s_pad // tk, tk, tq), jnp.int32), pltpu.VMEM((s_pad // tk, tk, tq), I16)],
        compiler_params=_cparams("parallel", "arbitrary"),
        name="dsa",
    )(pa, ki_all, k_all, vt_all)
    return out[:, :t]


D_R = N_HEADS_R * HEAD_DIM_R
D_SHIFT = 3 * D_R + 2 * D_LORA
N_PAIR = D_R // LANES
N_VEC = 8


def _split2(x):
    hi = x.astype(BF16)
    lo = (x - hi.astype(F32)).astype(BF16)
    return hi, lo


def _mm(a, b, ca, cb):
    return lax.dot_general(a, b, (((ca,), (cb,)), ((), ())), preferred_element_type=F32)


def _rwkv_kernel(sh_ref, sh0_ref, mu_ref, vec_ref, wl_ref, s0_ref, y_ref, sout_ref, st_sc, row_sc, *, bb, c):
    ci = pl.program_id(1)
    n = HEAD_DIM_R
    c2 = 2 * c

    @pl.when(ci == 0)
    def _():
        st_sc[...] = s0_ref[...]
        row_sc[...] = sh0_ref[...]

    head0 = lax.broadcasted_iota(jnp.int32, (c, LANES), 1) < n
    r2 = lax.broadcasted_iota(jnp.int32, (c2, c2), 0)
    q2 = lax.broadcasted_iota(jnp.int32, (c2, c2), 1)
    same_head = (r2 // c) == (q2 // c)
    strict = same_head & (r2 > q2)
    incl = same_head & (r2 >= q2)
    eye2 = jnp.where(r2 == q2, 1.0, 0.0).astype(F32)
    rc = lax.broadcasted_iota(jnp.int32, (c, c), 0)
    cc = lax.broadcasted_iota(jnp.int32, (c, c), 1)
    tri = jnp.where(rc >= cc, 1.0, 0.0).astype(BF16)
    tri3 = jnp.concatenate([tri, tri, tri], axis=1)
    la = lax.broadcasted_iota(jnp.int32, (LANES, LANES), 0)
    lb = lax.broadcasted_iota(jnp.int32, (LANES, LANES), 1)
    seg = jnp.where((la // n) == (lb // n), 1.0, 0.0).astype(BF16)
    seg2 = jnp.concatenate([seg, seg], axis=0)
    first_row = lax.broadcasted_iota(jnp.int32, (c, D_SHIFT), 0) == 0

    def seg_sum(x):
        xh, xl = _split2(x)
        return jnp.dot(jnp.concatenate([xh, xl], axis=1), seg2, preferred_element_type=F32)

    def vec(i, sl=slice(None)):
        return vec_ref[i:i + 1, sl]

    n_double = 0
    while (2 << n_double) < c:
        n_double += 1

    def stack(zz):
        return jnp.concatenate([jnp.where(head0, zz, 0.0), jnp.where(head0, 0.0, zz)], axis=0)

    chains = [(bi, p) for bi in range(bb) for p in range(N_PAIR)]
    states = [st_sc[bi, p] for bi, p in chains]

    prep = []
    for bi in range(bb):
        sh = sh_ref[bi]
        prev = jnp.where(first_row, row_sc[bi], pltpu.roll(sh, 1, 0))
        x = sh + (prev - sh) * mu_ref[...]
        r = x[:, :D_R]
        k = x[:, D_R:2 * D_R]
        v = x[:, 2 * D_R:3 * D_R]
        xl = x[:, 3 * D_R:]
        lora_in = jnp.where(head0, jnp.tanh(xl), xl)
        wa = jnp.dot(lora_in.astype(BF16), wl_ref[...], preferred_element_type=F32)
        w_raw = vec(0) + wa[:, :D_R]
        a = jax.nn.sigmoid(vec(1) + wa[:, D_R:])
        lw = -jnp.exp(-jax.nn.softplus(-w_raw) - 0.5)
        kkv = k * vec(2)
        kk2 = kkv * kkv
        ss = jnp.concatenate([seg_sum(kk2[:, p * LANES:(p + 1) * LANES]) for p in range(N_PAIR)], axis=1)
        kk = kkv / jnp.maximum(jnp.sqrt(ss), 1e-12)
        k2 = k * (1.0 + (a - 1.0) * vec(3))
        l_hi = lw.astype(BF16)
        l_r1 = lw - l_hi.astype(F32)
        l_mid = l_r1.astype(BF16)
        l_lo = (l_r1 - l_mid.astype(F32)).astype(BF16)
        cum = jnp.dot(tri3, jnp.concatenate([l_hi, l_mid, l_lo], axis=0), preferred_element_type=F32)
        p_fwd = jnp.exp(cum)
        p_inv = jnp.exp(-cum)
        prep.append(dict(v=v, p_fwd=p_fwd, at=-kk * jnp.exp(cum - lw), bt=kk * a * p_inv, kt=k2 * p_inv,
                         rt=r * p_fwd, rk_prod=r * k2 * vec(4)))
    for bi in range(bb):
        row_sc[bi] = sh_ref[bi, c - 1:c, :]

    ops = []
    for bi, p in chains:
        sl = slice(p * LANES, (p + 1) * LANES)
        d = prep[bi]
        b_st, k_st, v_st = stack(d["bt"][:, sl]), stack(d["kt"][:, sl]), stack(d["v"][:, sl])
        ar = jnp.concatenate([stack(d["at"][:, sl]), stack(d["rt"][:, sl])], axis=0).astype(BF16)
        bk = jnp.concatenate([b_st, k_st], axis=0).astype(BF16)
        p_last = d["p_fwd"][c - 1:c, sl]
        ops.append(dict(ar=ar, bk=bk, v_b=v_st.astype(BF16), p_last=p_last,
                        b_dec=(b_st * p_last).astype(BF16), k_dec=(k_st * p_last).astype(BF16)))
    grams = [_mm(o["ar"], o["bk"], 1, 1) for o in ops]
    l_ab = [jnp.where(strict, g[:c2, :c2], 0.0) for g in grams]
    l_ak = [jnp.where(strict, g[:c2, c2:], 0.0).astype(BF16) for g in grams]
    m_rb = [jnp.where(incl, g[c2:, :c2], 0.0).astype(BF16) for g in grams]
    m_rk = [jnp.where(incl, g[c2:, c2:], 0.0).astype(BF16) for g in grams]
    inv = [eye2 + l for l in l_ab]
    lp = l_ab
    for _ in range(n_double):
        lpb = [l.astype(BF16) for l in lp]
        lp = [_mm(l, l, 1, 0) for l in lpb]
        inv = [x + _mm(x.astype(BF16), l.astype(BF16), 1, 0) for x, l in zip(inv, lp)]

    s_b = [s.astype(BF16) for s in states]
    z = [_mm(o["ar"][:c2], sb, 1, 1) + _mm(la, o["v_b"], 1, 0) for o, sb, la in zip(ops, s_b, l_ak)]
    u_b = [_mm(x.astype(BF16), zz.astype(BF16), 1, 0).astype(BF16) for x, zz in zip(inv, z)]
    y_st = [_mm(o["ar"][c2:], sb, 1, 1) + _mm(mb, u, 1, 0) + _mm(mk, o["v_b"], 1, 0)
            for o, sb, mb, mk, u in zip(ops, s_b, m_rb, m_rk, u_b)]
    new_states = [s * o["p_last"] + _mm(u, o["b_dec"], 0, 0) + _mm(o["v_b"], o["k_dec"], 0, 0)
                  for s, o, u in zip(states, ops, u_b)]
    for (bi, p), s_new, ys in zip(chains, new_states, y_st):
        sl = slice(p * LANES, (p + 1) * LANES)
        st_sc[bi, p] = s_new
        y = ys[:c] + ys[c:]
        dev = y - seg_sum(y) * (1.0 / n)
        var = seg_sum(dev * dev) * (1.0 / n)
        yn = dev * lax.rsqrt(var + GN_EPS) * vec(5, sl) + vec(6, sl)
        y_ref[bi, :, sl] = yn + seg_sum(prep[bi]["rk_prod"][:, sl]) * prep[bi]["v"][:, sl]

    @pl.when(ci == pl.num_programs(1) - 1)
    def _():
        sout_ref[...] = st_sc[...]


def _rwkv(sh, shift0, mu, w0, w2, a0, a2, k_k, k_a, r_k, ln_w, ln_b, state0, *, bb, c):
    b, t, _ = sh.shape
    n = HEAD_DIM_R
    eye = jnp.eye(2, dtype=F32)
    s0 = jnp.einsum('bphij,hg->bphigj', state0.reshape(b, N_PAIR, 2, n, n), eye).reshape(b, N_PAIR, LANES, LANES)
    vecs = jnp.stack([w0, a0, k_k, k_a, r_k.reshape(D_R), ln_w, ln_b, jnp.zeros_like(w0)])
    zero = jnp.zeros((D_LORA, D_R), F32)
    w_lora = jnp.concatenate([jnp.concatenate([w2, zero], axis=1), jnp.concatenate([zero, a2], axis=1)],
                             axis=0).astype(BF16)
    full = lambda shape: pl.BlockSpec(shape, lambda bi, ci: (0,) * len(shape))
    st_spec = pl.BlockSpec((bb, N_PAIR, LANES, LANES), lambda bi, ci: (bi, 0, 0, 0))
    y, s_out = pl.pallas_call(
        functools.partial(_rwkv_kernel, bb=bb, c=c),
        grid=(b // bb, t // c),
        in_specs=[pl.BlockSpec((bb, c, D_SHIFT), lambda bi, ci: (bi, ci, 0)),
                  pl.BlockSpec((bb, 1, D_SHIFT), lambda bi, ci: (bi, 0, 0)),
                  full((1, D_SHIFT)), full((N_VEC, D_R)), full((2 * D_LORA, 2 * D_R)), st_spec],
        out_specs=[pl.BlockSpec((bb, c, D_R), lambda bi, ci: (bi, ci, 0)), st_spec],
        out_shape=[jax.ShapeDtypeStruct((b, t, D_R), F32),
                   jax.ShapeDtypeStruct((b, N_PAIR, LANES, LANES), F32)],
        scratch_shapes=[pltpu.VMEM((bb, N_PAIR, LANES, LANES), F32), pltpu.VMEM((bb, 1, D_SHIFT), F32)],
        compiler_params=_cparams("parallel", "arbitrary"),
        name="rwkv",
    )(sh, shift0, mu.reshape(1, D_SHIFT), vecs, w_lora, s0)
    s_t = jnp.einsum('bphigj,hg->bphij', s_out.reshape(b, N_PAIR, 2, n, 2, n), eye)
    return y, s_t.reshape(b, N_HEADS_R, n, n)


def _mem_attn_kernel(q_ref, mk_ref, mv_ref, o_ref):
    scale = HEAD_DIM_M ** -0.5
    for h in range(N_HEADS_M):
        sl = slice(h * HEAD_DIM_M, (h + 1) * HEAD_DIM_M)
        q = q_ref[0, :, sl].astype(BF16)
        logits = lax.dot_general(q, mk_ref[0, :, sl], (((1,), (1,)), ((), ())), preferred_element_type=F32) * scale
        e = jnp.exp(logits - jnp.max(logits, axis=1, keepdims=True))
        p = e / jnp.sum(e, axis=1, keepdims=True)
        o_ref[0, :, sl] = jnp.dot(p.astype(BF16), mv_ref[0, :, sl], preferred_element_type=F32)


def _mem_attn(qm, mk, mv, tm):
    b, t, d = qm.shape
    nm = mk.shape[1]
    return pl.pallas_call(
        _mem_attn_kernel,
        grid=(b, t // tm),
        in_specs=[pl.BlockSpec((1, tm, d), lambda bi, i: (bi, i, 0)),
                  pl.BlockSpec((1, nm, d), lambda bi, i: (bi, 0, 0)),
                  pl.BlockSpec((1, nm, d), lambda bi, i: (bi, 0, 0))],
        out_specs=pl.BlockSpec((1, tm, d), lambda bi, i: (bi, i, 0)),
        out_shape=jax.ShapeDtypeStruct((b, t, d), F32),
        compiler_params=_cparams("parallel", "parallel"),
        name="mem_attn",
    )(qm, mk, mv)


def _merge_kernel(oa_ref, yr_ref, om_ref, zg_ref, x_ref, wa_ref, wr_ref, wm_ref, wo_ref, g_ref, xo_ref, ho_ref):
    d_br = oa_ref.shape[1]
    d = x_ref.shape[1]

    def branch(o, z, w_ref):
        return jnp.dot((o * jax.nn.silu(z)).astype(BF16), w_ref[...], preferred_element_type=F32)

    pa = branch(oa_ref[...], zg_ref[:, 0:d_br], wa_ref)
    pr = branch(yr_ref[...], zg_ref[:, d_br:2 * d_br], wr_ref)
    pm = branch(om_ref[...], zg_ref[:, 2 * d_br:3 * d_br], wm_ref)
    g0 = 3 * d_br
    merged = (jax.nn.sigmoid(zg_ref[:, g0:g0 + d]) * pa
              + jax.nn.sigmoid(zg_ref[:, g0 + d:g0 + 2 * d]) * pr
              + jax.nn.sigmoid(zg_ref[:, g0 + 2 * d:g0 + 3 * d]) * pm)
    xn = x_ref[...] + jnp.dot(merged.astype(BF16), wo_ref[...], preferred_element_type=F32)
    xo_ref[...] = xn
    y = xn * lax.rsqrt(jnp.mean(xn * xn, axis=-1, keepdims=True) + RMS_EPS)
    ho_ref[...] = y * g_ref[...]


def _merge(oa, yr, om, zg, x2d, wa, wr, wm, wo, g_next, tm):
    m, d = x2d.shape
    d_br = oa.shape[1]
    tm = min(tm, m)
    row = lambda w: pl.BlockSpec((tm, w), lambda i: (i, 0))
    full = lambda a: pl.BlockSpec(a.shape, lambda i: (0, 0))
    g2 = g_next.reshape(1, d)
    return pl.pallas_call(
        _merge_kernel,
        grid=(m // tm,),
        in_specs=[row(d_br), row(d_br), row(d_br), row(zg.shape[1]), row(d),
                  full(wa), full(wr), full(wm), full(wo), full(g2)],
        out_specs=[row(d), row(d)],
        out_shape=[jax.ShapeDtypeStruct((m, d), F32), jax.ShapeDtypeStruct((m, d), F32)],
        compiler_params=_cparams("parallel"),
        name="merge",
    )(oa, yr, om, zg, x2d, wa, wr, wm, wo, g2)


def _round_up(x, m):
    return (x + m - 1) // m * m


IN_SIZES = (D_A, D_KV, D_KV, N_IDX * D_IDX, D_IDX, N_IDX, D_A, D_SHIFT, D_R, N_HEADS_M * HEAD_DIM_M,
            N_HEADS_M * HEAD_DIM_M)
IN_OFFS = tuple(int(o) for o in np.cumsum((0,) + IN_SIZES))


def _repack_kernel(w_ref, att_ref, sh_ref, qm_ref, zg_ref):
    col = lambda i: w_ref[:, IN_OFFS[i]:IN_OFFS[i + 1]]
    rows = w_ref.shape[0]
    att_ref[:, :W_OFF + N_IDX] = w_ref[:, :IN_OFFS[6]].astype(BF16)
    att_ref[:, W_OFF + N_IDX:] = jnp.zeros((rows, ATT_COLS - W_OFF - N_IDX), BF16)
    sh_ref[...] = col(7).astype(BF16)
    qm_ref[...] = col(9).astype(BF16)
    zg_ref[:, :D_A] = col(6).astype(BF16)
    zg_ref[:, D_A:D_A + D_R] = col(8).astype(BF16)
    zg_ref[:, D_A + D_R:D_A + D_R + IN_SIZES[10]] = col(10).astype(BF16)
    zg_ref[:, D_A + D_R + IN_SIZES[10]:] = w_ref[:, IN_OFFS[11]:].astype(BF16)


def _split_w_in(w_in):
    d_model, d_in = w_in.shape
    n_zg = D_A + D_R + IN_SIZES[10] + d_in - IN_OFFS[11]
    tr = 128
    widths = (ATT_COLS, D_SHIFT, IN_SIZES[9], n_zg)
    return tuple(pl.pallas_call(
        _repack_kernel,
        grid=(d_model // tr,),
        in_specs=[pl.BlockSpec((tr, d_in), lambda i: (i, 0))],
        out_specs=[pl.BlockSpec((tr, w), lambda i: (i, 0)) for w in widths],
        out_shape=[jax.ShapeDtypeStruct((d_model, w), BF16) for w in widths],
        compiler_params=_cparams("parallel"),
        name="repack",
    )(w_in))


def _layer(x2d, h2d, b, t, pos0, past_k, past_v, past_ki, wkv0, shift0, mem_k, mem_v, lw, g_next, cfg):
    (w_att, w_sh, w_qm, w_zg, mu_shift, w0, w2, a0, a2, k_k, k_a, r_k, ln_w, ln_b, w_br_a, w_br_r, w_br_m,
     w_out) = lw
    tm = cfg["tm"]
    d_a = N_HEADS_A * HEAD_DIM_A
    d_kv = N_KV_A * HEAD_DIM_A
    d_r = N_HEADS_R * HEAD_DIM_R
    d_m = N_HEADS_M * HEAD_DIM_M

    tables = _rope_tables(pos0, t)
    pa, kb, vb, ki2 = _attn_proj(h2d, w_att, tables, t, min(tm, t))
    pa = pa.reshape(b, t, ATT_COLS)
    sh = _matmul(h2d, w_sh, tm, w_sh.shape[1]).reshape(b, t, -1)
    qm = _matmul(h2d, w_qm, tm, w_qm.shape[1]).reshape(b, t, d_m)
    zg = _matmul(h2d, w_zg, tm, cfg["tn_zg"])

    k_new = pa[..., K_OFF:K_OFF + D_KV].reshape(b, t, N_KV_A, HEAD_DIM_A)
    v_new = pa[..., V_OFF:V_OFF + D_KV].reshape(b, t, N_KV_A, HEAD_DIM_A)
    ki_new = pa[..., KI_OFF:KI_OFF + D_IDX]
    s_real = pos0 + t
    tq, tk = cfg["tq"], cfg["tk"]
    s_pad = _round_up(s_real, tk)

    def with_past(past, new):
        full = jnp.concatenate([past.reshape(b, pos0, LANES).astype(BF16), new.reshape(b, t, LANES)], axis=1)
        return jnp.pad(full, ((0, 0), (0, s_pad - s_real), (0, 0)))

    vt_all = with_past(past_v, vb).transpose(0, 2, 1)
    o_a = _dsa(pa, with_past(jnp.concatenate([past_ki, past_ki], axis=-1), ki2), with_past(past_k, kb),
               vt_all, pos0=pos0, s_real=s_real, tq=tq, tk=tk, kv_buffers=cfg["kv_buffers"]).reshape(b * t, d_a)

    y_r, wkv_t = _rwkv(sh, shift0, mu_shift, w0, w2, a0, a2, k_k, k_a, r_k, ln_w, ln_b, wkv0,
                       bb=cfg["bb"], c=CHUNK)
    y_r = y_r.reshape(b * t, d_r)

    o_m = _mem_attn(qm, mem_k, mem_v, min(tm, t)).reshape(b * t, d_m)

    x_new, h_next = _merge(o_a, y_r, o_m, zg, x2d, w_br_a, w_br_r, w_br_m, w_out, g_next, cfg["tm_merge"])
    return x_new, h_next, k_new, v_new, ki_new, wkv_t, sh[:, -1:]


PROMPT_CFG = dict(tm=512, tn_zg=2304, tq=512, tk=512, kv_buffers=1, bb=2, tm_merge=512)
SAMPLE_CFG = dict(tm=512, tn_zg=2304, tq=128, tk=2176, kv_buffers=2, bb=2, tm_merge=512)


def kernel(x_prompt, x_sample, mem_prompt, cache_k, cache_v, cache_kidx, state_wkv, state_shift, cache_mem_k,
           cache_mem_v, norm_g, w_in, mu_shift, w0, w2, a0, a2, k_k, k_a, r_k, ln_x_w, ln_x_b, mem_norm_g,
           w_mem_kv, w_br_a, w_br_r, w_br_m, w_out, final_norm_g):
    depth = w_in.shape[0]
    bp, tp, d = x_prompt.shape
    bs, ts, _ = x_sample.shape
    n_mem = mem_prompt.shape[1]
    d_m = N_HEADS_M * HEAD_DIM_M
    past = cache_k.shape[2]

    xp = x_prompt.reshape(bp * tp, d)
    xs = x_sample.reshape(bs * ts, d)
    hp = _rmsnorm(xp, norm_g[0], F32, 512)
    hs = _rmsnorm(xs, norm_g[0], F32, 512)
    mem2d = mem_prompt.reshape(bp * n_mem, d)

    empty_kv = jnp.zeros((bp, 0, N_KV_A, HEAD_DIM_A), F32)
    empty_ki = jnp.zeros((bp, 0, D_IDX), F32)
    zero_state = jnp.zeros((bp, N_HEADS_R, HEAD_DIM_R, HEAD_DIM_R), F32)
    zero_shift = jnp.zeros((bp, 1, mu_shift.shape[1]), F32)

    outs_p = [[] for _ in range(7)]
    outs_s = [[] for _ in range(5)]
    for l in range(depth):
        g_next = norm_g[l + 1] if l + 1 < depth else final_norm_g
        lw = _split_w_in(w_in[l]) + (mu_shift[l], w0[l], w2[l], a0[l], a2[l], k_k[l], k_a[l], r_k[l], ln_x_w[l],
                                     ln_x_b[l], w_br_a[l].astype(BF16), w_br_r[l].astype(BF16),
                                     w_br_m[l].astype(BF16), w_out[l].astype(BF16))
        mem_h = _rmsnorm(mem2d, mem_norm_g[l], F32, 512)
        mkv = _matmul(mem_h, w_mem_kv[l].astype(BF16), 512, 2 * d_m).reshape(bp, n_mem, 2 * d_m)
        mk, mv = mkv[..., :d_m], mkv[..., d_m:]
        xp, hp, k_n, v_n, ki_n, wkv_n, sh_n = _layer(
            xp, hp, bp, tp, 0, empty_kv, empty_kv, empty_ki, zero_state, zero_shift,
            mk.astype(BF16), mv.astype(BF16), lw, g_next, PROMPT_CFG)
        for lst, val in zip(outs_p, (k_n, v_n, ki_n, wkv_n, sh_n,
                                     mk.reshape(bp, n_mem, N_HEADS_M, HEAD_DIM_M),
                                     mv.reshape(bp, n_mem, N_HEADS_M, HEAD_DIM_M))):
            lst.append(val)
        xs, hs, k_n, v_n, ki_n, wkv_n, sh_n = _layer(
            xs, hs, bs, ts, past, cache_k[l], cache_v[l], cache_kidx[l], state_wkv[l], state_shift[l],
            cache_mem_k[l].reshape(bs, n_mem, d_m).astype(BF16), cache_mem_v[l].reshape(bs, n_mem, d_m).astype(BF16),
            lw, g_next, SAMPLE_CFG)
        for lst, val in zip(outs_s, (k_n, v_n, ki_n, wkv_n, sh_n)):
            lst.append(val)

    y_prompt = hp.reshape(bp, tp, d)
    y_sample = hs.reshape(bs, ts, d)
    return (y_prompt, y_sample) + tuple(jnp.stack(o) for o in outs_p) + tuple(jnp.stack(o) for o in outs_s)
```

```python
import functools

import jax
import jax.numpy as jnp
import numpy as np
from jax import lax
from jax.experimental import pallas as pl
from jax.experimental.pallas import tpu as pltpu

F32 = jnp.float32
BF16 = jnp.bfloat16

CHUNK = 64
ROPE_THETA = 10000.0
RMS_EPS = 1e-6
NEG_INF = -1e30
N_HEADS_A = 8
HEAD_DIM_A = 64
N_KV_A = 2
N_IDX = 4
D_IDX = 64
TOPK_MAX = 256
N_HEADS_R = 8
HEAD_DIM_R = 64
D_LORA = 64
GN_EPS = 64e-5
N_HEADS_M = 4
HEAD_DIM_M = 128
N_BRANCH = 3

LANES = 128
VMEM_LIMIT = 56 * 1024 * 1024

_NEG_BITS = int(np.array(NEG_INF, np.float32).view(np.int32))
NEG_KEY = _NEG_BITS ^ 0x7FFFFFFF
SIGN_FLIP = -(2 ** 31)


def _cparams(*sem):
    return pltpu.CompilerParams(dimension_semantics=sem, vmem_limit_bytes=VMEM_LIMIT)


def _rmsnorm_kernel(x_ref, g_ref, o_ref):
    x = x_ref[...]
    y = x * lax.rsqrt(jnp.mean(x * x, axis=-1, keepdims=True) + RMS_EPS)
    o_ref[...] = (y * g_ref[...]).astype(o_ref.dtype)


def _rmsnorm(x2d, g, out_dtype, tm):
    m, d = x2d.shape
    tm = min(tm, m)
    return pl.pallas_call(
        _rmsnorm_kernel,
        grid=(m // tm,),
        in_specs=[pl.BlockSpec((tm, d), lambda i: (i, 0)), pl.BlockSpec((1, d), lambda i: (0, 0))],
        out_specs=pl.BlockSpec((tm, d), lambda i: (i, 0)),
        out_shape=jax.ShapeDtypeStruct((m, d), out_dtype),
        compiler_params=_cparams("parallel"),
        name="rmsnorm",
    )(x2d, g.reshape(1, d))


def _matmul_kernel(x_ref, w_ref, o_ref):
    o_ref[...] = jnp.dot(x_ref[...].astype(BF16), w_ref[...], preferred_element_type=F32)


def _matmul(x2d, w, tm, tn):
    m, k = x2d.shape
    n = w.shape[1]
    tm = min(tm, m)
    return pl.pallas_call(
        _matmul_kernel,
        grid=(n // tn, m // tm),
        in_specs=[pl.BlockSpec((tm, k), lambda j, i: (i, 0)), pl.BlockSpec((k, tn), lambda j, i: (0, j))],
        out_specs=pl.BlockSpec((tm, tn), lambda j, i: (i, j)),
        out_shape=jax.ShapeDtypeStruct((m, n), F32),
        compiler_params=_cparams("parallel", "parallel"),
        name="matmul",
    )(x2d, w)


D_A = N_HEADS_A * HEAD_DIM_A
D_KV = N_KV_A * HEAD_DIM_A
K_OFF = D_A
V_OFF = K_OFF + D_KV
QI_OFF = V_OFF + D_KV
KI_OFF = QI_OFF + N_IDX * D_IDX
W_OFF = KI_OFF + D_IDX
ATT_COLS = KI_OFF + LANES
_G_K = K_OFF // LANES
_G_V = V_OFF // LANES
_G_KI = KI_OFF // LANES


def _attn_proj_kernel(h_ref, w_ref, cos_ref, sa_ref, sb_ref, o_ref, kb_ref, vb_ref, ki2_ref, kf_ref, vf_ref, kif_ref):
    p = jnp.dot(h_ref[...].astype(BF16), w_ref[...], preferred_element_type=F32)
    cos, sa, sb = cos_ref[...], sa_ref[...], sb_ref[...]
    lane = lax.broadcasted_iota(jnp.int32, cos.shape, 1)
    for g in range(ATT_COLS // LANES):
        x = p[:, g * LANES:(g + 1) * LANES]
        if g == _G_V:
            o_ref[:, g * LANES:(g + 1) * LANES] = x
            vb_ref[...] = x.astype(BF16)
            vf_ref[...] = x
            continue
        y = x * cos + pltpu.roll(x, LANES - 32, 1) * sa + pltpu.roll(x, 32, 1) * sb
        if g == _G_KI:
            y = jnp.where(lane < D_IDX, y, x)
            ki2_ref[...] = jnp.where(lane < D_IDX, y, pltpu.roll(y, D_IDX, 1)).astype(BF16)
            kif_ref[...] = y[:, :D_IDX]
        if g == _G_K:
            kb_ref[...] = y.astype(BF16)
            kf_ref[...] = y
        o_ref[:, g * LANES:(g + 1) * LANES] = y


def _rope_tables(pos0, t):
    d = HEAD_DIM_A
    inv = ROPE_THETA ** (-jnp.arange(0, d, 2, dtype=F32) / d)
    ang = (pos0 + jnp.arange(t)).astype(F32)[:, None] * inv[None, :]
    cos, sin = jnp.cos(ang), jnp.sin(ang)
    zero = jnp.zeros_like(sin)
    cos_f = jnp.concatenate([cos] * 4, axis=-1)
    sin_a = jnp.concatenate([-sin, zero] * 2, axis=-1)
    sin_b = jnp.concatenate([zero, sin] * 2, axis=-1)
    return cos_f, sin_a, sin_b


def _attn_proj(h2d, w_att, tables, t, tm):
    m, k = h2d.shape
    nt = t // tm
    tab_spec = pl.BlockSpec((tm, LANES), lambda i: (i % nt, 0))
    lane_spec = pl.BlockSpec((tm, LANES), lambda i: (i, 0))
    lane_shape = jax.ShapeDtypeStruct((m, LANES), BF16)
    return pl.pallas_call(
        _attn_proj_kernel,
        grid=(m // tm,),
        in_specs=[pl.BlockSpec((tm, k), lambda i: (i, 0)), pl.BlockSpec((k, ATT_COLS), lambda i: (0, 0)),
                  tab_spec, tab_spec, tab_spec],
        out_specs=[pl.BlockSpec((tm, ATT_COLS), lambda i: (i, 0)), lane_spec, lane_spec, lane_spec,
                   lane_spec, lane_spec, pl.BlockSpec((tm, D_IDX), lambda i: (i, 0))],
        out_shape=[jax.ShapeDtypeStruct((m, ATT_COLS), F32), lane_shape, lane_shape, lane_shape,
                   jax.ShapeDtypeStruct((m, LANES), F32), jax.ShapeDtypeStruct((m, LANES), F32),
                   jax.ShapeDtypeStruct((m, D_IDX), F32)],
        compiler_params=_cparams("parallel"),
        name="attn_proj",
    )(h2d, w_att, *tables)


I16 = jnp.int16
I16_MIN = -(2 ** 15)
I16_MAX = 2 ** 15 - 1
I16_ROWS = 16
I32_ROWS = 8
LOG2E = 1.4426950408889634


def _tree(parts, op):
    while len(parts) > 1:
        parts = [op(parts[i], parts[i + 1]) if i + 1 < len(parts) else parts[i] for i in range(0, len(parts), 2)]
    return parts[0]


def _fold_rows(x, rows, op):
    return _tree([x[g * rows:(g + 1) * rows] for g in range(x.shape[0] // rows)], op)


def _dsa_kernel(pa_ref, ki_ref, k_ref, vt_ref, o_ref, key_sc, dig_sc, *, nb, tq, tk, pos0, t_real, s_real, topk):
    qb = pl.program_id(1)
    tqb = tq // nb
    q0 = pos0 + qb * tqb
    n_adm = jnp.minimum(q0 + tqb, s_real)
    n_kt = (n_adm + tk - 1) // tk
    hd = HEAD_DIM_A
    group = N_HEADS_A // N_KV_A
    cols = group * tq

    q_in_block = lax.broadcasted_iota(jnp.int32, (1, tq), 1) % tqb
    q_pos = q0 + q_in_block
    key_limit = jnp.minimum((lax.shift_right_arithmetic(q_pos, CHUNK.bit_length() - 1) + 1) * CHUNK, s_real)
    key_iota = lax.broadcasted_iota(jnp.int32, (tk, tq), 0)
    low_half = lax.broadcasted_iota(jnp.int32, (tq, LANES), 1) < hd
    halves = (low_half, jnp.logical_not(low_half))
    real_q = qb * tqb + q_in_block < t_real
    row_batch = lax.broadcasted_iota(jnp.int32, (tq, LANES), 0) // tqb

    def by_batch(x):
        if nb == 1:
            return x
        return jnp.concatenate([jnp.where(row_batch == i, x, 0.0) for i in range(nb)], axis=1)

    def key_rows(ref, start):
        return jnp.concatenate([ref[i, pl.ds(start, tk), :] for i in range(nb)], axis=1)

    pa = jnp.concatenate([pa_ref[i] for i in range(nb)], axis=0)
    qi = jnp.concatenate(
        [by_batch(jnp.where(halves[h % 2],
                            pa[:, QI_OFF + (h // 2) * LANES:QI_OFF + (h // 2 + 1) * LANES] * (D_IDX ** -0.5), 0.0))
         for h in range(N_IDX)], axis=0).astype(BF16)
    w_t = pa[:, KI_OFF:KI_OFF + LANES].T * (N_IDX ** -0.5)
    w_rows = [w_t[D_IDX + h:D_IDX + h + 1, :] for h in range(N_IDX)]

    def score_body(kt, carry):
        start = pl.multiple_of(kt * tk, tk)
        s = lax.dot_general(key_rows(ki_ref, start), qi, (((1,), (1,)), ((), ())), preferred_element_type=F32)
        score = jnp.zeros((tk, tq), F32)
        for h in range(N_IDX):
            score = score + jnp.maximum(s[:, h * tq:(h + 1) * tq], 0.0) * w_rows[h]
        score = score + 0.0
        score = jnp.where(start + key_iota < key_limit, score, NEG_INF)
        bits = pltpu.bitcast(score, jnp.int32)
        key = jnp.where(bits < 0, bits ^ 0x7FFFFFFF, bits)
        key_sc[kt] = key
        dig_sc[kt] = lax.shift_right_arithmetic(key, 16).astype(I16)
        return carry

    lax.fori_loop(0, n_kt, score_body, 0)

    def count(pred_fn):
        def body(kt, acc):
            hit = jnp.where(pred_fn(key_sc[kt], kt), 1, 0)
            return acc + _fold_rows(hit, I32_ROWS, jnp.add)

        acc = lax.fori_loop(0, n_kt, body, jnp.zeros((I32_ROWS, tq), jnp.int32))
        return jnp.sum(acc, axis=0, keepdims=True)

    def count16(ref, c_of_tile, cmp):
        def body(kt, acc):
            hit = jnp.where(cmp(ref[kt], c_of_tile(kt).astype(I16)), jnp.ones((), I16), jnp.zeros((), I16))
            return acc + _fold_rows(hit, I16_ROWS, jnp.add)

        acc = lax.fori_loop(0, n_kt, body, jnp.zeros((I16_ROWS, tq), I16))
        return jnp.sum(acc.astype(jnp.int32), axis=0, keepdims=True)

    def kth_digit(ref, kk):
        def body(i, prefix):
            cand = prefix | lax.shift_left(jnp.int32(1), 15 - i)
            cnt = count16(ref, lambda kt: cand + I16_MIN, lax.ge)
            return jnp.where(cnt >= kk, cand, prefix)

        return lax.fori_loop(0, 16, body, jnp.zeros((1, tq), jnp.int32))

    hi_thr = kth_digit(dig_sc, topk) + I16_MIN
    above = jnp.where(hi_thr < I16_MAX, count16(dig_sc, lambda kt: jnp.minimum(hi_thr + 1, I16_MAX), lax.ge), 0)

    def lo_body(kt, carry):
        key = key_sc[kt]
        in_bucket = lax.shift_right_arithmetic(key, 16) == hi_thr
        dig_sc[kt] = jnp.where(in_bucket, (key & 0xFFFF) + I16_MIN, I16_MIN).astype(I16)
        return carry

    lax.fori_loop(0, n_kt, lo_body, 0)
    lo_thr = kth_digit(dig_sc, topk - above)
    thr = jnp.maximum(lax.shift_left(hi_thr, 16) | lo_thr, NEG_KEY)
    cnt_gt = count(lambda key, kt: key > thr)
    cnt_ge = count(lambda key, kt: key >= thr)
    need = topk - cnt_gt
    real_thr = thr != NEG_KEY
    has_excess = jnp.max(jnp.where(real_thr & real_q & (cnt_ge > topk), 1, 0)) > 0

    n_idx_bits = max(1, int(np.ceil(np.log2(key_sc.shape[0] * tk))))

    def tie_search():
        def mark_body(kt, carry):
            dig_sc[kt] = jnp.where(key_sc[kt] == thr, key_iota, I16_MAX).astype(I16)
            return carry

        lax.fori_loop(0, n_kt, mark_body, 0)

        def jbody(i, jq):
            cand = jq | lax.shift_left(jnp.int32(1), n_idx_bits - 1 - i)
            cnt = count16(dig_sc, lambda kt: jnp.clip(cand - kt * tk, I16_MIN, I16_MAX), lax.lt)
            return jnp.where(cnt < need, cand, jq)

        return lax.fori_loop(0, n_idx_bits, jbody, jnp.zeros((1, tq), jnp.int32))

    j_cut = lax.cond(has_excess, tie_search, lambda: jnp.full((1, tq), 2 ** 30, jnp.int32))
    j_cut = jnp.where(real_thr, j_cut, -1)

    qs = []
    for j in range(N_KV_A):
        parts = []
        for g in range(group):
            h = j * group + g
            grp = pa[:, (h // 2) * LANES:(h // 2 + 1) * LANES] * (hd ** -0.5)
            if h % 2 != j:
                grp = pltpu.roll(grp, hd, 1)
            parts.append(by_batch(jnp.where(halves[j], grp, 0.0)))
        qs.append(jnp.concatenate(parts, axis=0).astype(BF16))

    col_batch = (lax.broadcasted_iota(jnp.int32, (1, cols), 1) % tq) // tqb

    def attn_body(kt, carry):
        start = pl.multiple_of(kt * tk, tk)
        key = key_sc[kt]
        sel = (key > thr) | ((key == thr) & (start + key_iota <= j_cut))
        bias = jnp.where(sel, 0.0, NEG_INF).astype(F32)
        bias = jnp.concatenate([bias] * group, axis=1)
        k_tile = key_rows(k_ref, start)
        heads = range(N_KV_A)
        logits = [lax.dot_general(k_tile, qs[j], (((1,), (1,)), ((), ())), preferred_element_type=F32) * LOG2E
                  + bias for j in heads]
        m_new = [jnp.maximum(carry[j][0], jnp.max(_fold_rows(logits[j], I32_ROWS, jnp.maximum), axis=0,
                                                   keepdims=True)) for j in heads]
        alpha = [jnp.exp2(carry[j][0] - m_new[j]) for j in heads]
        p = [jnp.exp2(logits[j] - m_new[j]) for j in heads]
        l_new = [carry[j][1] * alpha[j] + jnp.sum(_fold_rows(p[j], I32_ROWS, jnp.add), axis=0, keepdims=True)
                 for j in heads]
        acc = []
        for j in heads:
            v_t = jnp.concatenate([vt_ref[i, j * hd:(j + 1) * hd, pl.ds(start, tk)] for i in range(nb)], axis=0)
            pv_all = jnp.dot(v_t, p[j].astype(BF16), preferred_element_type=F32)
            pv = pv_all[:hd]
            for i in range(1, nb):
                pv = jnp.where(col_batch == i, pv_all[i * hd:(i + 1) * hd], pv)
            acc.append(carry[j][2] * alpha[j] + pv)
        return tuple((m_new[j], l_new[j], acc[j]) for j in heads)

    init = tuple((jnp.full((1, cols), 0.1 * NEG_INF, F32), jnp.zeros((1, cols), F32),
                  jnp.zeros((hd, cols), F32)) for _ in range(N_KV_A))
    res = lax.fori_loop(0, n_kt, attn_body, init)
    outs = [acc / l_i for _, l_i, acc in res]
    for grp_i in range(N_HEADS_A // 2):
        pair = []
        for h in (2 * grp_i, 2 * grp_i + 1):
            j, g = h // group, h % group
            pair.append(outs[j][:, g * tq:(g + 1) * tq])
        o_t = jnp.concatenate(pair, axis=0).T
        for i in range(nb):
            o_ref[i, :, grp_i * LANES:(grp_i + 1) * LANES] = o_t[i * tqb:(i + 1) * tqb]


def _dsa(pa, ki_all, k_all, vt_all, *, pos0, s_real, nb, tq, tk, kv_buffers):
    b, t, _ = pa.shape
    assert tk < I16_MAX
    tqb = tq // nb
    t_pad = _round_up(t, tqb)
    pa = jnp.pad(pa, ((0, 0), (0, t_pad - t), (0, 0)))
    s_pad = ki_all.shape[1]
    topk = min(TOPK_MAX, s_real // 4)
    kern = functools.partial(_dsa_kernel, nb=nb, tq=tq, tk=tk, pos0=pos0, t_real=t, s_real=s_real, topk=topk)
    kv_mode = pl.Buffered(kv_buffers)
    kv_spec = pl.BlockSpec((nb, s_pad, LANES), lambda bi, qi: (bi, 0, 0), pipeline_mode=kv_mode)
    out = pl.pallas_call(
        kern,
        grid=(b // nb, t_pad // tqb),
        in_specs=[pl.BlockSpec((nb, tqb, ATT_COLS), lambda bi, qi: (bi, qi, 0)), kv_spec, kv_spec,
                  pl.BlockSpec((nb, LANES, s_pad), lambda bi, qi: (bi, 0, 0), pipeline_mode=kv_mode)],
        out_specs=pl.BlockSpec((nb, tqb, D_A), lambda bi, qi: (bi, qi, 0)),
        out_shape=jax.ShapeDtypeStruct((b, t_pad, D_A), F32),
        scratch_shapes=[pltpu.VMEM((s_pad // tk, tk, tq), jnp.int32), pltpu.VMEM((s_pad // tk, tk, tq), I16)],
        compiler_params=_cparams("parallel", "arbitrary"),
        name="dsa",
    )(pa, ki_all, k_all, vt_all)
    return out[:, :t]


D_R = N_HEADS_R * HEAD_DIM_R
D_SHIFT = 3 * D_R + 2 * D_LORA
N_PAIR = D_R // LANES
N_VEC = 8


def _split2(x):
    hi = x.astype(BF16)
    lo = (x - hi.astype(F32)).astype(BF16)
    return hi, lo


def _mm(a, b, ca, cb):
    return lax.dot_general(a, b, (((ca,), (cb,)), ((), ())), preferred_element_type=F32)


def _rwkv_kernel(sh_ref, sh0_ref, mu_ref, vec_ref, wl_ref, s0_ref, y_ref, sout_ref, st_sc, row_sc, *, bb, c):
    ci = pl.program_id(1)
    n = HEAD_DIM_R
    c2 = 2 * c

    @pl.when(ci == 0)
    def _():
        st_sc[...] = s0_ref[...]
        row_sc[...] = sh0_ref[...]

    head0 = lax.broadcasted_iota(jnp.int32, (c, LANES), 1) < n
    r2 = lax.broadcasted_iota(jnp.int32, (c2, c2), 0)
    q2 = lax.broadcasted_iota(jnp.int32, (c2, c2), 1)
    same_head = (r2 // c) == (q2 // c)
    strict = same_head & (r2 > q2)
    incl = same_head & (r2 >= q2)
    eye2 = jnp.where(r2 == q2, 1.0, 0.0).astype(F32)
    rc = lax.broadcasted_iota(jnp.int32, (c, c), 0)
    cc = lax.broadcasted_iota(jnp.int32, (c, c), 1)
    tri = jnp.where(rc >= cc, 1.0, 0.0).astype(BF16)
    tri3 = jnp.concatenate([tri, tri, tri], axis=1)
    la = lax.broadcasted_iota(jnp.int32, (LANES, LANES), 0)
    lb = lax.broadcasted_iota(jnp.int32, (LANES, LANES), 1)
    seg = jnp.where((la // n) == (lb // n), 1.0, 0.0).astype(BF16)
    seg2 = jnp.concatenate([seg, seg], axis=0)
    first_row = lax.broadcasted_iota(jnp.int32, (c, D_SHIFT), 0) == 0

    def seg_sum(x):
        xh, xl = _split2(x)
        return jnp.dot(jnp.concatenate([xh, xl], axis=1), seg2, preferred_element_type=F32)

    def vec(i, sl=slice(None)):
        return vec_ref[i:i + 1, sl]

    n_double = 0
    while (2 << n_double) < c:
        n_double += 1

    def stack(zz):
        return jnp.concatenate([jnp.where(head0, zz, 0.0), jnp.where(head0, 0.0, zz)], axis=0)

    chains = [(bi, p) for bi in range(bb) for p in range(N_PAIR)]
    states = [st_sc[bi, p] for bi, p in chains]

    prep = []
    for bi in range(bb):
        sh = sh_ref[bi]
        prev = jnp.where(first_row, row_sc[bi], pltpu.roll(sh, 1, 0))
        x = sh + (prev - sh) * mu_ref[...]
        r = x[:, :D_R]
        k = x[:, D_R:2 * D_R]
        v = x[:, 2 * D_R:3 * D_R]
        xl = x[:, 3 * D_R:]
        lora_in = jnp.where(head0, jnp.tanh(xl), xl)
        wa = jnp.dot(lora_in.astype(BF16), wl_ref[...], preferred_element_type=F32)
        w_raw = vec(0) + wa[:, :D_R]
        a = jax.nn.sigmoid(vec(1) + wa[:, D_R:])
        lw = -jnp.exp(-jax.nn.softplus(-w_raw) - 0.5)
        kkv = k * vec(2)
        kk2 = kkv * kkv
        ss = jnp.concatenate([seg_sum(kk2[:, p * LANES:(p + 1) * LANES]) for p in range(N_PAIR)], axis=1)
        kk = kkv / jnp.maximum(jnp.sqrt(ss), 1e-12)
        k2 = k * (1.0 + (a - 1.0) * vec(3))
        l_hi = lw.astype(BF16)
        l_r1 = lw - l_hi.astype(F32)
        l_mid = l_r1.astype(BF16)
        l_lo = (l_r1 - l_mid.astype(F32)).astype(BF16)
        cum = jnp.dot(tri3, jnp.concatenate([l_hi, l_mid, l_lo], axis=0), preferred_element_type=F32)
        p_fwd = jnp.exp(cum)
        p_inv = jnp.exp(-cum)
        prep.append(dict(v=v, p_fwd=p_fwd, at=-kk * jnp.exp(cum - lw), bt=kk * a * p_inv, kt=k2 * p_inv,
                         rt=r * p_fwd, rk_prod=r * k2 * vec(4)))
    for bi in range(bb):
        row_sc[bi] = sh_ref[bi, c - 1:c, :]

    ops = []
    for bi, p in chains:
        sl = slice(p * LANES, (p + 1) * LANES)
        d = prep[bi]
        b_st, k_st, v_st = stack(d["bt"][:, sl]), stack(d["kt"][:, sl]), stack(d["v"][:, sl])
        ar = jnp.concatenate([stack(d["at"][:, sl]), stack(d["rt"][:, sl])], axis=0).astype(BF16)
        bk = jnp.concatenate([b_st, k_st], axis=0).astype(BF16)
        p_last = d["p_fwd"][c - 1:c, sl]
        ops.append(dict(ar=ar, bk=bk, v_b=v_st.astype(BF16), p_last=p_last,
                        b_dec=(b_st * p_last).astype(BF16), k_dec=(k_st * p_last).astype(BF16)))
    grams = [_mm(o["ar"], o["bk"], 1, 1) for o in ops]
    l_ab = [jnp.where(strict, g[:c2, :c2], 0.0) for g in grams]
    l_ak = [jnp.where(strict, g[:c2, c2:], 0.0).astype(BF16) for g in grams]
    m_rb = [jnp.where(incl, g[c2:, :c2], 0.0).astype(BF16) for g in grams]
    m_rk = [jnp.where(incl, g[c2:, c2:], 0.0).astype(BF16) for g in grams]
    inv = [eye2 + l for l in l_ab]
    lp = l_ab
    for _ in range(n_double):
        lpb = [l.astype(BF16) for l in lp]
        lp = [_mm(l, l, 1, 0) for l in lpb]
        inv = [x + _mm(x.astype(BF16), l.astype(BF16), 1, 0) for x, l in zip(inv, lp)]

    s_b = [s.astype(BF16) for s in states]
    z = [_mm(o["ar"][:c2], sb, 1, 1) + _mm(la, o["v_b"], 1, 0) for o, sb, la in zip(ops, s_b, l_ak)]
    u_b = [_mm(x.astype(BF16), zz.astype(BF16), 1, 0).astype(BF16) for x, zz in zip(inv, z)]
    y_st = [_mm(o["ar"][c2:], sb, 1, 1) + _mm(mb, u, 1, 0) + _mm(mk, o["v_b"], 1, 0)
            for o, sb, mb, mk, u in zip(ops, s_b, m_rb, m_rk, u_b)]
    new_states = [s * o["p_last"] + _mm(u, o["b_dec"], 0, 0) + _mm(o["v_b"], o["k_dec"], 0, 0)
                  for s, o, u in zip(states, ops, u_b)]
    for (bi, p), s_new, ys in zip(chains, new_states, y_st):
        sl = slice(p * LANES, (p + 1) * LANES)
        st_sc[bi, p] = s_new
        y = ys[:c] + ys[c:]
        dev = y - seg_sum(y) * (1.0 / n)
        var = seg_sum(dev * dev) * (1.0 / n)
        yn = dev * lax.rsqrt(var + GN_EPS) * vec(5, sl) + vec(6, sl)
        y_ref[bi, :, sl] = yn + seg_sum(prep[bi]["rk_prod"][:, sl]) * prep[bi]["v"][:, sl]

    @pl.when(ci == pl.num_programs(1) - 1)
    def _():
        sout_ref[...] = st_sc[...]


def _rwkv(sh, shift0, mu, w0, w2, a0, a2, k_k, k_a, r_k, ln_w, ln_b, state0, *, bb, c):
    b, t, _ = sh.shape
    n = HEAD_DIM_R
    eye = jnp.eye(2, dtype=F32)
    s0 = jnp.einsum('bphij,hg->bphigj', state0.reshape(b, N_PAIR, 2, n, n), eye).reshape(b, N_PAIR, LANES, LANES)
    vecs = jnp.stack([w0, a0, k_k, k_a, r_k.reshape(D_R), ln_w, ln_b, jnp.zeros_like(w0)])
    zero = jnp.zeros((D_LORA, D_R), F32)
    w_lora = jnp.concatenate([jnp.concatenate([w2, zero], axis=1), jnp.concatenate([zero, a2], axis=1)],
                             axis=0).astype(BF16)
    full = lambda shape: pl.BlockSpec(shape, lambda bi, ci: (0,) * len(shape))
    st_spec = pl.BlockSpec((bb, N_PAIR, LANES, LANES), lambda bi, ci: (bi, 0, 0, 0))
    y, s_out = pl.pallas_call(
        functools.partial(_rwkv_kernel, bb=bb, c=c),
        grid=(b // bb, t // c),
        in_specs=[pl.BlockSpec((bb, c, D_SHIFT), lambda bi, ci: (bi, ci, 0)),
                  pl.BlockSpec((bb, 1, D_SHIFT), lambda bi, ci: (bi, 0, 0)),
                  full((1, D_SHIFT)), full((N_VEC, D_R)), full((2 * D_LORA, 2 * D_R)), st_spec],
        out_specs=[pl.BlockSpec((bb, c, D_R), lambda bi, ci: (bi, ci, 0)), st_spec],
        out_shape=[jax.ShapeDtypeStruct((b, t, D_R), F32),
                   jax.ShapeDtypeStruct((b, N_PAIR, LANES, LANES), F32)],
        scratch_shapes=[pltpu.VMEM((bb, N_PAIR, LANES, LANES), F32), pltpu.VMEM((bb, 1, D_SHIFT), F32)],
        compiler_params=_cparams("parallel", "arbitrary"),
        name="rwkv",
    )(sh, shift0, mu.reshape(1, D_SHIFT), vecs, w_lora, s0)
    s_t = jnp.einsum('bphigj,hg->bphij', s_out.reshape(b, N_PAIR, 2, n, 2, n), eye)
    return y, s_t.reshape(b, N_HEADS_R, n, n)


def _mem_attn_kernel(q_ref, mk_ref, mv_ref, o_ref):
    scale = HEAD_DIM_M ** -0.5
    for h in range(N_HEADS_M):
        sl = slice(h * HEAD_DIM_M, (h + 1) * HEAD_DIM_M)
        q = q_ref[0, :, sl].astype(BF16)
        logits = lax.dot_general(q, mk_ref[0, :, sl].astype(BF16), (((1,), (1,)), ((), ())),
                                 preferred_element_type=F32) * scale
        e = jnp.exp(logits - jnp.max(logits, axis=1, keepdims=True))
        p = e / jnp.sum(e, axis=1, keepdims=True)
        o_ref[0, :, sl] = jnp.dot(p.astype(BF16), mv_ref[0, :, sl].astype(BF16), preferred_element_type=F32)


def _mem_attn(qm, mk, mv, tm):
    b, t, d = qm.shape
    nm = mk.shape[1]
    return pl.pallas_call(
        _mem_attn_kernel,
        grid=(b, t // tm),
        in_specs=[pl.BlockSpec((1, tm, d), lambda bi, i: (bi, i, 0)),
                  pl.BlockSpec((1, nm, d), lambda bi, i: (bi, 0, 0)),
                  pl.BlockSpec((1, nm, d), lambda bi, i: (bi, 0, 0))],
        out_specs=pl.BlockSpec((1, tm, d), lambda bi, i: (bi, i, 0)),
        out_shape=jax.ShapeDtypeStruct((b, t, d), F32),
        compiler_params=_cparams("parallel", "parallel"),
        name="mem_attn",
    )(qm, mk, mv)


def _merge_kernel(oa_ref, yr_ref, om_ref, zg_ref, x_ref, wa_ref, wr_ref, wm_ref, wo_ref, g_ref, xo_ref, ho_ref):
    d_br = oa_ref.shape[1]
    d = x_ref.shape[1]

    def branch(o, z, w_ref):
        return jnp.dot((o * jax.nn.silu(z)).astype(BF16), w_ref[...], preferred_element_type=F32)

    pa = branch(oa_ref[...], zg_ref[:, 0:d_br], wa_ref)
    pr = branch(yr_ref[...], zg_ref[:, d_br:2 * d_br], wr_ref)
    pm = branch(om_ref[...], zg_ref[:, 2 * d_br:3 * d_br], wm_ref)
    g0 = 3 * d_br
    merged = (jax.nn.sigmoid(zg_ref[:, g0:g0 + d]) * pa
              + jax.nn.sigmoid(zg_ref[:, g0 + d:g0 + 2 * d]) * pr
              + jax.nn.sigmoid(zg_ref[:, g0 + 2 * d:g0 + 3 * d]) * pm)
    xn = x_ref[...] + jnp.dot(merged.astype(BF16), wo_ref[...], preferred_element_type=F32)
    xo_ref[...] = xn
    y = xn * lax.rsqrt(jnp.mean(xn * xn, axis=-1, keepdims=True) + RMS_EPS)
    ho_ref[...] = y * g_ref[...]


def _merge(oa, yr, om, zg, x2d, wa, wr, wm, wo, g_next, tm):
    m, d = x2d.shape
    d_br = oa.shape[1]
    tm = min(tm, m)
    row = lambda w: pl.BlockSpec((tm, w), lambda i: (i, 0))
    full = lambda a: pl.BlockSpec(a.shape, lambda i: (0, 0))
    g2 = g_next.reshape(1, d)
    return pl.pallas_call(
        _merge_kernel,
        grid=(m // tm,),
        in_specs=[row(d_br), row(d_br), row(d_br), row(zg.shape[1]), row(d),
                  full(wa), full(wr), full(wm), full(wo), full(g2)],
        out_specs=[row(d), row(d)],
        out_shape=[jax.ShapeDtypeStruct((m, d), F32), jax.ShapeDtypeStruct((m, d), F32)],
        compiler_params=_cparams("parallel"),
        name="merge",
    )(oa, yr, om, zg, x2d, wa, wr, wm, wo, g2)


def _round_up(x, m):
    return (x + m - 1) // m * m


IN_SIZES = (D_A, D_KV, D_KV, N_IDX * D_IDX, D_IDX, N_IDX, D_A, D_SHIFT, D_R, N_HEADS_M * HEAD_DIM_M,
            N_HEADS_M * HEAD_DIM_M)
IN_OFFS = tuple(int(o) for o in np.cumsum((0,) + IN_SIZES))


def _repack_kernel(w_ref, att_ref, sh_ref, qm_ref, zg_ref):
    col = lambda i: w_ref[:, IN_OFFS[i]:IN_OFFS[i + 1]]
    rows = w_ref.shape[0]
    att_ref[:, :W_OFF + N_IDX] = w_ref[:, :IN_OFFS[6]].astype(BF16)
    att_ref[:, W_OFF + N_IDX:] = jnp.zeros((rows, ATT_COLS - W_OFF - N_IDX), BF16)
    sh_ref[...] = col(7).astype(BF16)
    qm_ref[...] = col(9).astype(BF16)
    zg_ref[:, :D_A] = col(6).astype(BF16)
    zg_ref[:, D_A:D_A + D_R] = col(8).astype(BF16)
    zg_ref[:, D_A + D_R:D_A + D_R + IN_SIZES[10]] = col(10).astype(BF16)
    zg_ref[:, D_A + D_R + IN_SIZES[10]:] = w_ref[:, IN_OFFS[11]:].astype(BF16)


def _split_w_in(w_in):
    d_model, d_in = w_in.shape
    n_zg = D_A + D_R + IN_SIZES[10] + d_in - IN_OFFS[11]
    tr = 128
    widths = (ATT_COLS, D_SHIFT, IN_SIZES[9], n_zg)
    return tuple(pl.pallas_call(
        _repack_kernel,
        grid=(d_model // tr,),
        in_specs=[pl.BlockSpec((tr, d_in), lambda i: (i, 0))],
        out_specs=[pl.BlockSpec((tr, w), lambda i: (i, 0)) for w in widths],
        out_shape=[jax.ShapeDtypeStruct((d_model, w), BF16) for w in widths],
        compiler_params=_cparams("parallel"),
        name="repack",
    )(w_in))


def _layer(x2d, h2d, b, t, pos0, past_k, past_v, past_ki, wkv0, shift0, mem_k, mem_v, lw, g_next, cfg):
    (w_att, w_sh, w_qm, w_zg, mu_shift, w0, w2, a0, a2, k_k, k_a, r_k, ln_w, ln_b, w_br_a, w_br_r, w_br_m,
     w_out) = lw
    tm = cfg["tm"]
    d_a = N_HEADS_A * HEAD_DIM_A
    d_kv = N_KV_A * HEAD_DIM_A
    d_r = N_HEADS_R * HEAD_DIM_R
    d_m = N_HEADS_M * HEAD_DIM_M

    tables = _rope_tables(pos0, t)
    pa, kb, vb, ki2, k_f32, v_f32, ki_f32 = _attn_proj(h2d, w_att, tables, t, min(tm, t))
    pa = pa.reshape(b, t, ATT_COLS)
    sh = _matmul(h2d, w_sh, tm, w_sh.shape[1]).reshape(b, t, -1)
    qm = _matmul(h2d, w_qm, tm, w_qm.shape[1]).reshape(b, t, d_m)
    zg = _matmul(h2d, w_zg, tm, cfg["tn_zg"])

    k_new = k_f32.reshape(b, t, N_KV_A, HEAD_DIM_A)
    v_new = v_f32.reshape(b, t, N_KV_A, HEAD_DIM_A)
    ki_new = ki_f32.reshape(b, t, D_IDX)
    s_real = pos0 + t
    tq, tk = cfg["tq"], cfg["tk"]
    s_pad = _round_up(s_real, tk)

    def with_past(past, new):
        full = jnp.concatenate([past.reshape(b, pos0, LANES).astype(BF16), new.reshape(b, t, LANES)], axis=1)
        return jnp.pad(full, ((0, 0), (0, s_pad - s_real), (0, 0)))

    vt_all = with_past(past_v, vb).transpose(0, 2, 1)
    o_a = _dsa(pa, with_past(jnp.concatenate([past_ki, past_ki], axis=-1), ki2), with_past(past_k, kb),
               vt_all, pos0=pos0, s_real=s_real, nb=cfg["nb"], tq=tq, tk=tk, kv_buffers=cfg["kv_buffers"]).reshape(b * t, d_a)

    y_r, wkv_t = _rwkv(sh, shift0, mu_shift, w0, w2, a0, a2, k_k, k_a, r_k, ln_w, ln_b, wkv0,
                       bb=cfg["bb"], c=CHUNK)
    y_r = y_r.reshape(b * t, d_r)

    o_m = _mem_attn(qm, mem_k, mem_v, min(tm, t)).reshape(b * t, d_m)

    x_new, h_next = _merge(o_a, y_r, o_m, zg, x2d, w_br_a, w_br_r, w_br_m, w_out, g_next, cfg["tm_merge"])
    return x_new, h_next, k_new, v_new, ki_new, wkv_t, sh[:, -1:]


PROMPT_CFG = dict(tm=512, tn_zg=2304, nb=1, tq=512, tk=512, kv_buffers=1, bb=2, tm_merge=512)
SAMPLE_CFG = dict(tm=512, tn_zg=2304, nb=2, tq=128, tk=2176, kv_buffers=2, bb=2, tm_merge=512)


def kernel(x_prompt, x_sample, mem_prompt, cache_k, cache_v, cache_kidx, state_wkv, state_shift, cache_mem_k,
           cache_mem_v, norm_g, w_in, mu_shift, w0, w2, a0, a2, k_k, k_a, r_k, ln_x_w, ln_x_b, mem_norm_g,
           w_mem_kv, w_br_a, w_br_r, w_br_m, w_out, final_norm_g):
    depth = w_in.shape[0]
    bp, tp, d = x_prompt.shape
    bs, ts, _ = x_sample.shape
    n_mem = mem_prompt.shape[1]
    d_m = N_HEADS_M * HEAD_DIM_M
    past = cache_k.shape[2]

    xp = x_prompt.reshape(bp * tp, d)
    xs = x_sample.reshape(bs * ts, d)
    hp = _rmsnorm(xp, norm_g[0], F32, 512)
    hs = _rmsnorm(xs, norm_g[0], F32, 512)
    mem2d = mem_prompt.reshape(bp * n_mem, d)

    empty_kv = jnp.zeros((bp, 0, N_KV_A, HEAD_DIM_A), F32)
    empty_ki = jnp.zeros((bp, 0, D_IDX), F32)
    zero_state = jnp.zeros((bp, N_HEADS_R, HEAD_DIM_R, HEAD_DIM_R), F32)
    zero_shift = jnp.zeros((bp, 1, mu_shift.shape[1]), F32)

    outs_p = [[] for _ in range(7)]
    outs_s = [[] for _ in range(5)]
    for l in range(depth):
        g_next = norm_g[l + 1] if l + 1 < depth else final_norm_g
        lw = _split_w_in(w_in[l]) + (mu_shift[l], w0[l], w2[l], a0[l], a2[l], k_k[l], k_a[l], r_k[l], ln_x_w[l],
                                     ln_x_b[l], w_br_a[l].astype(BF16), w_br_r[l].astype(BF16),
                                     w_br_m[l].astype(BF16), w_out[l].astype(BF16))
        mem_h = _rmsnorm(mem2d, mem_norm_g[l], F32, 512)
        mkv = _matmul(mem_h, w_mem_kv[l].astype(BF16), 512, 2 * d_m).reshape(bp, n_mem, 2 * d_m)
        mk, mv = mkv[..., :d_m], mkv[..., d_m:]
        xp, hp, k_n, v_n, ki_n, wkv_n, sh_n = _layer(
            xp, hp, bp, tp, 0, empty_kv, empty_kv, empty_ki, zero_state, zero_shift,
            mk, mv, lw, g_next, PROMPT_CFG)
        for lst, val in zip(outs_p, (k_n, v_n, ki_n, wkv_n, sh_n,
                                     mk.reshape(bp, n_mem, N_HEADS_M, HEAD_DIM_M),
                                     mv.reshape(bp, n_mem, N_HEADS_M, HEAD_DIM_M))):
            lst.append(val)
        xs, hs, k_n, v_n, ki_n, wkv_n, sh_n = _layer(
            xs, hs, bs, ts, past, cache_k[l], cache_v[l], cache_kidx[l], state_wkv[l], state_shift[l],
            cache_mem_k[l].reshape(bs, n_mem, d_m), cache_mem_v[l].reshape(bs, n_mem, d_m),
            lw, g_next, SAMPLE_CFG)
        for lst, val in zip(outs_s, (k_n, v_n, ki_n, wkv_n, sh_n)):
            lst.append(val)

    y_prompt = hp.reshape(bp, tp, d)
    y_sample = hs.reshape(bs, ts, d)
    return (y_prompt, y_sample) + tuple(jnp.stack(o) for o in outs_p) + tuple(jnp.stack(o) for o in outs_s)
```

```python
import functools

import jax
import jax.numpy as jnp
import numpy as np
from jax import lax
from jax.experimental import pallas as pl
from jax.experimental.pallas import tpu as pltpu

F32 = jnp.float32
BF16 = jnp.bfloat16

CHUNK = 64
ROPE_THETA = 10000.0
RMS_EPS = 1e-6
NEG_INF = -1e30
N_HEADS_A = 8
HEAD_DIM_A = 64
N_KV_A = 2
N_IDX = 4
D_IDX = 64
TOPK_MAX = 256
N_HEADS_R = 8
HEAD_DIM_R = 64
D_LORA = 64
GN_EPS = 64e-5
N_HEADS_M = 4
HEAD_DIM_M = 128
N_BRANCH = 3

LANES = 128
VMEM_LIMIT = 56 * 1024 * 1024

_NEG_BITS = int(np.array(NEG_INF, np.float32).view(np.int32))
NEG_KEY = _NEG_BITS ^ 0x7FFFFFFF
SIGN_FLIP = -(2 ** 31)


def _cparams(*sem):
    return pltpu.CompilerParams(dimension_semantics=sem, vmem_limit_bytes=VMEM_LIMIT)


def _rmsnorm_kernel(x_ref, g_ref, o_ref):
    x = x_ref[...]
    y = x * lax.rsqrt(jnp.mean(x * x, axis=-1, keepdims=True) + RMS_EPS)
    o_ref[...] = (y * g_ref[...]).astype(o_ref.dtype)


def _rmsnorm(x2d, g, out_dtype, tm):
    m, d = x2d.shape
    tm = min(tm, m)
    return pl.pallas_call(
        _rmsnorm_kernel,
        grid=(m // tm,),
        in_specs=[pl.BlockSpec((tm, d), lambda i: (i, 0)), pl.BlockSpec((1, d), lambda i: (0, 0))],
        out_specs=pl.BlockSpec((tm, d), lambda i: (i, 0)),
        out_shape=jax.ShapeDtypeStruct((m, d), out_dtype),
        compiler_params=_cparams("parallel"),
        name="rmsnorm",
    )(x2d, g.reshape(1, d))


def _matmul_kernel(x_ref, w_ref, o_ref):
    o_ref[...] = jnp.dot(x_ref[...].astype(BF16), w_ref[...], preferred_element_type=F32)


def _matmul(x2d, w, tm, tn):
    m, k = x2d.shape
    n = w.shape[1]
    tm = min(tm, m)
    return pl.pallas_call(
        _matmul_kernel,
        grid=(n // tn, m // tm),
        in_specs=[pl.BlockSpec((tm, k), lambda j, i: (i, 0)), pl.BlockSpec((k, tn), lambda j, i: (0, j))],
        out_specs=pl.BlockSpec((tm, tn), lambda j, i: (i, j)),
        out_shape=jax.ShapeDtypeStruct((m, n), F32),
        compiler_params=_cparams("parallel", "parallel"),
        name="matmul",
    )(x2d, w)


D_A = N_HEADS_A * HEAD_DIM_A
D_KV = N_KV_A * HEAD_DIM_A
K_OFF = D_A
V_OFF = K_OFF + D_KV
QI_OFF = V_OFF + D_KV
KI_OFF = QI_OFF + N_IDX * D_IDX
W_OFF = KI_OFF + D_IDX
ATT_COLS = KI_OFF + LANES
_G_K = K_OFF // LANES
_G_V = V_OFF // LANES
_G_KI = KI_OFF // LANES


def _attn_proj_kernel(h_ref, w_ref, cos_ref, sa_ref, sb_ref, o_ref, kb_ref, vb_ref, ki2_ref):
    p = jnp.dot(h_ref[...].astype(BF16), w_ref[...], preferred_element_type=F32)
    cos, sa, sb = cos_ref[...], sa_ref[...], sb_ref[...]
    lane = lax.broadcasted_iota(jnp.int32, cos.shape, 1)
    for g in range(ATT_COLS // LANES):
        x = p[:, g * LANES:(g + 1) * LANES]
        if g == _G_V:
            o_ref[:, g * LANES:(g + 1) * LANES] = x
            vb_ref[...] = x.astype(BF16)
            continue
        y = x * cos + pltpu.roll(x, LANES - 32, 1) * sa + pltpu.roll(x, 32, 1) * sb
        if g == _G_KI:
            y = jnp.where(lane < D_IDX, y, x)
            ki2_ref[...] = jnp.where(lane < D_IDX, y, pltpu.roll(y, D_IDX, 1)).astype(BF16)
        if g == _G_K:
            kb_ref[...] = y.astype(BF16)
        o_ref[:, g * LANES:(g + 1) * LANES] = y


def _rope_tables(pos0, t):
    d = HEAD_DIM_A
    inv = ROPE_THETA ** (-jnp.arange(0, d, 2, dtype=F32) / d)
    ang = (pos0 + jnp.arange(t)).astype(F32)[:, None] * inv[None, :]
    cos, sin = jnp.cos(ang), jnp.sin(ang)
    zero = jnp.zeros_like(sin)
    cos_f = jnp.concatenate([cos] * 4, axis=-1)
    sin_a = jnp.concatenate([-sin, zero] * 2, axis=-1)
    sin_b = jnp.concatenate([zero, sin] * 2, axis=-1)
    return cos_f, sin_a, sin_b


def _attn_proj(h2d, w_att, tables, t, tm):
    m, k = h2d.shape
    nt = t // tm
    tab_spec = pl.BlockSpec((tm, LANES), lambda i: (i % nt, 0))
    lane_spec = pl.BlockSpec((tm, LANES), lambda i: (i, 0))
    lane_shape = jax.ShapeDtypeStruct((m, LANES), BF16)
    return pl.pallas_call(
        _attn_proj_kernel,
        grid=(m // tm,),
        in_specs=[pl.BlockSpec((tm, k), lambda i: (i, 0)), pl.BlockSpec((k, ATT_COLS), lambda i: (0, 0)),
                  tab_spec, tab_spec, tab_spec],
        out_specs=[pl.BlockSpec((tm, ATT_COLS), lambda i: (i, 0)), lane_spec, lane_spec, lane_spec],
        out_shape=[jax.ShapeDtypeStruct((m, ATT_COLS), F32), lane_shape, lane_shape, lane_shape],
        compiler_params=_cparams("parallel"),
        name="attn_proj",
    )(h2d, w_att, *tables)


I16 = jnp.int16
I16_MIN = -(2 ** 15)
I16_MAX = 2 ** 15 - 1
I16_ROWS = 16
I32_ROWS = 8
LOG2E = 1.4426950408889634


def _tree(parts, op):
    while len(parts) > 1:
        parts = [op(parts[i], parts[i + 1]) if i + 1 < len(parts) else parts[i] for i in range(0, len(parts), 2)]
    return parts[0]


def _fold_rows(x, rows, op):
    return _tree([x[g * rows:(g + 1) * rows] for g in range(x.shape[0] // rows)], op)


def _dsa_kernel(pa_ref, ki_ref, k_ref, vt_ref, o_ref, key_sc, dig_sc, *, nb, tq, tk, pos0, s_real, topk):
    qb = pl.program_id(1)
    tqb = tq // nb
    q0 = pos0 + qb * tqb
    n_adm = jnp.minimum(q0 + tqb, s_real)
    n_kt = (n_adm + tk - 1) // tk
    hd = HEAD_DIM_A
    group = N_HEADS_A // N_KV_A
    cols = group * tq

    q_in_block = lax.broadcasted_iota(jnp.int32, (1, tq), 1) % tqb
    q_pos = q0 + q_in_block
    key_limit = jnp.minimum((lax.shift_right_arithmetic(q_pos, CHUNK.bit_length() - 1) + 1) * CHUNK, s_real)
    key_iota = lax.broadcasted_iota(jnp.int32, (tk, tq), 0)
    low_half = lax.broadcasted_iota(jnp.int32, (tq, LANES), 1) < hd
    halves = (low_half, jnp.logical_not(low_half))
    row_batch = lax.broadcasted_iota(jnp.int32, (tq, LANES), 0) // tqb

    def by_batch(x):
        if nb == 1:
            return x
        return jnp.concatenate([jnp.where(row_batch == i, x, 0.0) for i in range(nb)], axis=1)

    def key_rows(ref, start):
        return jnp.concatenate([ref[i, pl.ds(start, tk), :] for i in range(nb)], axis=1)

    pa = jnp.concatenate([pa_ref[i] for i in range(nb)], axis=0)
    qi = jnp.concatenate(
        [by_batch(jnp.where(halves[h % 2],
                            pa[:, QI_OFF + (h // 2) * LANES:QI_OFF + (h // 2 + 1) * LANES] * (D_IDX ** -0.5), 0.0))
         for h in range(N_IDX)], axis=0).astype(BF16)
    w_t = pa[:, KI_OFF:KI_OFF + LANES].T * (N_IDX ** -0.5)
    w_rows = [w_t[D_IDX + h:D_IDX + h + 1, :] for h in range(N_IDX)]

    def score_body(kt, carry):
        start = pl.multiple_of(kt * tk, tk)
        s = lax.dot_general(key_rows(ki_ref, start), qi, (((1,), (1,)), ((), ())), preferred_element_type=F32)
        score = jnp.zeros((tk, tq), F32)
        for h in range(N_IDX):
            score = score + jnp.maximum(s[:, h * tq:(h + 1) * tq], 0.0) * w_rows[h]
        score = score + 0.0
        score = jnp.where(start + key_iota < key_limit, score, NEG_INF)
        bits = pltpu.bitcast(score, jnp.int32)
        key = jnp.where(bits < 0, bits ^ 0x7FFFFFFF, bits)
        key_sc[kt] = key
        dig_sc[kt] = lax.shift_right_arithmetic(key, 16).astype(I16)
        return carry

    lax.fori_loop(0, n_kt, score_body, 0)

    def count(pred_fn):
        def body(kt, acc):
            hit = jnp.where(pred_fn(key_sc[kt], kt), 1, 0)
            return acc + _fold_rows(hit, I32_ROWS, jnp.add)

        acc = lax.fori_loop(0, n_kt, body, jnp.zeros((I32_ROWS, tq), jnp.int32))
        return jnp.sum(acc, axis=0, keepdims=True)

    def count16(ref, c_of_tile, cmp):
        def body(kt, acc):
            hit = jnp.where(cmp(ref[kt], c_of_tile(kt).astype(I16)), jnp.ones((), I16), jnp.zeros((), I16))
            return acc + _fold_rows(hit, I16_ROWS, jnp.add)

        acc = lax.fori_loop(0, n_kt, body, jnp.zeros((I16_ROWS, tq), I16))
        return jnp.sum(acc.astype(jnp.int32), axis=0, keepdims=True)

    def kth_digit(ref, kk):
        def body(i, prefix):
            cand = prefix | lax.shift_left(jnp.int32(1), 15 - i)
            cnt = count16(ref, lambda kt: cand + I16_MIN, lax.ge)
            return jnp.where(cnt >= kk, cand, prefix)

        return lax.fori_loop(0, 16, body, jnp.zeros((1, tq), jnp.int32))

    hi_thr = kth_digit(dig_sc, topk) + I16_MIN
    above = jnp.where(hi_thr < I16_MAX, count16(dig_sc, lambda kt: jnp.minimum(hi_thr + 1, I16_MAX), lax.ge), 0)

    def lo_body(kt, carry):
        key = key_sc[kt]
        in_bucket = lax.shift_right_arithmetic(key, 16) == hi_thr
        dig_sc[kt] = jnp.where(in_bucket, (key & 0xFFFF) + I16_MIN, I16_MIN).astype(I16)
        return carry

    lax.fori_loop(0, n_kt, lo_body, 0)
    lo_thr = kth_digit(dig_sc, topk - above)
    thr = jnp.maximum(lax.shift_left(hi_thr, 16) | lo_thr, NEG_KEY)
    cnt_gt = count(lambda key, kt: key > thr)
    need = jnp.where(thr != NEG_KEY, topk - cnt_gt, 0).astype(F32)
    ti = lax.broadcasted_iota(jnp.int32, (tk, tk), 0)
    tj = lax.broadcasted_iota(jnp.int32, (tk, tk), 1)
    below = jnp.where(ti > tj, 1.0, 0.0).astype(BF16)
    ones_row = jnp.ones((I32_ROWS, tk), BF16)

    qs = []
    for j in range(N_KV_A):
        parts = []
        for g in range(group):
            h = j * group + g
            grp = pa[:, (h // 2) * LANES:(h // 2 + 1) * LANES] * (hd ** -0.5)
            if h % 2 != j:
                grp = pltpu.roll(grp, hd, 1)
            parts.append(by_batch(jnp.where(halves[j], grp, 0.0)))
        qs.append(jnp.concatenate(parts, axis=0).astype(BF16))

    col_batch = (lax.broadcasted_iota(jnp.int32, (1, cols), 1) % tq) // tqb

    def attn_body(kt, carry):
        start = pl.multiple_of(kt * tk, tk)
        key = key_sc[kt]
        tied = key == thr
        tied_b = jnp.where(tied, 1.0, 0.0).astype(BF16)
        before = jnp.dot(below, tied_b, preferred_element_type=F32) + carry[N_KV_A]
        tied_seen = carry[N_KV_A] + jnp.dot(ones_row, tied_b, preferred_element_type=F32)[0:1]
        sel = (key > thr) | (tied & (before < need))
        bias = jnp.where(sel, 0.0, NEG_INF).astype(F32)
        bias = jnp.concatenate([bias] * group, axis=1)
        k_tile = key_rows(k_ref, start)
        heads = range(N_KV_A)
        logits = [lax.dot_general(k_tile, qs[j], (((1,), (1,)), ((), ())), preferred_element_type=F32) * LOG2E
                  + bias for j in heads]
        m_new = [jnp.maximum(carry[j][0], jnp.max(_fold_rows(logits[j], I32_ROWS, jnp.maximum), axis=0,
                                                   keepdims=True)) for j in heads]
        alpha = [jnp.exp2(carry[j][0] - m_new[j]) for j in heads]
        p = [jnp.exp2(logits[j] - m_new[j]) for j in heads]
        l_new = [carry[j][1] * alpha[j] + jnp.sum(_fold_rows(p[j], I32_ROWS, jnp.add), axis=0, keepdims=True)
                 for j in heads]
        acc = []
        for j in heads:
            v_t = jnp.concatenate([vt_ref[i, j * hd:(j + 1) * hd, pl.ds(start, tk)] for i in range(nb)], axis=0)
            pv_all = jnp.dot(v_t, p[j].astype(BF16), preferred_element_type=F32)
            pv = pv_all[:hd]
            for i in range(1, nb):
                pv = jnp.where(col_batch == i, pv_all[i * hd:(i + 1) * hd], pv)
            acc.append(carry[j][2] * alpha[j] + pv)
        return tuple((m_new[j], l_new[j], acc[j]) for j in heads) + (tied_seen,)

    init = tuple((jnp.full((1, cols), 0.1 * NEG_INF, F32), jnp.zeros((1, cols), F32),
                  jnp.zeros((hd, cols), F32)) for _ in range(N_KV_A)) + (jnp.zeros((1, tq), F32),)
    res = lax.fori_loop(0, n_kt, attn_body, init)
    outs = [acc / l_i for _, l_i, acc in res[:N_KV_A]]
    for grp_i in range(N_HEADS_A // 2):
        pair = []
        for h in (2 * grp_i, 2 * grp_i + 1):
            j, g = h // group, h % group
            pair.append(outs[j][:, g * tq:(g + 1) * tq])
        o_t = jnp.concatenate(pair, axis=0).T
        for i in range(nb):
            o_ref[i, :, grp_i * LANES:(grp_i + 1) * LANES] = o_t[i * tqb:(i + 1) * tqb]


def _dsa(pa, ki_all, k_all, vt_all, *, pos0, s_real, nb, tq, tk, kv_buffers):
    b, t, _ = pa.shape
    tqb = tq // nb
    t_pad = _round_up(t, tqb)
    pa = jnp.pad(pa, ((0, 0), (0, t_pad - t), (0, 0)))
    s_pad = ki_all.shape[1]
    topk = min(TOPK_MAX, s_real // 4)
    kern = functools.partial(_dsa_kernel, nb=nb, tq=tq, tk=tk, pos0=pos0, s_real=s_real, topk=topk)
    kv_mode = pl.Buffered(kv_buffers)
    kv_spec = pl.BlockSpec((nb, s_pad, LANES), lambda bi, qi: (bi, 0, 0), pipeline_mode=kv_mode)
    out = pl.pallas_call(
        kern,
        grid=(b // nb, t_pad // tqb),
        in_specs=[pl.BlockSpec((nb, tqb, ATT_COLS), lambda bi, qi: (bi, qi, 0)), kv_spec, kv_spec,
                  pl.BlockSpec((nb, LANES, s_pad), lambda bi, qi: (bi, 0, 0), pipeline_mode=kv_mode)],
        out_specs=pl.BlockSpec((nb, tqb, D_A), lambda bi, qi: (bi, qi, 0)),
        out_shape=jax.ShapeDtypeStruct((b, t_pad, D_A), F32),
        scratch_shapes=[pltpu.VMEM((s_pad // tk, tk, tq), jnp.int32), pltpu.VMEM((s_pad // tk, tk, tq), I16)],
        compiler_params=_cparams("parallel", "arbitrary"),
        name="dsa",
    )(pa, ki_all, k_all, vt_all)
    return out[:, :t]


D_R = N_HEADS_R * HEAD_DIM_R
D_SHIFT = 3 * D_R + 2 * D_LORA
N_PAIR = D_R // LANES
N_VEC = 8


def _split2(x):
    hi = x.astype(BF16)
    lo = (x - hi.astype(F32)).astype(BF16)
    return hi, lo


def _mm(a, b, ca, cb):
    return lax.dot_general(a, b, (((ca,), (cb,)), ((), ())), preferred_element_type=F32)


def _rwkv_kernel(sh_ref, sh0_ref, mu_ref, vec_ref, wl_ref, s0_ref, y_ref, sout_ref, st_sc, row_sc, *, bb, c):
    ci = pl.program_id(1)
    n = HEAD_DIM_R
    c2 = 2 * c

    @pl.when(ci == 0)
    def _():
        st_sc[...] = s0_ref[...]
        row_sc[...] = sh0_ref[...]

    head0 = lax.broadcasted_iota(jnp.int32, (c, LANES), 1) < n
    r2 = lax.broadcasted_iota(jnp.int32, (c2, c2), 0)
    q2 = lax.broadcasted_iota(jnp.int32, (c2, c2), 1)
    same_head = (r2 // c) == (q2 // c)
    strict = same_head & (r2 > q2)
    incl = same_head & (r2 >= q2)
    eye2 = jnp.where(r2 == q2, 1.0, 0.0).astype(F32)
    rc = lax.broadcasted_iota(jnp.int32, (c, c), 0)
    cc = lax.broadcasted_iota(jnp.int32, (c, c), 1)
    tri = jnp.where(rc >= cc, 1.0, 0.0).astype(BF16)
    tri3 = jnp.concatenate([tri, tri, tri], axis=1)
    la = lax.broadcasted_iota(jnp.int32, (LANES, LANES), 0)
    lb = lax.broadcasted_iota(jnp.int32, (LANES, LANES), 1)
    seg = jnp.where((la // n) == (lb // n), 1.0, 0.0).astype(BF16)
    seg2 = jnp.concatenate([seg, seg], axis=0)
    first_row = lax.broadcasted_iota(jnp.int32, (c, D_SHIFT), 0) == 0

    def seg_sum(x):
        xh, xl = _split2(x)
        return jnp.dot(jnp.concatenate([xh, xl], axis=1), seg2, preferred_element_type=F32)

    def vec(i, sl=slice(None)):
        return vec_ref[i:i + 1, sl]

    n_double = 0
    while (2 << n_double) < c:
        n_double += 1

    def stack(zz):
        return jnp.concatenate([jnp.where(head0, zz, 0.0), jnp.where(head0, 0.0, zz)], axis=0)

    chains = [(bi, p) for bi in range(bb) for p in range(N_PAIR)]
    states = [st_sc[bi, p] for bi, p in chains]

    prep = []
    for bi in range(bb):
        sh = sh_ref[bi]
        prev = jnp.where(first_row, row_sc[bi], pltpu.roll(sh, 1, 0))
        x = sh + (prev - sh) * mu_ref[...]
        r = x[:, :D_R]
        k = x[:, D_R:2 * D_R]
        v = x[:, 2 * D_R:3 * D_R]
        xl = x[:, 3 * D_R:]
        lora_in = jnp.where(head0, jnp.tanh(xl), xl)
        wa = jnp.dot(lora_in.astype(BF16), wl_ref[...], preferred_element_type=F32)
        w_raw = vec(0) + wa[:, :D_R]
        a = jax.nn.sigmoid(vec(1) + wa[:, D_R:])
        lw = -jnp.exp(-jax.nn.softplus(-w_raw) - 0.5)
        kkv = k * vec(2)
        kk2 = kkv * kkv
        ss = jnp.concatenate([seg_sum(kk2[:, p * LANES:(p + 1) * LANES]) for p in range(N_PAIR)], axis=1)
        kk = kkv / jnp.maximum(jnp.sqrt(ss), 1e-12)
        k2 = k * (1.0 + (a - 1.0) * vec(3))
        l_hi = lw.astype(BF16)
        l_r1 = lw - l_hi.astype(F32)
        l_mid = l_r1.astype(BF16)
        l_lo = (l_r1 - l_mid.astype(F32)).astype(BF16)
        cum = jnp.dot(tri3, jnp.concatenate([l_hi, l_mid, l_lo], axis=0), preferred_element_type=F32)
        p_fwd = jnp.exp(cum)
        p_inv = jnp.exp(-cum)
        prep.append(dict(v=v, p_fwd=p_fwd, at=-kk * jnp.exp(cum - lw), bt=kk * a * p_inv, kt=k2 * p_inv,
                         rt=r * p_fwd, rk_prod=r * k2 * vec(4)))
    for bi in range(bb):
        row_sc[bi] = sh_ref[bi, c - 1:c, :]

    ops = []
    for bi, p in chains:
        sl = slice(p * LANES, (p + 1) * LANES)
        d = prep[bi]
        b_st, k_st, v_st = stack(d["bt"][:, sl]), stack(d["kt"][:, sl]), stack(d["v"][:, sl])
        ar = jnp.concatenate([stack(d["at"][:, sl]), stack(d["rt"][:, sl])], axis=0).astype(BF16)
        bk = jnp.concatenate([b_st, k_st], axis=0).astype(BF16)
        p_last = d["p_fwd"][c - 1:c, sl]
        ops.append(dict(ar=ar, bk=bk, v_b=v_st.astype(BF16), p_last=p_last,
                        b_dec=(b_st * p_last).astype(BF16), k_dec=(k_st * p_last).astype(BF16)))
    grams = [_mm(o["ar"], o["bk"], 1, 1) for o in ops]
    l_ab = [jnp.where(strict, g[:c2, :c2], 0.0) for g in grams]
    l_ak = [jnp.where(strict, g[:c2, c2:], 0.0).astype(BF16) for g in grams]
    m_rb = [jnp.where(incl, g[c2:, :c2], 0.0).astype(BF16) for g in grams]
    m_rk = [jnp.where(incl, g[c2:, c2:], 0.0).astype(BF16) for g in grams]
    inv = [eye2 + l for l in l_ab]
    lp = l_ab
    for _ in range(n_double):
        lpb = [l.astype(BF16) for l in lp]
        lp = [_mm(l, l, 1, 0) for l in lpb]
        inv = [x + _mm(x.astype(BF16), l.astype(BF16), 1, 0) for x, l in zip(inv, lp)]

    s_b = [s.astype(BF16) for s in states]
    z = [_mm(o["ar"][:c2], sb, 1, 1) + _mm(la, o["v_b"], 1, 0) for o, sb, la in zip(ops, s_b, l_ak)]
    u_b = [_mm(x.astype(BF16), zz.astype(BF16), 1, 0).astype(BF16) for x, zz in zip(inv, z)]
    y_st = [_mm(o["ar"][c2:], sb, 1, 1) + _mm(mb, u, 1, 0) + _mm(mk, o["v_b"], 1, 0)
            for o, sb, mb, mk, u in zip(ops, s_b, m_rb, m_rk, u_b)]
    new_states = [s * o["p_last"] + _mm(u, o["b_dec"], 0, 0) + _mm(o["v_b"], o["k_dec"], 0, 0)
                  for s, o, u in zip(states, ops, u_b)]
    for (bi, p), s_new, ys in zip(chains, new_states, y_st):
        sl = slice(p * LANES, (p + 1) * LANES)
        st_sc[bi, p] = s_new
        y = ys[:c] + ys[c:]
        dev = y - seg_sum(y) * (1.0 / n)
        var = seg_sum(dev * dev) * (1.0 / n)
        yn = dev * lax.rsqrt(var + GN_EPS) * vec(5, sl) + vec(6, sl)
        y_ref[bi, :, sl] = yn + seg_sum(prep[bi]["rk_prod"][:, sl]) * prep[bi]["v"][:, sl]

    @pl.when(ci == pl.num_programs(1) - 1)
    def _():
        sout_ref[...] = st_sc[...]


def _rwkv(sh, shift0, mu, w0, w2, a0, a2, k_k, k_a, r_k, ln_w, ln_b, state0, *, bb, c):
    b, t, _ = sh.shape
    n = HEAD_DIM_R
    eye = jnp.eye(2, dtype=F32)
    s0 = jnp.einsum('bphij,hg->bphigj', state0.reshape(b, N_PAIR, 2, n, n), eye).reshape(b, N_PAIR, LANES, LANES)
    vecs = jnp.stack([w0, a0, k_k, k_a, r_k.reshape(D_R), ln_w, ln_b, jnp.zeros_like(w0)])
    zero = jnp.zeros((D_LORA, D_R), F32)
    w_lora = jnp.concatenate([jnp.concatenate([w2, zero], axis=1), jnp.concatenate([zero, a2], axis=1)],
                             axis=0).astype(BF16)
    full = lambda shape: pl.BlockSpec(shape, lambda bi, ci: (0,) * len(shape))
    st_spec = pl.BlockSpec((bb, N_PAIR, LANES, LANES), lambda bi, ci: (bi, 0, 0, 0))
    y, s_out = pl.pallas_call(
        functools.partial(_rwkv_kernel, bb=bb, c=c),
        grid=(b // bb, t // c),
        in_specs=[pl.BlockSpec((bb, c, D_SHIFT), lambda bi, ci: (bi, ci, 0)),
                  pl.BlockSpec((bb, 1, D_SHIFT), lambda bi, ci: (bi, 0, 0)),
                  full((1, D_SHIFT)), full((N_VEC, D_R)), full((2 * D_LORA, 2 * D_R)), st_spec],
        out_specs=[pl.BlockSpec((bb, c, D_R), lambda bi, ci: (bi, ci, 0)), st_spec],
        out_shape=[jax.ShapeDtypeStruct((b, t, D_R), F32),
                   jax.ShapeDtypeStruct((b, N_PAIR, LANES, LANES), F32)],
        scratch_shapes=[pltpu.VMEM((bb, N_PAIR, LANES, LANES), F32), pltpu.VMEM((bb, 1, D_SHIFT), F32)],
        compiler_params=_cparams("parallel", "arbitrary"),
        name="rwkv",
    )(sh, shift0, mu.reshape(1, D_SHIFT), vecs, w_lora, s0)
    s_t = jnp.einsum('bphigj,hg->bphij', s_out.reshape(b, N_PAIR, 2, n, 2, n), eye)
    return y, s_t.reshape(b, N_HEADS_R, n, n)


def _mem_attn_kernel(q_ref, mk_ref, mv_ref, o_ref):
    scale = HEAD_DIM_M ** -0.5
    for h in range(N_HEADS_M):
        sl = slice(h * HEAD_DIM_M, (h + 1) * HEAD_DIM_M)
        q = q_ref[0, :, sl].astype(BF16)
        logits = lax.dot_general(q, mk_ref[0, :, sl], (((1,), (1,)), ((), ())), preferred_element_type=F32) * scale
        e = jnp.exp(logits - jnp.max(logits, axis=1, keepdims=True))
        p = e / jnp.sum(e, axis=1, keepdims=True)
        o_ref[0, :, sl] = jnp.dot(p.astype(BF16), mv_ref[0, :, sl], preferred_element_type=F32)


def _mem_attn(qm, mk, mv, tm):
    b, t, d = qm.shape
    nm = mk.shape[1]
    return pl.pallas_call(
        _mem_attn_kernel,
        grid=(b, t // tm),
        in_specs=[pl.BlockSpec((1, tm, d), lambda bi, i: (bi, i, 0)),
                  pl.BlockSpec((1, nm, d), lambda bi, i: (bi, 0, 0)),
                  pl.BlockSpec((1, nm, d), lambda bi, i: (bi, 0, 0))],
        out_specs=pl.BlockSpec((1, tm, d), lambda bi, i: (bi, i, 0)),
        out_shape=jax.ShapeDtypeStruct((b, t, d), F32),
        compiler_params=_cparams("parallel", "parallel"),
        name="mem_attn",
    )(qm, mk, mv)


def _merge_kernel(oa_ref, yr_ref, om_ref, zg_ref, x_ref, wa_ref, wr_ref, wm_ref, wo_ref, g_ref, xo_ref, ho_ref):
    d_br = oa_ref.shape[1]
    d = x_ref.shape[1]

    def branch(o, z, w_ref):
        return jnp.dot((o * jax.nn.silu(z)).astype(BF16), w_ref[...], preferred_element_type=F32)

    pa = branch(oa_ref[...], zg_ref[:, 0:d_br], wa_ref)
    pr = branch(yr_ref[...], zg_ref[:, d_br:2 * d_br], wr_ref)
    pm = branch(om_ref[...], zg_ref[:, 2 * d_br:3 * d_br], wm_ref)
    g0 = 3 * d_br
    merged = (jax.nn.sigmoid(zg_ref[:, g0:g0 + d]) * pa
              + jax.nn.sigmoid(zg_ref[:, g0 + d:g0 + 2 * d]) * pr
              + jax.nn.sigmoid(zg_ref[:, g0 + 2 * d:g0 + 3 * d]) * pm)
    xn = x_ref[...] + jnp.dot(merged.astype(BF16), wo_ref[...], preferred_element_type=F32)
    xo_ref[...] = xn
    y = xn * lax.rsqrt(jnp.mean(xn * xn, axis=-1, keepdims=True) + RMS_EPS)
    ho_ref[...] = y * g_ref[...]


def _merge(oa, yr, om, zg, x2d, wa, wr, wm, wo, g_next, tm):
    m, d = x2d.shape
    d_br = oa.shape[1]
    tm = min(tm, m)
    row = lambda w: pl.BlockSpec((tm, w), lambda i: (i, 0))
    full = lambda a: pl.BlockSpec(a.shape, lambda i: (0, 0))
    g2 = g_next.reshape(1, d)
    return pl.pallas_call(
        _merge_kernel,
        grid=(m // tm,),
        in_specs=[row(d_br), row(d_br), row(d_br), row(zg.shape[1]), row(d),
                  full(wa), full(wr), full(wm), full(wo), full(g2)],
        out_specs=[row(d), row(d)],
        out_shape=[jax.ShapeDtypeStruct((m, d), F32), jax.ShapeDtypeStruct((m, d), F32)],
        compiler_params=_cparams("parallel"),
        name="merge",
    )(oa, yr, om, zg, x2d, wa, wr, wm, wo, g2)


def _round_up(x, m):
    return (x + m - 1) // m * m


IN_SIZES = (D_A, D_KV, D_KV, N_IDX * D_IDX, D_IDX, N_IDX, D_A, D_SHIFT, D_R, N_HEADS_M * HEAD_DIM_M,
            N_HEADS_M * HEAD_DIM_M)
IN_OFFS = tuple(int(o) for o in np.cumsum((0,) + IN_SIZES))


def _repack_kernel(w_ref, att_ref, sh_ref, qm_ref, zg_ref):
    col = lambda i: w_ref[:, IN_OFFS[i]:IN_OFFS[i + 1]]
    rows = w_ref.shape[0]
    att_ref[:, :W_OFF + N_IDX] = w_ref[:, :IN_OFFS[6]].astype(BF16)
    att_ref[:, W_OFF + N_IDX:] = jnp.zeros((rows, ATT_COLS - W_OFF - N_IDX), BF16)
    sh_ref[...] = col(7).astype(BF16)
    qm_ref[...] = col(9).astype(BF16)
    zg_ref[:, :D_A] = col(6).astype(BF16)
    zg_ref[:, D_A:D_A + D_R] = col(8).astype(BF16)
    zg_ref[:, D_A + D_R:D_A + D_R + IN_SIZES[10]] = col(10).astype(BF16)
    zg_ref[:, D_A + D_R + IN_SIZES[10]:] = w_ref[:, IN_OFFS[11]:].astype(BF16)


def _split_w_in(w_in):
    d_model, d_in = w_in.shape
    n_zg = D_A + D_R + IN_SIZES[10] + d_in - IN_OFFS[11]
    tr = 128
    widths = (ATT_COLS, D_SHIFT, IN_SIZES[9], n_zg)
    return tuple(pl.pallas_call(
        _repack_kernel,
        grid=(d_model // tr,),
        in_specs=[pl.BlockSpec((tr, d_in), lambda i: (i, 0))],
        out_specs=[pl.BlockSpec((tr, w), lambda i: (i, 0)) for w in widths],
        out_shape=[jax.ShapeDtypeStruct((d_model, w), BF16) for w in widths],
        compiler_params=_cparams("parallel"),
        name="repack",
    )(w_in))


def _layer(x2d, h2d, b, t, pos0, past_k, past_v, past_ki, wkv0, shift0, mem_k, mem_v, lw, g_next, cfg):
    (w_att, w_sh, w_qm, w_zg, mu_shift, w0, w2, a0, a2, k_k, k_a, r_k, ln_w, ln_b, w_br_a, w_br_r, w_br_m,
     w_out) = lw
    tm = cfg["tm"]
    d_a = N_HEADS_A * HEAD_DIM_A
    d_kv = N_KV_A * HEAD_DIM_A
    d_r = N_HEADS_R * HEAD_DIM_R
    d_m = N_HEADS_M * HEAD_DIM_M

    tables = _rope_tables(pos0, t)
    pa, kb, vb, ki2 = _attn_proj(h2d, w_att, tables, t, min(tm, t))
    pa = pa.reshape(b, t, ATT_COLS)
    sh = _matmul(h2d, w_sh, tm, w_sh.shape[1]).reshape(b, t, -1)
    qm = _matmul(h2d, w_qm, tm, w_qm.shape[1]).reshape(b, t, d_m)
    zg = _matmul(h2d, w_zg, tm, cfg["tn_zg"])

    k_new = pa[..., K_OFF:K_OFF + D_KV].reshape(b, t, N_KV_A, HEAD_DIM_A)
    v_new = pa[..., V_OFF:V_OFF + D_KV].reshape(b, t, N_KV_A, HEAD_DIM_A)
    ki_new = pa[..., KI_OFF:KI_OFF + D_IDX]
    s_real = pos0 + t
    tq, tk = cfg["tq"], cfg["tk"]
    s_pad = _round_up(s_real, tk)

    def with_past(past, new):
        full = jnp.concatenate([past.reshape(b, pos0, LANES).astype(BF16), new.reshape(b, t, LANES)], axis=1)
        return jnp.pad(full, ((0, 0), (0, s_pad - s_real), (0, 0)))

    vt_all = with_past(past_v, vb).transpose(0, 2, 1)
    o_a = _dsa(pa, with_past(jnp.concatenate([past_ki, past_ki], axis=-1), ki2), with_past(past_k, kb),
               vt_all, pos0=pos0, s_real=s_real, nb=cfg["nb"], tq=tq, tk=tk, kv_buffers=cfg["kv_buffers"]).reshape(b * t, d_a)

    y_r, wkv_t = _rwkv(sh, shift0, mu_shift, w0, w2, a0, a2, k_k, k_a, r_k, ln_w, ln_b, wkv0,
                       bb=cfg["bb"], c=CHUNK)
    y_r = y_r.reshape(b * t, d_r)

    o_m = _mem_attn(qm, mem_k, mem_v, min(tm, t)).reshape(b * t, d_m)

    x_new, h_next = _merge(o_a, y_r, o_m, zg, x2d, w_br_a, w_br_r, w_br_m, w_out, g_next, cfg["tm_merge"])
    return x_new, h_next, k_new, v_new, ki_new, wkv_t, sh[:, -1:]


PROMPT_CFG = dict(tm=512, tn_zg=2304, nb=1, tq=512, tk=512, kv_buffers=1, bb=2, tm_merge=512)
SAMPLE_CFG = dict(tm=512, tn_zg=2304, nb=2, tq=128, tk=1152, kv_buffers=2, bb=2, tm_merge=512)


def kernel(x_prompt, x_sample, mem_prompt, cache_k, cache_v, cache_kidx, state_wkv, state_shift, cache_mem_k,
           cache_mem_v, norm_g, w_in, mu_shift, w0, w2, a0, a2, k_k, k_a, r_k, ln_x_w, ln_x_b, mem_norm_g,
           w_mem_kv, w_br_a, w_br_r, w_br_m, w_out, final_norm_g):
    depth = w_in.shape[0]
    bp, tp, d = x_prompt.shape
    bs, ts, _ = x_sample.shape
    n_mem = mem_prompt.shape[1]
    d_m = N_HEADS_M * HEAD_DIM_M
    past = cache_k.shape[2]

    xp = x_prompt.reshape(bp * tp, d)
    xs = x_sample.reshape(bs * ts, d)
    hp = _rmsnorm(xp, norm_g[0], F32, 512)
    hs = _rmsnorm(xs, norm_g[0], F32, 512)
    mem2d = mem_prompt.reshape(bp * n_mem, d)

    empty_kv = jnp.zeros((bp, 0, N_KV_A, HEAD_DIM_A), F32)
    empty_ki = jnp.zeros((bp, 0, D_IDX), F32)
    zero_state = jnp.zeros((bp, N_HEADS_R, HEAD_DIM_R, HEAD_DIM_R), F32)
    zero_shift = jnp.zeros((bp, 1, mu_shift.shape[1]), F32)

    outs_p = [[] for _ in range(7)]
    outs_s = [[] for _ in range(5)]
    for l in range(depth):
        g_next = norm_g[l + 1] if l + 1 < depth else final_norm_g
        lw = _split_w_in(w_in[l]) + (mu_shift[l], w0[l], w2[l], a0[l], a2[l], k_k[l], k_a[l], r_k[l], ln_x_w[l],
                                     ln_x_b[l], w_br_a[l].astype(BF16), w_br_r[l].astype(BF16),
                                     w_br_m[l].astype(BF16), w_out[l].astype(BF16))
        mem_h = _rmsnorm(mem2d, mem_norm_g[l], F32, 512)
        mkv = _matmul(mem_h, w_mem_kv[l].astype(BF16), 512, 2 * d_m).reshape(bp, n_mem, 2 * d_m)
        mk, mv = mkv[..., :d_m], mkv[..., d_m:]
        xp, hp, k_n, v_n, ki_n, wkv_n, sh_n = _layer(
            xp, hp, bp, tp, 0, empty_kv, empty_kv, empty_ki, zero_state, zero_shift,
            mk.astype(BF16), mv.astype(BF16), lw, g_next, PROMPT_CFG)
        for lst, val in zip(outs_p, (k_n, v_n, ki_n, wkv_n, sh_n,
                                     mk.reshape(bp, n_mem, N_HEADS_M, HEAD_DIM_M),
                                     mv.reshape(bp, n_mem, N_HEADS_M, HEAD_DIM_M))):
            lst.append(val)
        xs, hs, k_n, v_n, ki_n, wkv_n, sh_n = _layer(
            xs, hs, bs, ts, past, cache_k[l], cache_v[l], cache_kidx[l], state_wkv[l], state_shift[l],
            cache_mem_k[l].reshape(bs, n_mem, d_m).astype(BF16), cache_mem_v[l].reshape(bs, n_mem, d_m).astype(BF16),
            lw, g_next, SAMPLE_CFG)
        for lst, val in zip(outs_s, (k_n, v_n, ki_n, wkv_n, sh_n)):
            lst.append(val)

    y_prompt = hp.reshape(bp, tp, d)
    y_sample = hs.reshape(bs, ts, d)
    return (y_prompt, y_sample) + tuple(jnp.stack(o) for o in outs_p) + tuple(jnp.stack(o) for o in outs_s)
```
